```python
import math
import jax
import jax.numpy as jnp
from jax import lax
import numpy as np

D_MODEL = 2048
BATCH = 2
SEQ = 4096
DEPTH = 4

GRID_W = 64
CTX_LEN = 256
EPS = 1e-6
N_BRANCH = 3
N_MOD = 6
MIX_W = D_MODEL // 2

ATT_HEAD_DIM = 128
ATT_HEADS = MIX_W // ATT_HEAD_DIM
ATT_KV_HEADS = 2
ATT_GROUP = ATT_HEADS // ATT_KV_HEADS
ATT_Q = ATT_HEADS * ATT_HEAD_DIM
ATT_KV = ATT_KV_HEADS * ATT_HEAD_DIM
ATT_BLOCK = 128
ROPE_THETA = 10000.0
ROPE_PAIRS = ATT_HEAD_DIM // 4

GLA_HEADS = 4
GLA_DV = MIX_W // GLA_HEADS
GLA_DK = GLA_DV // 2
GLA_V = GLA_HEADS * GLA_DV
GLA_QK = GLA_HEADS * GLA_DK
GLA_LOWRANK = 16
GLA_TAU = 16.0
GLA_CHUNK = 64

SSD_HEAD_DIM = 64
SSD_HEADS = MIX_W // SSD_HEAD_DIM
SSD_GROUPS = 2
SSD_HG = SSD_HEADS // SSD_GROUPS
SSD_STATE = 128
SSD_INNER = SSD_HEADS * SSD_HEAD_DIM
SSD_BC = SSD_GROUPS * SSD_STATE
SSD_XBC = SSD_INNER + 2 * SSD_BC
SSD_CONV = 5
SSD_CHUNK = 64

D_FF = 4 * D_MODEL

IN_SPLITS = (ATT_Q, ATT_KV, ATT_KV, GLA_QK, GLA_QK, GLA_V, GLA_V, 2 * GLA_LOWRANK,
             SSD_INNER, SSD_XBC, 2 * SSD_HEADS, N_BRANCH * D_MODEL)
IN_COLS = sum(IN_SPLITS)
IN_OFFSETS = tuple(int(v) for v in np.cumsum(IN_SPLITS)[:-1])

kernel_name = 'hybrid_gqa_gla_ssd_prefix_dit'


def _rms(x, g):
    xf = x.astype(jnp.float32)
    y = xf * lax.rsqrt(jnp.mean(xf * xf, axis=-1, keepdims=True) + EPS)
    return (y * g.astype(jnp.float32)).astype(x.dtype)


def _modulate(x, g, shift, scale):
    return _rms(x, g) * (1.0 + scale) + shift


def _flip(t):
    return jnp.flip(t, axis=1)


def _heads(t, n, d):
    return t.reshape(t.shape[:-1] + (n, d))


def _axial_rope_tables(n_tokens):
    rows = n_tokens // GRID_W
    row = jnp.repeat(jnp.arange(rows, dtype=jnp.float32), GRID_W)
    col = jnp.tile(jnp.arange(GRID_W, dtype=jnp.float32), rows)
    inv = jnp.exp(-math.log(ROPE_THETA) * jnp.arange(ROPE_PAIRS, dtype=jnp.float32) / ROPE_PAIRS)
    ar = row[:, None] * inv
    ac = col[:, None] * inv
    ang = jnp.concatenate([ar, ar, ac, ac], axis=-1)[:, None, :]
    return jnp.cos(ang), jnp.sin(ang)


def _apply_rope(x, cos, sin):
    xf = x.astype(jnp.float32)
    xs = xf.reshape(xf.shape[:-1] + (2, 2, ROPE_PAIRS))
    rot = jnp.stack([-xs[..., 1, :], xs[..., 0, :]], axis=-2).reshape(xf.shape)
    return (xf * cos + rot * sin).astype(x.dtype)


def _group_q(t):
    return t.reshape(t.shape[:2] + (ATT_KV_HEADS, ATT_GROUP, ATT_HEAD_DIM))


def _sdpa_gqa(q, k, v):
    s = jnp.einsum('bqgrd,bsgd->bgrqs', q, k, preferred_element_type=jnp.float32) * (ATT_HEAD_DIM ** -0.5)
    p = jax.nn.softmax(s, axis=-1).astype(v.dtype)
    return jnp.einsum('bgrqs,bsgd->bqgrd', p, v)


def _attention(q, k, v, qc, kc, vc, q_gain, k_gain, cos, sin):
    B, T, _ = q.shape
    q = _apply_rope(_rms(_heads(q, ATT_HEADS, ATT_HEAD_DIM), q_gain), cos, sin)
    k = _apply_rope(_rms(_heads(k, ATT_KV_HEADS, ATT_HEAD_DIM), k_gain), cos, sin)
    v = _heads(v, ATT_KV_HEADS, ATT_HEAD_DIM)
    qc = _rms(_heads(qc, ATT_HEADS, ATT_HEAD_DIM), q_gain)
    kc = _rms(_heads(kc, ATT_KV_HEADS, ATT_HEAD_DIM), k_gain)
    vc = _heads(vc, ATT_KV_HEADS, ATT_HEAD_DIM)
    oc = _sdpa_gqa(_group_q(qc), kc, vc).reshape(qc.shape[:2] + (ATT_Q,))
    k_all = jnp.concatenate([kc, k], axis=1)
    v_all = jnp.concatenate([vc, v], axis=1)
    nb = T // ATT_BLOCK
    qb = jnp.moveaxis(_group_q(q).reshape(B, nb, ATT_BLOCK, ATT_KV_HEADS, ATT_GROUP, ATT_HEAD_DIM), 1, 0)
    o = lax.map(lambda blk: _sdpa_gqa(blk, k_all, v_all), qb)
    o = jnp.moveaxis(o, 0, 1).reshape(B, T, ATT_Q)
    return o, oc


def _gla_scan(q, k, v, log_a, s0):
    B, T, H, DK = q.shape
    DV = v.shape[-1]
    C = GLA_CHUNK
    n = T // C
    f32 = jnp.float32
    q = q.astype(f32).reshape(B, n, C, H, DK)
    k = k.astype(f32).reshape(B, n, C, H, DK)
    vv = v.astype(f32).reshape(B, n, C, H, DV)
    b = jnp.cumsum(log_a.astype(f32).reshape(B, n, C, H, DK), axis=2)
    b_last = b[:, :, -1:]
    q_dec = q * jnp.exp(b)
    att = jnp.einsum('bnihd,bnjhd->bnhij', q_dec, k * jnp.exp(-b))
    att = jnp.where(jnp.tril(jnp.ones((C, C), bool)), att, 0.0)
    o = jnp.einsum('bnhij,bnjhe->bnihe', att, vv)
    ds = jnp.einsum('bnjhd,bnjhe->bnhde', k * jnp.exp(b_last - b), vv)
    g = jnp.exp(b_last[:, :, 0])

    def step(s, inp):
        g_n, ds_n = inp
        return g_n[..., None] * s + ds_n, s

    s_fin, s_prev = lax.scan(step, s0, (jnp.moveaxis(g, 1, 0), jnp.moveaxis(ds, 1, 0)))
    o = o + jnp.einsum('bnihd,bnhde->bnihe', q_dec, jnp.moveaxis(s_prev, 0, 1))
    return o.reshape(B, T, H, DV).astype(v.dtype), s_fin


def _gla(q, k, v, gate, lr, qc, kc, vc, gatec, lrc, w_dec, b_dec, norm_g):
    def prep(q, k, v, lr):
        q = _heads(q, GLA_HEADS, GLA_DK) * (GLA_DK ** -0.5)
        k = _heads(k, GLA_HEADS, GLA_DK)
        v = _heads(v, GLA_HEADS, GLA_DV)
        las = [_heads(jax.nn.log_sigmoid((lr[..., d * GLA_LOWRANK:(d + 1) * GLA_LOWRANK] @ w_dec[d] + b_dec[d]).astype(jnp.float32)) / GLA_TAU,
                      GLA_HEADS, GLA_DK) for d in range(2)]
        return q, k, v, las[0], las[1]

    q, k, v, la_f, la_b = prep(q, k, v, lr)
    qc, kc, vc, lac_f, lac_b = prep(qc, kc, vc, lrc)
    s0 = jnp.zeros((q.shape[0], GLA_HEADS, GLA_DK, GLA_DV), jnp.float32)
    oc_f, sc_f = _gla_scan(qc, kc, vc, lac_f, s0)
    oc_b, sc_b = _gla_scan(_flip(qc), _flip(kc), _flip(vc), _flip(lac_b), s0)
    o_f, _ = _gla_scan(q, k, v, la_f, sc_f)
    o_b, _ = _gla_scan(_flip(q), _flip(k), _flip(v), _flip(la_b), sc_b)

    def finish(o, g):
        o = _rms(o, norm_g) * jax.nn.silu(_heads(g, GLA_HEADS, GLA_DV))
        return o.reshape(o.shape[:2] + (GLA_V,))

    return finish(o_f + _flip(o_b), gate), finish(oc_f + _flip(oc_b), gatec)


def _ssd_scan(x, dt, a, bm, cm, s0):
    B, T, G, HG, P = x.shape
    N = bm.shape[-1]
    C = SSD_CHUNK
    n = T // C
    f32 = jnp.float32
    xdt = (x.astype(f32) * dt[..., None]).reshape(B, n, C, G, HG, P)
    bm = bm.astype(f32).reshape(B, n, C, G, N)
    cm = cm.astype(f32).reshape(B, n, C, G, N)
    cs = jnp.cumsum((dt * a).reshape(B, n, C, G, HG), axis=2)
    mask = jnp.tril(jnp.ones((C, C), bool))[:, :, None, None]
    seg = jnp.where(mask, cs[:, :, :, None] - cs[:, :, None, :], -jnp.inf)
    lmat = jnp.exp(seg) * jnp.einsum('bnigs,bnjgs->bnijg', cm, bm)[..., None]
    y = jnp.einsum('bnijgh,bnjghp->bnighp', lmat, xdt)
    ds = jnp.einsum('bnjgs,bnjghp->bnghps', bm, xdt * jnp.exp(cs[:, :, -1:] - cs)[..., None])
    g = jnp.exp(cs[:, :, -1])

    def step(s, inp):
        g_n, ds_n = inp
        return g_n[..., None, None] * s + ds_n, s

    s_fin, s_prev = lax.scan(step, s0, (jnp.moveaxis(g, 1, 0), jnp.moveaxis(ds, 1, 0)))
    y = y + jnp.einsum('bnigs,bnghps->bnighp', cm, jnp.moveaxis(s_prev, 0, 1)) * jnp.exp(cs)[..., None]
    return y.reshape(B, T, G, HG, P).astype(x.dtype), s_fin


def _ssd(z, xbc, dtr, zc, xbcc, dtrc, conv_w, conv_b, dt_bias, a_log, d_skip, norm_g):
    def conv(t):
        pad = SSD_CONV // 2
        y = lax.conv_general_dilated(t, conv_w[:, None, :].astype(t.dtype), window_strides=(1,),
                                     padding=[(pad, pad)], dimension_numbers=('NWC', 'WIO', 'NWC'),
                                     feature_group_count=SSD_XBC)
        return jax.nn.silu(y + conv_b)

    def prep(xbc, dtr):
        xbc = conv(xbc)
        xs, bm, cm = jnp.split(xbc, [SSD_INNER, SSD_INNER + SSD_BC], axis=-1)
        lead = xs.shape[:2]
        xs = xs.reshape(lead + (SSD_GROUPS, SSD_HG, SSD_HEAD_DIM))
        bm = bm.reshape(lead + (SSD_GROUPS, SSD_STATE))
        cm = cm.reshape(lead + (SSD_GROUPS, SSD_STATE))
        dts = [jax.nn.softplus(dtr[..., d * SSD_HEADS:(d + 1) * SSD_HEADS].astype(jnp.float32) + dt_bias[d].astype(jnp.float32)).reshape(lead + (SSD_GROUPS, SSD_HG))
               for d in range(2)]
        return xs, bm, cm, dts[0], dts[1]

    a_f = -jnp.exp(a_log[0].astype(jnp.float32)).reshape(SSD_GROUPS, SSD_HG)
    a_b = -jnp.exp(a_log[1].astype(jnp.float32)).reshape(SSD_GROUPS, SSD_HG)
    x, bm, cm, dt_f, dt_b = prep(xbc, dtr)
    xc, bmc, cmc, dtc_f, dtc_b = prep(xbcc, dtrc)
    s0 = jnp.zeros((x.shape[0], SSD_GROUPS, SSD_HG, SSD_HEAD_DIM, SSD_STATE), jnp.float32)
    yc_f, sc_f = _ssd_scan(xc, dtc_f, a_f, bmc, cmc, s0)
    yc_b, sc_b = _ssd_scan(_flip(xc), _flip(dtc_b), a_b, _flip(bmc), _flip(cmc), s0)
    y_f, _ = _ssd_scan(x, dt_f, a_f, bm, cm, sc_f)
    y_b, _ = _ssd_scan(_flip(x), _flip(dt_b), a_b, _flip(bm), _flip(cm), sc_b)
    dsk = d_skip.reshape(SSD_GROUPS, SSD_HG, 1)

    def finish(y, xs, zz):
        y = (y + xs * dsk).reshape(y.shape[:2] + (SSD_INNER,))
        yg = (y * jax.nn.silu(zz)).reshape(y.shape[:2] + (SSD_GROUPS, SSD_INNER // SSD_GROUPS))
        return _rms(yg, norm_g.reshape(SSD_GROUPS, SSD_INNER // SSD_GROUPS)).reshape(y.shape)

    return finish(y_f + _flip(y_b), x, z), finish(yc_f + _flip(yc_b), xc, zc)


def _merge(o_att, o_gla, o_ssd, gates, w_branch, w_out):
    br = jnp.einsum('btnw,nwd->btnd', jnp.stack([o_att, o_gla, o_ssd], axis=2), w_branch)
    gt = jax.nn.sigmoid(gates.reshape(gates.shape[:2] + (N_BRANCH, D_MODEL)))
    return jnp.sum(gt * br, axis=2) @ w_out


def _mlp(h, w1, w2):
    return jnp.square(jax.nn.relu(h @ w1)) @ w2


def setup_inputs(seed: int = 0) -> dict:
    key = jax.random.key(seed)
    ks = iter(jax.random.split(key, 32))
    f32 = jnp.float32

    def nrm(shape, fan):
        return jax.random.normal(next(ks), shape, f32) * (fan ** -0.5)

    def gain(shape):
        return 1.0 + 0.05 * jax.random.normal(next(ks), shape, f32)

    def small(shape, s):
        return s * jax.random.normal(next(ks), shape, f32)

    dt = jnp.exp(jax.random.uniform(next(ks), (DEPTH, 2, SSD_HEADS), f32, math.log(1e-3), math.log(1e-1)))
    return {
        'x': jax.random.normal(next(ks), (BATCH, SEQ, D_MODEL), f32),
        'c': jax.random.normal(next(ks), (BATCH, D_MODEL), f32),
        'ctx': jax.random.normal(next(ks), (BATCH, CTX_LEN, D_MODEL), f32),
        'c_ctx': jax.random.normal(next(ks), (D_MODEL,), f32),
        'norm1': gain((DEPTH, D_MODEL)),
        'norm2': gain((DEPTH, D_MODEL)),
        'w_ada': nrm((DEPTH, D_MODEL, N_MOD * D_MODEL), D_MODEL),
        'b_ada': small((DEPTH, N_MOD * D_MODEL), 0.02),
        'w_in': nrm((DEPTH, D_MODEL, IN_COLS), D_MODEL),
        'att_q_norm': gain((DEPTH, ATT_HEAD_DIM)),
        'att_k_norm': gain((DEPTH, ATT_HEAD_DIM)),
        'gla_w_decay': nrm((DEPTH, 2, GLA_LOWRANK, GLA_QK), GLA_LOWRANK),
        'gla_b_decay': small((DEPTH, 2, GLA_QK), 0.1),
        'gla_norm': gain((DEPTH, GLA_DV)),
        'ssd_conv_w': nrm((DEPTH, SSD_CONV, SSD_XBC), SSD_CONV),
        'ssd_conv_b': small((DEPTH, SSD_XBC), 0.02),
        'ssd_dt_bias': dt + jnp.log(-jnp.expm1(-dt)),
        'ssd_a_log': jnp.log(jax.random.uniform(next(ks), (DEPTH, 2, SSD_HEADS), f32, 1.0, 16.0)),
        'ssd_d': gain((DEPTH, SSD_HEADS)),
        'ssd_norm': gain((DEPTH, SSD_INNER)),
        'w_branch': nrm((DEPTH, N_BRANCH, MIX_W, D_MODEL), MIX_W),
        'w_out': nrm((DEPTH, D_MODEL, D_MODEL), D_MODEL),
        'w_ff1': nrm((DEPTH, D_MODEL, D_FF), D_MODEL),
        'w_ff2': nrm((DEPTH, D_FF, D_MODEL), D_FF),
        'final_norm': gain((D_MODEL,)),
    }


def reference(x, c, ctx, c_ctx, norm1, norm2, w_ada, b_ada, w_in, att_q_norm, att_k_norm,
              gla_w_decay, gla_b_decay, gla_norm, ssd_conv_w, ssd_conv_b, ssd_dt_bias, ssd_a_log,
              ssd_d, ssd_norm, w_branch, w_out, w_ff1, w_ff2, final_norm):
    cos, sin = _axial_rope_tables(x.shape[1])
    sc = jax.nn.silu(c)[:, None, :]
    scc = jax.nn.silu(c_ctx)[None, None, :]
    xc = ctx
    for l in range(DEPTH):
        mod = jnp.split(sc @ w_ada[l] + b_ada[l], N_MOD, axis=-1)
        modc = jnp.split(scc @ w_ada[l] + b_ada[l], N_MOD, axis=-1)
        u = jnp.split(_modulate(x, norm1[l], mod[0], mod[1]) @ w_in[l], IN_OFFSETS, axis=-1)
        uc = jnp.split(_modulate(xc, norm1[l], modc[0], modc[1]) @ w_in[l], IN_OFFSETS, axis=-1)
        o_att, oc_att = _attention(u[0], u[1], u[2], uc[0], uc[1], uc[2],
                                   att_q_norm[l], att_k_norm[l], cos, sin)
        o_gla, oc_gla = _gla(u[3], u[4], u[5], u[6], u[7], uc[3], uc[4], uc[5], uc[6], uc[7],
                             gla_w_decay[l], gla_b_decay[l], gla_norm[l])
        o_ssd, oc_ssd = _ssd(u[8], u[9], u[10], uc[8], uc[9], uc[10], ssd_conv_w[l], ssd_conv_b[l],
                             ssd_dt_bias[l], ssd_a_log[l], ssd_d[l], ssd_norm[l])
        x = x + mod[2] * _merge(o_att, o_gla, o_ssd, u[11], w_branch[l], w_out[l])
        x = x + mod[5] * _mlp(_modulate(x, norm2[l], mod[3], mod[4]), w_ff1[l], w_ff2[l])
        if l < DEPTH - 1:
            xc = xc + modc[2] * _merge(oc_att, oc_gla, oc_ssd, uc[11], w_branch[l], w_out[l])
            xc = xc + modc[5] * _mlp(_modulate(xc, norm2[l], modc[3], modc[4]), w_ff1[l], w_ff2[l])
    return _rms(x, final_norm)
```

```python
import functools
import math

import jax
import jax.numpy as jnp
import numpy as np
from jax import lax
from jax.experimental import pallas as pl
from jax.experimental.pallas import tpu as pltpu

F32 = jnp.float32
BF16 = jnp.bfloat16

EPS = 1e-6
GRID_W = 64
N_BRANCH = 3
N_MOD = 6
ATT_HEAD_DIM = 128
ATT_KV_HEADS = 2
ROPE_THETA = 10000.0
GLA_HEADS = 4
GLA_LOWRANK = 16
GLA_TAU = 16.0
GLA_CHUNK = 64
SSD_HEAD_DIM = 64
SSD_GROUPS = 2
SSD_STATE = 128
SSD_CONV = 5
SSD_CHUNK = 64

SEQ_BLOCK = 256
MOD_ROWS = 8
SMALL_COLS = 128
HALO_ROWS = 16
VMEM_LIMIT = 56 * 1024 * 1024


def _cparams(sem):
    return pltpu.CompilerParams(dimension_semantics=sem, vmem_limit_bytes=VMEM_LIMIT)


def _dot(a, b):
    return jnp.dot(a, b, preferred_element_type=F32)


def _dot_nt(a, b):
    return lax.dot_general(a, b, (((1,), (1,)), ((), ())), preferred_element_type=F32)


def _dot_tn(a, b):
    return lax.dot_general(a, b, (((0,), (0,)), ((), ())), preferred_element_type=F32)


def _split3(x):
    hi = x.astype(BF16)
    r = x - hi.astype(F32)
    mid = r.astype(BF16)
    lo = (r - mid.astype(F32)).astype(BF16)
    return hi, mid, lo


def _dot_exact_lhs(a, x):
    hi, mid, lo = _split3(x)
    return _dot(a, lo) + _dot(a, mid) + _dot(a, hi)


def _dot_exact_rhs(x, e):
    hi, mid, lo = _split3(x)
    return _dot(lo, e) + _dot(mid, e) + _dot(hi, e)


def _silu(x):
    return x * jax.nn.sigmoid(x)


def _softplus(x):
    return jnp.maximum(x, 0.0) + jnp.log1p(jnp.exp(-jnp.abs(x)))


def _block_tri(n, chunk, rev):
    r = lax.broadcasted_iota(jnp.int32, (n, n), 0)
    c = lax.broadcasted_iota(jnp.int32, (n, n), 1)
    shift = chunk.bit_length() - 1
    assert chunk == 1 << shift
    same = jnp.right_shift(r, shift) == jnp.right_shift(c, shift)
    tri = (c >= r) if rev else (c <= r)
    return jnp.where(same, jnp.where(tri, 1.0, 0.0), 0.0).astype(BF16)


class _Layout:
    def __init__(self, B, CTX, SEQ, D):
        assert CTX % SEQ_BLOCK == 0 and SEQ % SEQ_BLOCK == 0
        self.B, self.CTX, self.SEQ, self.D = B, CTX, SEQ, D
        self.R = B * (CTX + SEQ)
        self.NC = CTX // SEQ_BLOCK
        self.NL = SEQ // SEQ_BLOCK
        self.NS = self.NC + self.NL
        self.tm = next(t for t in (512, 256) if (B * CTX) % t == 0 and SEQ % t == 0)
        assert B + 1 <= MOD_ROWS

    def row_block(self, b, s):
        return jnp.where(s < self.NC, b * self.NC + s, self.B * self.NC + b * self.NL + (s - self.NC))

    def seq_order(self, n, rev):
        if not rev:
            return n
        return jnp.where(n < self.NC, self.NC - 1 - n, self.NC + self.NL - 1 - (n - self.NC))

    def mod_row(self, i):
        nctx = (self.B * self.CTX) // self.tm
        return jnp.where(i < nctx, self.B, (i - nctx) // (self.SEQ // self.tm))


def _ada_kernel(c_ref, w_ref, b_ref, o_ref):
    s = _silu(c_ref[...]).astype(BF16)
    o_ref[0] = _dot(s, w_ref[0].astype(BF16)) + b_ref[0]


def _ada(cvec, w_ada, b_ada):
    depth, D, n = w_ada.shape
    tn = 1024
    return pl.pallas_call(
        _ada_kernel,
        out_shape=jax.ShapeDtypeStruct((depth, MOD_ROWS, n), F32),
        grid=(depth, n // tn),
        in_specs=[
            pl.BlockSpec((MOD_ROWS, D), lambda l, j: (0, 0)),
            pl.BlockSpec((1, D, tn), lambda l, j: (l, 0, j)),
            pl.BlockSpec((1, 1, tn), lambda l, j: (l, 0, j)),
        ],
        out_specs=pl.BlockSpec((1, MOD_ROWS, tn), lambda l, j: (l, 0, j)),
        compiler_params=_cparams(("arbitrary", "arbitrary")),
        name="ada",
    )(cvec, w_ada, b_ada.reshape(depth, 1, n))


def _norm_mod(x_ref, g_ref, mod_ref, k, h_ref):
    D = x_ref.shape[-1]
    shift = mod_ref[:, k * D:(k + 1) * D]
    a = g_ref[...] * (1.0 + mod_ref[:, (k + 1) * D:(k + 2) * D])
    rc = 64

    def body(r, carry):
        sl = pl.ds(pl.multiple_of(r * rc, rc), rc)
        xf = x_ref[sl, :]
        ms = jnp.mean(xf * xf, axis=-1, keepdims=True)
        h_ref[sl, :] = (xf * lax.rsqrt(ms + EPS) * a + shift).astype(h_ref.dtype)
        return carry

    lax.fori_loop(0, x_ref.shape[0] // rc, body, 0)


def _inproj_kernel(x_ref, mod_ref, g_ref, w_ref, ws_ref, u_ref, small_ref, h_ref):
    @pl.when(pl.program_id(1) == 0)
    def _():
        _norm_mod(x_ref, g_ref, mod_ref, 0, h_ref)
        small_ref[...] = _dot(h_ref[...], ws_ref[...])

    u_ref[...] = _dot(h_ref[...], w_ref[...]).astype(u_ref.dtype)


def _inproj(lay, X, mod, g, w_main, w_small):
    D, n = w_main.shape
    tm, tn = lay.tm, 1024
    return pl.pallas_call(
        _inproj_kernel,
        out_shape=(jax.ShapeDtypeStruct((lay.R, n), BF16),
                   jax.ShapeDtypeStruct((lay.R, SMALL_COLS), F32)),
        grid=(lay.R // tm, n // tn),
        in_specs=[
            pl.BlockSpec((tm, D), lambda i, j: (i, 0)),
            pl.BlockSpec((None, 1, N_MOD * D), lambda i, j: (lay.mod_row(i), 0, 0)),
            pl.BlockSpec((1, D), lambda i, j: (0, 0)),
            pl.BlockSpec((D, tn), lambda i, j: (0, j)),
            pl.BlockSpec((D, SMALL_COLS), lambda i, j: (0, 0)),
        ],
        out_specs=(pl.BlockSpec((tm, tn), lambda i, j: (i, j)),
                   pl.BlockSpec((tm, SMALL_COLS), lambda i, j: (i, 0))),
        scratch_shapes=[pltpu.VMEM((tm, D), BF16)],
        compiler_params=_cparams(("arbitrary", "arbitrary")),
        name="in_proj",
    )(X, mod, g, w_main, w_small)


def _merge_kernel(x_ref, mod_ref, g_ref, oa_ref, og_ref, os_ref,
                  wg0_ref, wg1_ref, wg2_ref, wb0_ref, wb1_ref, wb2_ref, m_ref, h_ref):
    @pl.when(pl.program_id(1) == 0)
    def _():
        _norm_mod(x_ref, g_ref, mod_ref, 0, h_ref)

    h = h_ref[...]
    acc = None
    for o_ref, wg_ref, wb_ref in ((oa_ref, wg0_ref, wb0_ref), (og_ref, wg1_ref, wb1_ref),
                                  (os_ref, wg2_ref, wb2_ref)):
        t = jax.nn.sigmoid(_dot(h, wg_ref[...])) * _dot(o_ref[...], wb_ref[...])
        acc = t if acc is None else acc + t
    m_ref[...] = acc.astype(m_ref.dtype)


def _merge(lay, X, mod, g, o_att, o_gla, o_ssd, w_gate, w_branch, row0):
    D = lay.D
    W = o_att.shape[1]
    tm, tn = lay.tm, 512
    nj = D // tn
    rows = lambda i, j: (i + row0, 0)
    wg = lambda b: pl.BlockSpec((D, tn), lambda i, j: (0, b * nj + j))
    wb = lambda b: pl.BlockSpec((None, W, tn), lambda i, j: (b, 0, j))
    return pl.pallas_call(
        _merge_kernel,
        out_shape=jax.ShapeDtypeStruct((lay.R, D), BF16),
        grid=(lay.R // tm - row0, nj),
        in_specs=[
            pl.BlockSpec((tm, D), rows),
            pl.BlockSpec((None, 1, N_MOD * D), lambda i, j: (lay.mod_row(i + row0), 0, 0)),
            pl.BlockSpec((1, D), lambda i, j: (0, 0)),
            pl.BlockSpec((tm, W), rows), pl.BlockSpec((tm, W), rows), pl.BlockSpec((tm, W), rows),
            wg(0), wg(1), wg(2), wb(0), wb(1), wb(2),
        ],
        out_specs=pl.BlockSpec((tm, tn), lambda i, j: (i + row0, j)),
        scratch_shapes=[pltpu.VMEM((tm, D), BF16)],
        compiler_params=_cparams(("arbitrary", "arbitrary")),
        name="merge",
    )(X, mod, g, o_att, o_gla, o_ssd, w_gate, w_gate, w_gate, w_branch, w_branch, w_branch)


def _outproj_kernel(m_ref, w_ref, x_ref, gate_ref, o_ref):
    o_ref[...] = x_ref[...] + gate_ref[...] * _dot(m_ref[...], w_ref[...])


def _outproj(lay, m, w_out, X, mod, row0):
    D = lay.D
    tm, tn = lay.tm, 1024
    nj = D // tn
    return pl.pallas_call(
        _outproj_kernel,
        out_shape=jax.ShapeDtypeStruct((lay.R, D), F32),
        grid=(lay.R // tm - row0, nj),
        in_specs=[
            pl.BlockSpec((tm, D), lambda i, j: (i + row0, 0)),
            pl.BlockSpec((D, tn), lambda i, j: (0, j)),
            pl.BlockSpec((tm, tn), lambda i, j: (i + row0, j)),
            pl.BlockSpec((None, 1, tn), lambda i, j: (lay.mod_row(i + row0), 0, 2 * nj + j)),
        ],
        out_specs=pl.BlockSpec((tm, tn), lambda i, j: (i + row0, j)),
        input_output_aliases={2: 0},
        compiler_params=_cparams(("arbitrary", "arbitrary")),
        name="out_proj",
    )(m, w_out, X, mod)


def _mlp_kernel(x_ref, mod_ref, g_ref, w1_ref, w2_ref, *rest, final):
    if final:
        fg_ref, o_ref, h_ref = rest
    else:
        o_ref, h_ref = rest
    k = pl.program_id(1)
    D = x_ref.shape[-1]

    @pl.when(k == 0)
    def _():
        _norm_mod(x_ref, g_ref, mod_ref, 3, h_ref)

    f = jnp.square(jnp.maximum(_dot(h_ref[...], w1_ref[...]), 0.0)).astype(BF16)
    part = _dot(f, w2_ref[...])

    @pl.when(k == 0)
    def _():
        o_ref[...] = part

    @pl.when(k > 0)
    def _():
        o_ref[...] += part

    @pl.when(k == pl.num_programs(1) - 1)
    def _():
        gate = mod_ref[:, 5 * D:6 * D]
        rc = 64

        def body(r, carry):
            sl = pl.ds(pl.multiple_of(r * rc, rc), rc)
            y = x_ref[sl, :] + gate * o_ref[sl, :]
            if final:
                ms = jnp.mean(y * y, axis=-1, keepdims=True)
                y = y * lax.rsqrt(ms + EPS) * fg_ref[...]
            o_ref[sl, :] = y
            return carry

        lax.fori_loop(0, x_ref.shape[0] // rc, body, 0)


def _mlp(lay, X, mod, g, w1, w2, row0, final_gain=None):
    D, FF = w1.shape
    tm, tf = lay.tm, 512
    final = final_gain is not None
    out_rows = lay.R - row0 * tm if final else lay.R
    out_off = 0 if final else row0
    in_specs = [
        pl.BlockSpec((tm, D), lambda i, k: (i + row0, 0)),
        pl.BlockSpec((None, 1, N_MOD * D), lambda i, k: (lay.mod_row(i + row0), 0, 0)),
        pl.BlockSpec((1, D), lambda i, k: (0, 0)),
        pl.BlockSpec((D, tf), lambda i, k: (0, k)),
        pl.BlockSpec((tf, D), lambda i, k: (k, 0)),
    ]
    args = [X, mod, g, w1, w2]
    if final:
        in_specs.append(pl.BlockSpec((1, D), lambda i, k: (0, 0)))
        args.append(final_gain)
    return pl.pallas_call(
        functools.partial(_mlp_kernel, final=final),
        out_shape=jax.ShapeDtypeStruct((out_rows, D), F32),
        grid=(lay.R // tm - row0, FF // tf),
        in_specs=in_specs,
        out_specs=pl.BlockSpec((tm, D), lambda i, k: (i + out_off, 0)),
        scratch_shapes=[pltpu.VMEM((tm, D), BF16)],
        compiler_params=_cparams(("arbitrary", "arbitrary")),
        name="mlp",
    )(*args)


def _qkprep_kernel(q_ref, k_ref, v_ref, qg_ref, kg_ref, cos_ref, sa_ref, sb_ref,
                   qo_ref, kt_ref, vo_ref):
    cos, sa, sb = cos_ref[...], sa_ref[...], sb_ref[...]
    hd = ATT_HEAD_DIM

    def prep(x, g):
        xf = x.astype(F32)
        y = xf * lax.rsqrt(jnp.mean(xf * xf, axis=-1, keepdims=True) + EPS) * g
        return y * cos + pltpu.roll(y, hd - hd // 4, 1) * sa + pltpu.roll(y, hd // 4, 1) * sb

    for h in range(q_ref.shape[1] // hd):
        sl = slice(h * hd, (h + 1) * hd)
        qo_ref[:, sl] = (prep(q_ref[:, sl], qg_ref[...]) * (hd ** -0.5)).astype(qo_ref.dtype)
    for g in range(k_ref.shape[1] // hd):
        sl = slice(g * hd, (g + 1) * hd)
        kt_ref[g] = prep(k_ref[:, sl], kg_ref[...]).T.astype(kt_ref.dtype)
    vo_ref[...] = v_ref[...]


def _qkprep(lay, U, cols, q_gain, k_gain, rope):
    B, NS, blk = lay.B, lay.NS, SEQ_BLOCK
    att_q = cols["att_q"][1]
    att_kv = cols["att_k"][1]
    kvh = att_kv // ATT_HEAD_DIM
    row = lambda b, n: lay.row_block(b, n)
    tab = pl.BlockSpec((blk, ATT_HEAD_DIM), lambda b, n: (n, 0))
    return pl.pallas_call(
        _qkprep_kernel,
        out_shape=(jax.ShapeDtypeStruct((lay.R, att_q), BF16),
                   jax.ShapeDtypeStruct((B, kvh, NS, ATT_HEAD_DIM, blk), BF16),
                   jax.ShapeDtypeStruct((B, NS * blk, att_kv), BF16)),
        grid=(B, NS),
        in_specs=[
            pl.BlockSpec((blk, att_q), lambda b, n: (row(b, n), cols["att_q"][0] // att_q)),
            pl.BlockSpec((blk, att_kv), lambda b, n: (row(b, n), cols["att_k"][0] // att_kv)),
            pl.BlockSpec((blk, att_kv), lambda b, n: (row(b, n), cols["att_v"][0] // att_kv)),
            pl.BlockSpec((1, ATT_HEAD_DIM), lambda b, n: (0, 0)),
            pl.BlockSpec((1, ATT_HEAD_DIM), lambda b, n: (0, 0)),
            tab, tab, tab,
        ],
        out_specs=(pl.BlockSpec((blk, att_q), lambda b, n: (row(b, n), 0)),
                   pl.BlockSpec((None, kvh, None, ATT_HEAD_DIM, blk), lambda b, n: (b, 0, n, 0, 0)),
                   pl.BlockSpec((None, blk, att_kv), lambda b, n: (b, n, 0))),
        compiler_params=_cparams(("arbitrary", "arbitrary")),
        name="qk_prep",
    )(U, U, U, q_gain, k_gain, *rope)


def _attn_kernel(q_ref, kt_ref, v_ref, o_ref, *, n_ctx_blocks):
    n = pl.program_id(2)
    nblk, hd, tk = kt_ref.shape
    n_kv = jnp.where(n < n_ctx_blocks, n_ctx_blocks, nblk)
    tq = q_ref.shape[0]
    for h in range(q_ref.shape[1] // hd):
        q = q_ref[:, h * hd:(h + 1) * hd]

        def body(c, carry):
            m, l, acc = carry
            s = _dot(q, kt_ref[c])
            m_new = jnp.maximum(m, jnp.max(s, axis=-1, keepdims=True))
            alpha = jnp.exp(m - m_new)
            p = jnp.exp(s - m_new)
            v = v_ref[pl.ds(pl.multiple_of(c * tk, tk), tk), :]
            l = alpha * l + jnp.sum(p, axis=-1, keepdims=True)
            acc = alpha * acc + _dot(p.astype(BF16), v)
            return m_new, l, acc

        init = (jnp.full((tq, 1), -1e30, F32), jnp.zeros((tq, 1), F32), jnp.zeros((tq, hd), F32))
        _, l, acc = lax.fori_loop(0, n_kv, body, init)
        o_ref[:, h * hd:(h + 1) * hd] = (acc / l).astype(o_ref.dtype)


def _attention(lay, qp, kt, vv):
    B, kvh, NS, hd, blk = kt.shape
    att_q = qp.shape[1]
    gw = att_q // kvh
    return pl.pallas_call(
        functools.partial(_attn_kernel, n_ctx_blocks=lay.NC),
        out_shape=jax.ShapeDtypeStruct((lay.R, att_q), BF16),
        grid=(B, kvh, NS),
        in_specs=[
            pl.BlockSpec((blk, gw), lambda b, g, n: (lay.row_block(b, n), g)),
            pl.BlockSpec((None, None, NS, hd, blk), lambda b, g, n: (b, g, 0, 0, 0)),
            pl.BlockSpec((None, NS * blk, hd), lambda b, g, n: (b, 0, g)),
        ],
        out_specs=pl.BlockSpec((blk, gw), lambda b, g, n: (lay.row_block(b, n), g)),
        compiler_params=_cparams(("arbitrary", "arbitrary", "arbitrary")),
        name="attention",
    )(qp, kt, vv)


def _gla_kernel(*refs, rev, final, dk, dv):
    if final:
        q_ref, k_ref, v_ref, sm_ref, wd_ref, bd_ref, gate_ref, ng_ref, prev_ref, o_ref, st_ref = refs
    else:
        q_ref, k_ref, v_ref, sm_ref, wd_ref, bd_ref, o_ref, st_ref = refs
    blk = q_ref.shape[0]
    heads = q_ref.shape[1] // dk
    C = GLA_CHUNK

    @pl.when(pl.program_id(1) == 0)
    def _():
        st_ref[...] = jnp.zeros_like(st_ref)

    z = _dot(sm_ref[...].astype(BF16), wd_ref[...]) + bd_ref[...]
    la = (jnp.minimum(z, 0.0) - jnp.log1p(jnp.exp(-jnp.abs(z)))) * (1.0 / GLA_TAU)
    bcum = _dot_exact_lhs(_block_tri(blk, C, rev), la)

    ri = lax.broadcasted_iota(jnp.int32, (C, C), 0)
    ci = lax.broadcasted_iota(jnp.int32, (C, C), 1)
    causal = (ci >= ri) if rev else (ci <= ri)

    n_chunks = blk // C
    for step in range(n_chunks):
        c = n_chunks - 1 - step if rev else step
        rs = slice(c * C, (c + 1) * C)
        last = c * C if rev else (c + 1) * C - 1
        for h in range(heads):
            ks = slice(h * dk, (h + 1) * dk)
            vs = slice(h * dv, (h + 1) * dv)
            b = bcum[rs, ks]
            b_last = bcum[last:last + 1, ks]
            q = q_ref[rs, ks].astype(F32) * (dk ** -0.5)
            k = k_ref[rs, ks].astype(F32)
            v = v_ref[rs, vs]
            q_dec = (q * jnp.exp(b)).astype(BF16)
            k_dec = (k * jnp.exp(-b)).astype(BF16)
            k_st = (k * jnp.exp(b_last - b)).astype(BF16)
            att = jnp.where(causal, _dot_nt(q_dec, k_dec), 0.0).astype(BF16)
            st = st_ref[h]
            o = _dot(att, v) + _dot_nt(q_dec, st.astype(BF16))
            st_ref[h] = st * jnp.exp(b_last) + _dot_tn(v, k_st)
            if final:
                tot = o + prev_ref[rs, vs]
                y = tot * lax.rsqrt(jnp.mean(tot * tot, axis=-1, keepdims=True) + EPS) * ng_ref[...]
                o_ref[rs, vs] = (y * _silu(gate_ref[rs, vs].astype(F32))).astype(o_ref.dtype)
            else:
                o_ref[rs, vs] = o


def _gla_dir(lay, U, small, cols, w_dec, b_dec, rev, norm_g=None, prev=None):
    B, NS, blk = lay.B, lay.NS, SEQ_BLOCK
    qk_w = cols["gla_q"][1]
    v_w = cols["gla_v"][1]
    dk, dv = qk_w // GLA_HEADS, v_w // GLA_HEADS
    final = prev is not None
    rowi = lambda b, n: lay.row_block(b, lay.seq_order(n, rev))
    ucol = lambda name: pl.BlockSpec((blk, cols[name][1]), lambda b, n: (rowi(b, n), cols[name][0] // cols[name][1]))
    const = lambda shape: pl.BlockSpec(shape, lambda b, n: (0,) * len(shape))
    in_specs = [ucol("gla_q"), ucol("gla_k"), ucol("gla_v"),
                pl.BlockSpec((blk, SMALL_COLS), lambda b, n: (rowi(b, n), 0)),
                const((SMALL_COLS, qk_w)), const((1, qk_w))]
    args = [U, U, U, small, w_dec, b_dec]
    if final:
        in_specs += [ucol("gla_gate"), const((1, dv)), pl.BlockSpec((blk, v_w), lambda b, n: (rowi(b, n), 0))]
        args += [U, norm_g, prev]
    return pl.pallas_call(
        functools.partial(_gla_kernel, rev=rev, final=final, dk=dk, dv=dv),
        out_shape=jax.ShapeDtypeStruct((lay.R, v_w), BF16 if final else F32),
        grid=(B, NS),
        in_specs=in_specs,
        out_specs=pl.BlockSpec((blk, v_w), lambda b, n: (rowi(b, n), 0)),
        scratch_shapes=[pltpu.VMEM((GLA_HEADS, dv, dk), F32)],
        compiler_params=_cparams(("arbitrary", "arbitrary")),
        name="gla_bwd" if rev else "gla_fwd",
    )(*args)


def _conv_kernel(x_ref, xp_ref, xn_ref, bc_ref, bcp_ref, bcn_ref, wx_ref, bx_ref, wbc_ref, bbc_ref,
                 xo_ref, bco_ref, ex_ref, ebc_ref, *, n_ctx_total, NC, NL):
    i = pl.program_id(0)
    in_ctx = i < n_ctx_total
    p = jnp.where(in_ctx, i % NC, jnp.maximum(i - n_ctx_total, 0) % NL)
    seg = jnp.where(in_ctx, NC, NL)
    has_left = (p > 0).astype(F32)
    has_right = (p < seg - 1).astype(F32)
    blk = x_ref.shape[0]
    pad = SSD_CONV // 2

    def run(m_ref, p_ref, n_ref, w_ref, b_ref, e_ref, o_ref):
        e_ref[0:HALO_ROWS] = p_ref[...].astype(F32) * has_left
        e_ref[HALO_ROWS:HALO_ROWS + blk] = m_ref[...].astype(F32)
        e_ref[HALO_ROWS + blk:] = n_ref[...].astype(F32) * has_right
        acc = b_ref[...]
        for j in range(SSD_CONV):
            acc = acc + w_ref[j:j + 1, :] * e_ref[HALO_ROWS - pad + j:HALO_ROWS - pad + j + blk, :]
        o_ref[...] = _silu(acc).astype(o_ref.dtype)

    run(x_ref, xp_ref, xn_ref, wx_ref, bx_ref, ex_ref, xo_ref)
    run(bc_ref, bcp_ref, bcn_ref, wbc_ref, bbc_ref, ebc_ref, bco_ref)


def _conv(lay, U, cols, wx, bx, wbc, bbc):
    blk = SEQ_BLOCK
    nb = lay.R // blk
    hb = blk // HALO_ROWS
    nh = lay.R // HALO_ROWS
    xw = cols["ssd_x"][1]
    bcw = cols["ssd_bc"][1]
    xc, bcc = cols["ssd_x"][0] // xw, cols["ssd_bc"][0] // bcw
    prev = lambda i: jnp.maximum(i * hb - 1, 0)
    nxt = lambda i: jnp.minimum((i + 1) * hb, nh - 1)
    const = lambda shape: pl.BlockSpec(shape, lambda i: (0,) * len(shape))
    return pl.pallas_call(
        functools.partial(_conv_kernel, n_ctx_total=lay.B * lay.NC, NC=lay.NC, NL=lay.NL),
        out_shape=(jax.ShapeDtypeStruct((lay.R, xw), BF16), jax.ShapeDtypeStruct((lay.R, bcw), BF16)),
        grid=(nb,),
        in_specs=[
            pl.BlockSpec((blk, xw), lambda i: (i, xc)),
            pl.BlockSpec((HALO_ROWS, xw), lambda i: (prev(i), xc)),
            pl.BlockSpec((HALO_ROWS, xw), lambda i: (nxt(i), xc)),
            pl.BlockSpec((blk, bcw), lambda i: (i, bcc)),
            pl.BlockSpec((HALO_ROWS, bcw), lambda i: (prev(i), bcc)),
            pl.BlockSpec((HALO_ROWS, bcw), lambda i: (nxt(i), bcc)),
            const((SSD_CONV, xw)), const((1, xw)), const((SSD_CONV, bcw)), const((1, bcw)),
        ],
        out_specs=(pl.BlockSpec((blk, xw), lambda i: (i, 0)), pl.BlockSpec((blk, bcw), lambda i: (i, 0))),
        scratch_shapes=[pltpu.VMEM((blk + 2 * HALO_ROWS, xw), F32),
                        pltpu.VMEM((blk + 2 * HALO_ROWS, bcw), F32)],
        compiler_params=_cparams(("arbitrary",)),
        name="ssd_conv",
    )(U, U, U, U, U, U, wx, bx, wbc, bbc)


def _ssd_kernel(*refs, rev, final):
    if final:
        (x_ref, bc_ref, sm_ref, dtb_ref, apad_ref, e_ref, z_ref, dsk_ref, ng_ref, prev_ref,
         o_ref, st_ref, y_ref) = refs
    else:
        x_ref, bc_ref, sm_ref, dtb_ref, apad_ref, e_ref, o_ref, st_ref = refs
        y_ref = o_ref
    blk, inner = x_ref.shape
    N = SSD_STATE
    P = SSD_HEAD_DIM
    C = SSD_CHUNK
    gw = inner // SSD_GROUPS
    pair = 2 * P

    @pl.when(pl.program_id(1) == 0)
    def _():
        st_ref[...] = jnp.zeros_like(st_ref)

    dt16 = _softplus(sm_ref[...] + dtb_ref[...])
    cs16 = _dot_exact_lhs(_block_tri(blk, C, rev), dt16 * apad_ref[...])
    e = e_ref[...]
    dt_full = _dot_exact_rhs(dt16, e)
    cs_full = _dot_exact_rhs(cs16, e)

    ri = lax.broadcasted_iota(jnp.int32, (C, pair), 0)
    ci = lax.broadcasted_iota(jnp.int32, (C, pair), 1)
    cm = jnp.where(ci >= P, ci - P, ci)
    eye2 = cm == ri
    causal2 = (cm >= ri) if rev else (cm <= ri)
    left = ci < P
    zero = jnp.zeros((C, pair), BF16)

    n_chunks = blk // C
    for step in range(n_chunks):
        c = n_chunks - 1 - step if rev else step
        rs = slice(c * C, (c + 1) * C)
        last = c * C if rev else (c + 1) * C - 1
        for g in range(SSD_GROUPS):
            gs = slice(g * gw, (g + 1) * gw)
            bm = bc_ref[rs, g * N:(g + 1) * N]
            cmat = bc_ref[rs, SSD_GROUPS * N + g * N:SSD_GROUPS * N + (g + 1) * N]
            cb2 = _dot_nt(cmat, jnp.concatenate([bm, bm], axis=0))
            cs_g = cs_full[rs, gs]
            cs_last = cs_full[last:last + 1, gs]
            xdt = x_ref[rs, gs].astype(F32) * dt_full[rs, gs]
            st = st_ref[g]
            y_inter = _dot(cmat, st.astype(BF16)) * jnp.exp(cs_g)
            w = (xdt * jnp.exp(cs_last - cs_g)).astype(BF16)
            st_ref[g] = st * jnp.exp(cs_last) + _dot_tn(bm, w)
            xdt_b = xdt.astype(BF16)
            for pr in range(gw // pair):
                ps = slice(pr * pair, (pr + 1) * pair)
                col = cs_g[:, ps]
                rowv = jnp.sum(jnp.where(eye2, col, 0.0), axis=0, keepdims=True)
                lmat = jnp.exp(jnp.where(causal2, col - rowv, -jnp.inf))
                mm = (lmat * cb2).astype(BF16)
                xp = xdt_b[:, ps]
                rhs = jnp.concatenate([jnp.where(left, xp, zero), jnp.where(left, zero, xp)], axis=0)
                y = _dot(mm, rhs) + y_inter[:, ps]
                os_ = slice(g * gw + pr * pair, g * gw + (pr + 1) * pair)
                if final:
                    y_ref[rs, os_] = y + prev_ref[rs, os_] + x_ref[rs, os_].astype(F32) * dsk_ref[:, os_]
                else:
                    y_ref[rs, os_] = y

    if final:
        rc = 64
        for r in range(blk // rc):
            rs = slice(r * rc, (r + 1) * rc)
            for g in range(SSD_GROUPS):
                gs = slice(g * gw, (g + 1) * gw)
                yg = y_ref[rs, gs] * _silu(z_ref[rs, gs].astype(F32))
                ms = jnp.mean(yg * yg, axis=-1, keepdims=True)
                o_ref[rs, gs] = (yg * lax.rsqrt(ms + EPS) * ng_ref[:, gs]).astype(o_ref.dtype)


def _ssd_dir(lay, xc, bcc, small, U, cols, dtb, apad, expand, rev, dsk=None, norm_g=None, prev=None):
    B, NS, blk = lay.B, lay.NS, SEQ_BLOCK
    inner = xc.shape[1]
    bcw = bcc.shape[1]
    final = prev is not None
    rowi = lambda b, n: lay.row_block(b, lay.seq_order(n, rev))
    rows = lambda w: pl.BlockSpec((blk, w), lambda b, n: (rowi(b, n), 0))
    const = lambda shape: pl.BlockSpec(shape, lambda b, n: (0,) * len(shape))
    in_specs = [rows(inner), rows(bcw), rows(SMALL_COLS), const((1, SMALL_COLS)), const((1, SMALL_COLS)),
                const((SMALL_COLS, inner))]
    args = [xc, bcc, small, dtb, apad, expand]
    if final:
        zc = cols["ssd_z"]
        in_specs += [pl.BlockSpec((blk, inner), lambda b, n: (rowi(b, n), zc[0] // zc[1])),
                     const((1, inner)), const((1, inner)), rows(inner)]
        args += [U, dsk, norm_g, prev]
    return pl.pallas_call(
        functools.partial(_ssd_kernel, rev=rev, final=final),
        out_shape=jax.ShapeDtypeStruct((lay.R, inner), BF16 if final else F32),
        grid=(B, NS),
        in_specs=in_specs,
        out_specs=rows(inner),
        scratch_shapes=[pltpu.VMEM((SSD_GROUPS, SSD_STATE, inner // SSD_GROUPS), F32)]
        + ([pltpu.VMEM((blk, inner), F32)] if final else []),
        compiler_params=_cparams(("arbitrary", "arbitrary")),
        name="ssd_bwd" if rev else "ssd_fwd",
    )(*args)


def _rope_tables(CTX, SEQ):
    hd = ATT_HEAD_DIM
    pairs = hd // 4
    rows = SEQ // GRID_W
    row = jnp.repeat(jnp.arange(rows, dtype=F32), GRID_W)
    col = jnp.tile(jnp.arange(GRID_W, dtype=F32), rows)
    inv = jnp.exp(-math.log(ROPE_THETA) * jnp.arange(pairs, dtype=F32) / pairs)
    ar = row[:, None] * inv
    ac = col[:, None] * inv
    ang = jnp.concatenate([ar, ar, ac, ac], axis=-1)
    cos, sin = jnp.cos(ang), jnp.sin(ang)
    first = (jnp.arange(hd) % (2 * pairs)) < pairs
    sa = jnp.where(first, -sin, 0.0)
    sb = jnp.where(first, 0.0, sin)
    ones = jnp.ones((CTX, hd), F32)
    zeros = jnp.zeros((CTX, hd), F32)
    return (jnp.concatenate([ones, cos]), jnp.concatenate([zeros, sa]), jnp.concatenate([zeros, sb]))


def _column_plan(D):
    mix = D // 2
    att_q = mix
    att_kv = ATT_KV_HEADS * ATT_HEAD_DIM
    gla_v = mix
    gla_qk = mix // 2
    ssd_inner = mix
    ssd_bc = 2 * SSD_GROUPS * SSD_STATE
    order = [("att_q", att_q), ("gla_v", gla_v), ("gla_gate", gla_v), ("ssd_z", ssd_inner),
             ("ssd_x", ssd_inner), ("gla_q", gla_qk), ("gla_k", gla_qk), ("att_k", att_kv),
             ("att_v", att_kv), ("ssd_bc", ssd_bc)]
    cols, off = {}, 0
    for name, w in order:
        assert off % w == 0
        cols[name] = (off, w)
        off += w
    return cols, off


def _source_columns(D):
    mix = D // 2
    att_kv = ATT_KV_HEADS * ATT_HEAD_DIM
    ssd_bc = SSD_GROUPS * SSD_STATE
    heads = mix // SSD_HEAD_DIM
    widths = [("att_q", mix), ("att_k", att_kv), ("att_v", att_kv), ("gla_q", mix // 2), ("gla_k", mix // 2),
              ("gla_v", mix), ("gla_gate", mix), ("gla_lr", 2 * GLA_LOWRANK), ("ssd_z", mix),
              ("ssd_x", mix), ("ssd_bc", 2 * ssd_bc), ("ssd_dt", 2 * heads), ("gates", N_BRANCH * D)]
    src, off = {}, 0
    for name, w in widths:
        src[name] = (off, w)
        off += w
    return src, off


def kernel(x, c, ctx, c_ctx, norm1, norm2, w_ada, b_ada, w_in, att_q_norm, att_k_norm, gla_w_decay,
           gla_b_decay, gla_norm, ssd_conv_w, ssd_conv_b, ssd_dt_bias, ssd_a_log, ssd_d, ssd_norm,
           w_branch, w_out, w_ff1, w_ff2, final_norm):
    B, SEQ, D = x.shape
    CTX = ctx.shape[1]
    depth = w_in.shape[0]
    lay = _Layout(B, CTX, SEQ, D)
    cols, n_main = _column_plan(D)
    src, n_src = _source_columns(D)
    assert n_src == w_in.shape[2]
    mix = D // 2
    ssd_heads = mix // SSD_HEAD_DIM
    lr_w = 2 * GLA_LOWRANK
    assert lr_w + 2 * ssd_heads <= SMALL_COLS

    X = jnp.concatenate([ctx.reshape(B * CTX, D), x.reshape(B * SEQ, D)], axis=0)
    cvec = jnp.concatenate([c, c_ctx[None, :], jnp.zeros((MOD_ROWS - B - 1, D), F32)], axis=0)
    mods = _ada(cvec, w_ada, b_ada).reshape(depth, MOD_ROWS, 1, N_MOD * D)
    rope = _rope_tables(CTX, SEQ)

    expand = np.zeros((2, SMALL_COLS, mix), np.float32)
    for d in range(2):
        for h in range(ssd_heads):
            expand[d, lr_w + d * ssd_heads + h, h * SSD_HEAD_DIM:(h + 1) * SSD_HEAD_DIM] = 1.0
    expand = jnp.asarray(expand, BF16)

    row0_last = (B * CTX) // lay.tm
    out = None
    for l in range(depth):
        wl = w_in[l]
        take = lambda name: wl[:, src[name][0]:src[name][0] + src[name][1]]
        w_main = jnp.concatenate([take(n) for n, _ in sorted(cols.items(), key=lambda kv: kv[1][0])],
                                 axis=1).astype(BF16)
        w_small = jnp.concatenate([take("gla_lr"), take("ssd_dt"),
                                   jnp.zeros((D, SMALL_COLS - lr_w - 2 * ssd_heads), F32)], axis=1).astype(BF16)
        w_gate = take("gates").astype(BF16)
        mod = mods[l]
        g1 = norm1[l][None, :]

        U, small = _inproj(lay, X, mod, g1, w_main, w_small)

        qp, kt, vv = _qkprep(lay, U, cols, att_q_norm[l][None, :], att_k_norm[l][None, :], rope)
        o_att = _attention(lay, qp, kt, vv)

        gla_out = None
        for d in range(2):
            wd = jnp.zeros((SMALL_COLS, mix // 2), F32).at[d * GLA_LOWRANK:(d + 1) * GLA_LOWRANK].set(
                gla_w_decay[l, d]).astype(BF16)
            bd = gla_b_decay[l, d][None, :]
            if d == 0:
                gla_out = _gla_dir(lay, U, small, cols, wd, bd, rev=False)
            else:
                gla_out = _gla_dir(lay, U, small, cols, wd, bd, rev=True,
                                   norm_g=gla_norm[l][None, :], prev=gla_out)
        o_gla = gla_out

        cw, cbias = ssd_conv_w[l], ssd_conv_b[l]
        xc, bcc = _conv(lay, U, cols, cw[:, :mix], cbias[None, :mix], cw[:, mix:], cbias[None, mix:])
        ssd_out = None
        for d in range(2):
            lanes = slice(lr_w + d * ssd_heads, lr_w + (d + 1) * ssd_heads)
            dtb = jnp.zeros((1, SMALL_COLS), F32).at[0, lanes].set(ssd_dt_bias[l, d])
            apad = jnp.zeros((1, SMALL_COLS), F32).at[0, lanes].set(-jnp.exp(ssd_a_log[l, d]))
            if d == 0:
                ssd_out = _ssd_dir(lay, xc, bcc, small, U, cols, dtb, apad, expand[d], rev=False)
            else:
                ssd_out = _ssd_dir(lay, xc, bcc, small, U, cols, dtb, apad, expand[d], rev=True,
                                   dsk=jnp.repeat(ssd_d[l], SSD_HEAD_DIM)[None, :],
                                   norm_g=ssd_norm[l][None, :], prev=ssd_out)
        o_ssd = ssd_out

        last = l == depth - 1
        row0 = row0_last if last else 0
        m = _merge(lay, X, mod, g1, o_att, o_gla, o_ssd, w_gate, w_branch[l].astype(BF16), row0)
        X = _outproj(lay, m, w_out[l].astype(BF16), X, mod, row0)
        if last:
            out = _mlp(lay, X, mod, norm2[l][None, :], w_ff1[l].astype(BF16), w_ff2[l].astype(BF16), row0,
                       final_gain=final_norm[None, :])
        else:
            X = _mlp(lay, X, mod, norm2[l][None, :], w_ff1[l].astype(BF16), w_ff2[l].astype(BF16), row0)
    return out.reshape(B, SEQ, D)
```

```python
import functools
import math

import jax
import jax.numpy as jnp
import numpy as np
from jax import lax
from jax.experimental import pallas as pl
from jax.experimental.pallas import tpu as pltpu

F32 = jnp.float32
BF16 = jnp.bfloat16

EPS = 1e-6
GRID_W = 64
N_BRANCH = 3
N_MOD = 6
ATT_HEAD_DIM = 128
ATT_KV_HEADS = 2
ROPE_THETA = 10000.0
GLA_HEADS = 4
GLA_LOWRANK = 16
GLA_TAU = 16.0
GLA_CHUNK = 64
SSD_HEAD_DIM = 64
SSD_GROUPS = 2
SSD_STATE = 128
SSD_CONV = 5
SSD_CHUNK = 64

SEQ_BLOCK = 256
MOD_ROWS = 8
SMALL_COLS = 128
HALO_ROWS = 16
ATT_ONES_ROWS = 16
VMEM_LIMIT = 56 * 1024 * 1024


def _cparams(sem):
    return pltpu.CompilerParams(dimension_semantics=sem, vmem_limit_bytes=VMEM_LIMIT)


def _dot(a, b):
    return jnp.dot(a, b, preferred_element_type=F32)


def _dot_nt(a, b):
    return lax.dot_general(a, b, (((1,), (1,)), ((), ())), preferred_element_type=F32)


def _dot_tn(a, b):
    return lax.dot_general(a, b, (((0,), (0,)), ((), ())), preferred_element_type=F32)


def _split3(x):
    hi = x.astype(BF16)
    r = x - hi.astype(F32)
    mid = r.astype(BF16)
    lo = (r - mid.astype(F32)).astype(BF16)
    return hi, mid, lo


def _dot_exact_lhs(a, x):
    hi, mid, lo = _split3(x)
    return _dot(a, lo) + _dot(a, mid) + _dot(a, hi)


def _dot_exact_rhs(x, e):
    hi, mid, lo = _split3(x)
    return _dot(lo, e) + _dot(mid, e) + _dot(hi, e)


def _silu(x):
    return x * jax.nn.sigmoid(x)


def _softplus(x):
    return jnp.maximum(x, 0.0) + jnp.log1p(jnp.exp(-jnp.abs(x)))


def _block_tri(n, chunk, rev):
    r = lax.broadcasted_iota(jnp.int32, (n, n), 0)
    c = lax.broadcasted_iota(jnp.int32, (n, n), 1)
    shift = chunk.bit_length() - 1
    assert chunk == 1 << shift
    same = jnp.right_shift(r, shift) == jnp.right_shift(c, shift)
    tri = (c >= r) if rev else (c <= r)
    return jnp.where(same, jnp.where(tri, 1.0, 0.0), 0.0).astype(BF16)


class _Layout:
    def __init__(self, B, CTX, SEQ, D):
        assert CTX % SEQ_BLOCK == 0 and SEQ % SEQ_BLOCK == 0
        self.B, self.CTX, self.SEQ, self.D = B, CTX, SEQ, D
        self.R = B * (CTX + SEQ)
        self.NC = CTX // SEQ_BLOCK
        self.NL = SEQ // SEQ_BLOCK
        self.NS = self.NC + self.NL
        self.tm = next(t for t in (512, 256) if (B * CTX) % t == 0 and SEQ % t == 0)
        assert B + 1 <= MOD_ROWS

    def row_block(self, b, s):
        return jnp.where(s < self.NC, b * self.NC + s, self.B * self.NC + b * self.NL + (s - self.NC))

    def seq_order(self, n, rev):
        if not rev:
            return n
        return jnp.where(n < self.NC, self.NC - 1 - n, self.NC + self.NL - 1 - (n - self.NC))

    def mod_row(self, i):
        nctx = (self.B * self.CTX) // self.tm
        return jnp.where(i < nctx, self.B, (i - nctx) // (self.SEQ // self.tm))


def _ada_kernel(c_ref, w_ref, b_ref, o_ref):
    s = _silu(c_ref[...]).astype(BF16)
    o_ref[0] = _dot(s, w_ref[0].astype(BF16)) + b_ref[0]


def _ada(cvec, w_ada, b_ada):
    depth, D, n = w_ada.shape
    tn = 1024
    return pl.pallas_call(
        _ada_kernel,
        out_shape=jax.ShapeDtypeStruct((depth, MOD_ROWS, n), F32),
        grid=(depth, n // tn),
        in_specs=[
            pl.BlockSpec((MOD_ROWS, D), lambda l, j: (0, 0)),
            pl.BlockSpec((1, D, tn), lambda l, j: (l, 0, j)),
            pl.BlockSpec((1, 1, tn), lambda l, j: (l, 0, j)),
        ],
        out_specs=pl.BlockSpec((1, MOD_ROWS, tn), lambda l, j: (l, 0, j)),
        compiler_params=_cparams(("arbitrary", "arbitrary")),
        name="ada",
    )(cvec, w_ada, b_ada.reshape(depth, 1, n))


def _norm_mod(x_ref, g_ref, mod_ref, k, h_ref):
    D = x_ref.shape[-1]
    shift = mod_ref[:, k * D:(k + 1) * D]
    a = g_ref[...] * (1.0 + mod_ref[:, (k + 1) * D:(k + 2) * D])
    rc = 64

    def body(r, carry):
        sl = pl.ds(pl.multiple_of(r * rc, rc), rc)
        xf = x_ref[sl, :]
        ms = jnp.mean(xf * xf, axis=-1, keepdims=True)
        h_ref[sl, :] = (xf * lax.rsqrt(ms + EPS) * a + shift).astype(h_ref.dtype)
        return carry

    lax.fori_loop(0, x_ref.shape[0] // rc, body, 0)


def _inproj_kernel(x_ref, mod_ref, g_ref, w_ref, ws_ref, u_ref, small_ref, h_ref):
    @pl.when(pl.program_id(1) == 0)
    def _():
        _norm_mod(x_ref, g_ref, mod_ref, 0, h_ref)
        small_ref[...] = _dot(h_ref[...], ws_ref[...])

    u_ref[...] = _dot(h_ref[...], w_ref[...]).astype(u_ref.dtype)


def _inproj(lay, X, mod, g, w_main, w_small):
    D, n = w_main.shape
    tm, tn = lay.tm, 1024
    return pl.pallas_call(
        _inproj_kernel,
        out_shape=(jax.ShapeDtypeStruct((lay.R, n), BF16),
                   jax.ShapeDtypeStruct((lay.R, SMALL_COLS), F32)),
        grid=(lay.R // tm, n // tn),
        in_specs=[
            pl.BlockSpec((tm, D), lambda i, j: (i, 0)),
            pl.BlockSpec((None, 1, N_MOD * D), lambda i, j: (lay.mod_row(i), 0, 0)),
            pl.BlockSpec((1, D), lambda i, j: (0, 0)),
            pl.BlockSpec((D, tn), lambda i, j: (0, j)),
            pl.BlockSpec((D, SMALL_COLS), lambda i, j: (0, 0)),
        ],
        out_specs=(pl.BlockSpec((tm, tn), lambda i, j: (i, j)),
                   pl.BlockSpec((tm, SMALL_COLS), lambda i, j: (i, 0))),
        scratch_shapes=[pltpu.VMEM((tm, D), BF16)],
        compiler_params=_cparams(("arbitrary", "arbitrary")),
        name="in_proj",
    )(X, mod, g, w_main, w_small)


def _merge_kernel(x_ref, mod_ref, g_ref, oa_ref, og_ref, os_ref,
                  wg0_ref, wg1_ref, wg2_ref, wb0_ref, wb1_ref, wb2_ref, m_ref, h_ref):
    @pl.when(pl.program_id(1) == 0)
    def _():
        _norm_mod(x_ref, g_ref, mod_ref, 0, h_ref)

    h = h_ref[...]
    acc = None
    for o_ref, wg_ref, wb_ref in ((oa_ref, wg0_ref, wb0_ref), (og_ref, wg1_ref, wb1_ref),
                                  (os_ref, wg2_ref, wb2_ref)):
        t = jax.nn.sigmoid(_dot(h, wg_ref[...])) * _dot(o_ref[...], wb_ref[...])
        acc = t if acc is None else acc + t
    m_ref[...] = acc.astype(m_ref.dtype)


def _merge(lay, X, mod, g, o_att, o_gla, o_ssd, w_gate, w_branch, row0):
    D = lay.D
    W = o_att.shape[1]
    tm, tn = lay.tm, 512
    nj = D // tn
    rows = lambda i, j: (i + row0, 0)
    wg = lambda b: pl.BlockSpec((D, tn), lambda i, j: (0, b * nj + j))
    wb = lambda b: pl.BlockSpec((None, W, tn), lambda i, j: (b, 0, j))
    return pl.pallas_call(
        _merge_kernel,
        out_shape=jax.ShapeDtypeStruct((lay.R, D), BF16),
        grid=(lay.R // tm - row0, nj),
        in_specs=[
            pl.BlockSpec((tm, D), rows),
            pl.BlockSpec((None, 1, N_MOD * D), lambda i, j: (lay.mod_row(i + row0), 0, 0)),
            pl.BlockSpec((1, D), lambda i, j: (0, 0)),
            pl.BlockSpec((tm, W), rows), pl.BlockSpec((tm, W), rows), pl.BlockSpec((tm, W), rows),
            wg(0), wg(1), wg(2), wb(0), wb(1), wb(2),
        ],
        out_specs=pl.BlockSpec((tm, tn), lambda i, j: (i + row0, j)),
        scratch_shapes=[pltpu.VMEM((tm, D), BF16)],
        compiler_params=_cparams(("arbitrary", "arbitrary")),
        name="merge",
    )(X, mod, g, o_att, o_gla, o_ssd, w_gate, w_gate, w_gate, w_branch, w_branch, w_branch)


def _outproj_kernel(m_ref, w_ref, x_ref, gate_ref, o_ref):
    o_ref[...] = x_ref[...] + gate_ref[...] * _dot(m_ref[...], w_ref[...])


def _outproj(lay, m, w_out, X, mod, row0):
    D = lay.D
    tm, tn = lay.tm, 1024
    nj = D // tn
    return pl.pallas_call(
        _outproj_kernel,
        out_shape=jax.ShapeDtypeStruct((lay.R, D), F32),
        grid=(lay.R // tm - row0, nj),
        in_specs=[
            pl.BlockSpec((tm, D), lambda i, j: (i + row0, 0)),
            pl.BlockSpec((D, tn), lambda i, j: (0, j)),
            pl.BlockSpec((tm, tn), lambda i, j: (i + row0, j)),
            pl.BlockSpec((None, 1, tn), lambda i, j: (lay.mod_row(i + row0), 0, 2 * nj + j)),
        ],
        out_specs=pl.BlockSpec((tm, tn), lambda i, j: (i + row0, j)),
        input_output_aliases={2: 0},
        compiler_params=_cparams(("arbitrary", "arbitrary")),
        name="out_proj",
    )(m, w_out, X, mod)


def _mlp_kernel(x_ref, mod_ref, g_ref, w1_ref, w2_ref, *rest, final):
    if final:
        fg_ref, o_ref, h_ref = rest
    else:
        o_ref, h_ref = rest
    k = pl.program_id(1)
    D = x_ref.shape[-1]

    @pl.when(k == 0)
    def _():
        _norm_mod(x_ref, g_ref, mod_ref, 3, h_ref)

    f = jnp.square(jnp.maximum(_dot(h_ref[...], w1_ref[...]), 0.0)).astype(BF16)
    part = _dot(f, w2_ref[...])

    @pl.when(k == 0)
    def _():
        o_ref[...] = part

    @pl.when(k > 0)
    def _():
        o_ref[...] += part

    @pl.when(k == pl.num_programs(1) - 1)
    def _():
        gate = mod_ref[:, 5 * D:6 * D]
        rc = 64

        def body(r, carry):
            sl = pl.ds(pl.multiple_of(r * rc, rc), rc)
            y = x_ref[sl, :] + gate * o_ref[sl, :]
            if final:
                ms = jnp.mean(y * y, axis=-1, keepdims=True)
                y = y * lax.rsqrt(ms + EPS) * fg_ref[...]
            o_ref[sl, :] = y
            return carry

        lax.fori_loop(0, x_ref.shape[0] // rc, body, 0)


def _mlp(lay, X, mod, g, w1, w2, row0, final_gain=None):
    D, FF = w1.shape
    tm, tf = lay.tm, 512
    final = final_gain is not None
    out_rows = lay.R - row0 * tm if final else lay.R
    out_off = 0 if final else row0
    in_specs = [
        pl.BlockSpec((tm, D), lambda i, k: (i + row0, 0)),
        pl.BlockSpec((None, 1, N_MOD * D), lambda i, k: (lay.mod_row(i + row0), 0, 0)),
        pl.BlockSpec((1, D), lambda i, k: (0, 0)),
        pl.BlockSpec((D, tf), lambda i, k: (0, k)),
        pl.BlockSpec((tf, D), lambda i, k: (k, 0)),
    ]
    args = [X, mod, g, w1, w2]
    if final:
        in_specs.append(pl.BlockSpec((1, D), lambda i, k: (0, 0)))
        args.append(final_gain)
    return pl.pallas_call(
        functools.partial(_mlp_kernel, final=final),
        out_shape=jax.ShapeDtypeStruct((out_rows, D), F32),
        grid=(lay.R // tm - row0, FF // tf),
        in_specs=in_specs,
        out_specs=pl.BlockSpec((tm, D), lambda i, k: (i + out_off, 0)),
        scratch_shapes=[pltpu.VMEM((tm, D), BF16)],
        compiler_params=_cparams(("arbitrary", "arbitrary")),
        name="mlp",
    )(*args)


def _qkprep_kernel(q_ref, k_ref, v_ref, qg_ref, kg_ref, cos_ref, sa_ref, sb_ref,
                   qt_ref, ko_ref, vt_ref):
    cos, sa, sb = cos_ref[...], sa_ref[...], sb_ref[...]
    hd = ATT_HEAD_DIM

    def prep(x, g):
        xf = x.astype(F32)
        y = xf * lax.rsqrt(jnp.mean(xf * xf, axis=-1, keepdims=True) + EPS) * g
        return y * cos + pltpu.roll(y, hd - hd // 4, 1) * sa + pltpu.roll(y, hd // 4, 1) * sb

    q_scale = (hd ** -0.5) * math.log2(math.e)
    for h in range(q_ref.shape[1] // hd):
        sl = slice(h * hd, (h + 1) * hd)
        qt_ref[h] = (prep(q_ref[:, sl], qg_ref[...]) * q_scale).T.astype(qt_ref.dtype)
    for g in range(k_ref.shape[1] // hd):
        sl = slice(g * hd, (g + 1) * hd)
        ko_ref[:, sl] = prep(k_ref[:, sl], kg_ref[...]).astype(ko_ref.dtype)
        vt_ref[g, 0:hd, :] = v_ref[:, sl].astype(F32).T.astype(vt_ref.dtype)
        vt_ref[g, hd:, :] = jnp.ones((ATT_ONES_ROWS, v_ref.shape[0]), vt_ref.dtype)


def _qkprep(lay, U, cols, q_gain, k_gain, rope):
    B, NS, blk = lay.B, lay.NS, SEQ_BLOCK
    att_q = cols["att_q"][1]
    att_kv = cols["att_k"][1]
    hd = ATT_HEAD_DIM
    kvh = att_kv // hd
    heads = att_q // hd
    row = lambda b, n: lay.row_block(b, n)
    tab = pl.BlockSpec((blk, hd), lambda b, n: (n, 0))
    return pl.pallas_call(
        _qkprep_kernel,
        out_shape=(jax.ShapeDtypeStruct((lay.R // blk, heads, hd, blk), BF16),
                   jax.ShapeDtypeStruct((B, NS * blk, att_kv), BF16),
                   jax.ShapeDtypeStruct((B, kvh, NS, hd + ATT_ONES_ROWS, blk), BF16)),
        grid=(B, NS),
        in_specs=[
            pl.BlockSpec((blk, att_q), lambda b, n: (row(b, n), cols["att_q"][0] // att_q)),
            pl.BlockSpec((blk, att_kv), lambda b, n: (row(b, n), cols["att_k"][0] // att_kv)),
            pl.BlockSpec((blk, att_kv), lambda b, n: (row(b, n), cols["att_v"][0] // att_kv)),
            pl.BlockSpec((1, hd), lambda b, n: (0, 0)),
            pl.BlockSpec((1, hd), lambda b, n: (0, 0)),
            tab, tab, tab,
        ],
        out_specs=(pl.BlockSpec((None, heads, hd, blk), lambda b, n: (row(b, n), 0, 0, 0)),
                   pl.BlockSpec((None, blk, att_kv), lambda b, n: (b, n, 0)),
                   pl.BlockSpec((None, kvh, None, hd + ATT_ONES_ROWS, blk), lambda b, n: (b, 0, n, 0, 0))),
        compiler_params=_cparams(("arbitrary", "arbitrary")),
        name="qk_prep",
    )(U, U, U, q_gain, k_gain, *rope)


def _attn_kernel(qt_ref, k_ref, vt_ref, o_ref, m_ref, acc_ref, s0_ref, s1_ref, *, n_ctx_blocks):
    n = pl.program_id(2)
    heads, hd, tq = qt_ref.shape
    nblk, vrows, tk = vt_ref.shape
    n_kv = jnp.where(n < n_ctx_blocks, n_ctx_blocks, nblk)
    m_ref[...] = jnp.full(m_ref.shape, -1e30, F32)
    acc_ref[...] = jnp.zeros(acc_ref.shape, F32)

    def scores(c, s_ref):
        c = jnp.minimum(c, n_kv - 1)
        k = k_ref[pl.ds(pl.multiple_of(c * tk, tk), tk), :]
        for h in range(heads):
            s_ref[h] = _dot(k, qt_ref[h])

    def update(c, s_ref):
        vt = vt_ref[c]
        for h in range(heads):
            st = s_ref[h]
            m_old = m_ref[h]
            m_new = jnp.maximum(m_old, jnp.max(st, axis=0, keepdims=True))
            pt = jnp.exp2(st - m_new).astype(BF16)
            acc_ref[h] = jnp.exp2(m_old - m_new) * acc_ref[h] + _dot(vt, pt)
            m_ref[h] = m_new

    scores(0, s0_ref)

    def body(i, carry):
        scores(2 * i + 1, s1_ref)
        update(2 * i, s0_ref)
        scores(2 * i + 2, s0_ref)
        update(2 * i + 1, s1_ref)
        return carry

    lax.fori_loop(0, n_kv // 2, body, 0)

    @pl.when(n_kv % 2 == 1)
    def _():
        update(n_kv - 1, s0_ref)

    for h in range(heads):
        acc = acc_ref[h]
        ot = acc[0:hd] / acc[hd:hd + 1]
        o_ref[:, h * hd:(h + 1) * hd] = ot.T.astype(o_ref.dtype)


def _attention(lay, qt, kk, vt):
    B, kvh, NS, vrows, blk = vt.shape
    _, heads, hd, _ = qt.shape
    group = heads // kvh
    return pl.pallas_call(
        functools.partial(_attn_kernel, n_ctx_blocks=lay.NC),
        out_shape=jax.ShapeDtypeStruct((lay.R, heads * hd), BF16),
        grid=(B, kvh, NS),
        in_specs=[
            pl.BlockSpec((None, group, hd, blk), lambda b, g, n: (lay.row_block(b, n), g, 0, 0)),
            pl.BlockSpec((None, NS * blk, hd), lambda b, g, n: (b, 0, g)),
            pl.BlockSpec((None, None, NS, vrows, blk), lambda b, g, n: (b, g, 0, 0, 0)),
        ],
        out_specs=pl.BlockSpec((blk, group * hd), lambda b, g, n: (lay.row_block(b, n), g)),
        scratch_shapes=[pltpu.VMEM((group, 1, blk), F32), pltpu.VMEM((group, vrows, blk), F32),
                        pltpu.VMEM((group, blk, blk), F32), pltpu.VMEM((group, blk, blk), F32)],
        compiler_params=_cparams(("arbitrary", "arbitrary", "arbitrary")),
        name="attention",
    )(qt, kk, vt)


def _gla_kernel(*refs, rev, final, dk, dv):
    if final:
        q_ref, k_ref, v_ref, sm_ref, wd_ref, bd_ref, gate_ref, ng_ref, prev_ref, o_ref, st_ref = refs
    else:
        q_ref, k_ref, v_ref, sm_ref, wd_ref, bd_ref, o_ref, st_ref = refs
    blk = q_ref.shape[0]
    heads = q_ref.shape[1] // dk
    C = GLA_CHUNK

    @pl.when(pl.program_id(1) == 0)
    def _():
        st_ref[...] = jnp.zeros_like(st_ref)

    z = _dot(sm_ref[...].astype(BF16), wd_ref[...]) + bd_ref[...]
    la = (jnp.minimum(z, 0.0) - jnp.log1p(jnp.exp(-jnp.abs(z)))) * (1.0 / GLA_TAU)
    bcum = _dot_exact_lhs(_block_tri(blk, C, rev), la)

    ri = lax.broadcasted_iota(jnp.int32, (C, C), 0)
    ci = lax.broadcasted_iota(jnp.int32, (C, C), 1)
    causal = (ci >= ri) if rev else (ci <= ri)

    n_chunks = blk // C
    for step in range(n_chunks):
        c = n_chunks - 1 - step if rev else step
        rs = slice(c * C, (c + 1) * C)
        last = c * C if rev else (c + 1) * C - 1
        for h in range(heads):
            ks = slice(h * dk, (h + 1) * dk)
            vs = slice(h * dv, (h + 1) * dv)
            b = bcum[rs, ks]
            b_last = bcum[last:last + 1, ks]
            q = q_ref[rs, ks].astype(F32) * (dk ** -0.5)
            k = k_ref[rs, ks].astype(F32)
            v = v_ref[rs, vs]
            q_dec = (q * jnp.exp(b)).astype(BF16)
            k_dec = (k * jnp.exp(-b)).astype(BF16)
            k_st = (k * jnp.exp(b_last - b)).astype(BF16)
            att = jnp.where(causal, _dot_nt(q_dec, k_dec), 0.0).astype(BF16)
            st = st_ref[h]
            o = _dot(att, v) + _dot_nt(q_dec, st.astype(BF16))
            st_ref[h] = st * jnp.exp(b_last) + _dot_tn(v, k_st)
            if final:
                tot = o + prev_ref[rs, vs]
                y = tot * lax.rsqrt(jnp.mean(tot * tot, axis=-1, keepdims=True) + EPS) * ng_ref[...]
                o_ref[rs, vs] = (y * _silu(gate_ref[rs, vs].astype(F32))).astype(o_ref.dtype)
            else:
                o_ref[rs, vs] = o


def _gla_dir(lay, U, small, cols, w_dec, b_dec, rev, norm_g=None, prev=None):
    B, NS, blk = lay.B, lay.NS, SEQ_BLOCK
    qk_w = cols["gla_q"][1]
    v_w = cols["gla_v"][1]
    dk, dv = qk_w // GLA_HEADS, v_w // GLA_HEADS
    final = prev is not None
    rowi = lambda b, n: lay.row_block(b, lay.seq_order(n, rev))
    ucol = lambda name: pl.BlockSpec((blk, cols[name][1]), lambda b, n: (rowi(b, n), cols[name][0] // cols[name][1]))
    const = lambda shape: pl.BlockSpec(shape, lambda b, n: (0,) * len(shape))
    in_specs = [ucol("gla_q"), ucol("gla_k"), ucol("gla_v"),
                pl.BlockSpec((blk, SMALL_COLS), lambda b, n: (rowi(b, n), 0)),
                const((SMALL_COLS, qk_w)), const((1, qk_w))]
    args = [U, U, U, small, w_dec, b_dec]
    if final:
        in_specs += [ucol("gla_gate"), const((1, dv)), pl.BlockSpec((blk, v_w), lambda b, n: (rowi(b, n), 0))]
        args += [U, norm_g, prev]
    return pl.pallas_call(
        functools.partial(_gla_kernel, rev=rev, final=final, dk=dk, dv=dv),
        out_shape=jax.ShapeDtypeStruct((lay.R, v_w), BF16 if final else F32),
        grid=(B, NS),
        in_specs=in_specs,
        out_specs=pl.BlockSpec((blk, v_w), lambda b, n: (rowi(b, n), 0)),
        scratch_shapes=[pltpu.VMEM((GLA_HEADS, dv, dk), F32)],
        compiler_params=_cparams(("arbitrary", "arbitrary")),
        name="gla_bwd" if rev else "gla_fwd",
    )(*args)


def _conv_kernel(x_ref, xp_ref, xn_ref, bc_ref, bcp_ref, bcn_ref, wx_ref, bx_ref, wbc_ref, bbc_ref,
                 xo_ref, bco_ref, ex_ref, ebc_ref, *, n_ctx_total, NC, NL):
    i = pl.program_id(0)
    in_ctx = i < n_ctx_total
    p = jnp.where(in_ctx, i % NC, jnp.maximum(i - n_ctx_total, 0) % NL)
    seg = jnp.where(in_ctx, NC, NL)
    has_left = (p > 0).astype(F32)
    has_right = (p < seg - 1).astype(F32)
    blk = x_ref.shape[0]
    pad = SSD_CONV // 2

    def run(m_ref, p_ref, n_ref, w_ref, b_ref, e_ref, o_ref):
        e_ref[0:HALO_ROWS] = p_ref[...].astype(F32) * has_left
        e_ref[HALO_ROWS:HALO_ROWS + blk] = m_ref[...].astype(F32)
        e_ref[HALO_ROWS + blk:] = n_ref[...].astype(F32) * has_right
        acc = b_ref[...]
        for j in range(SSD_CONV):
            acc = acc + w_ref[j:j + 1, :] * e_ref[HALO_ROWS - pad + j:HALO_ROWS - pad + j + blk, :]
        o_ref[...] = _silu(acc).astype(o_ref.dtype)

    run(x_ref, xp_ref, xn_ref, wx_ref, bx_ref, ex_ref, xo_ref)
    run(bc_ref, bcp_ref, bcn_ref, wbc_ref, bbc_ref, ebc_ref, bco_ref)


def _conv(lay, U, cols, wx, bx, wbc, bbc):
    blk = SEQ_BLOCK
    nb = lay.R // blk
    hb = blk // HALO_ROWS
    nh = lay.R // HALO_ROWS
    xw = cols["ssd_x"][1]
    bcw = cols["ssd_bc"][1]
    xc, bcc = cols["ssd_x"][0] // xw, cols["ssd_bc"][0] // bcw
    prev = lambda i: jnp.maximum(i * hb - 1, 0)
    nxt = lambda i: jnp.minimum((i + 1) * hb, nh - 1)
    const = lambda shape: pl.BlockSpec(shape, lambda i: (0,) * len(shape))
    return pl.pallas_call(
        functools.partial(_conv_kernel, n_ctx_total=lay.B * lay.NC, NC=lay.NC, NL=lay.NL),
        out_shape=(jax.ShapeDtypeStruct((lay.R, xw), BF16), jax.ShapeDtypeStruct((lay.R, bcw), BF16)),
        grid=(nb,),
        in_specs=[
            pl.BlockSpec((blk, xw), lambda i: (i, xc)),
            pl.BlockSpec((HALO_ROWS, xw), lambda i: (prev(i), xc)),
            pl.BlockSpec((HALO_ROWS, xw), lambda i: (nxt(i), xc)),
            pl.BlockSpec((blk, bcw), lambda i: (i, bcc)),
            pl.BlockSpec((HALO_ROWS, bcw), lambda i: (prev(i), bcc)),
            pl.BlockSpec((HALO_ROWS, bcw), lambda i: (nxt(i), bcc)),
            const((SSD_CONV, xw)), const((1, xw)), const((SSD_CONV, bcw)), const((1, bcw)),
        ],
        out_specs=(pl.BlockSpec((blk, xw), lambda i: (i, 0)), pl.BlockSpec((blk, bcw), lambda i: (i, 0))),
        scratch_shapes=[pltpu.VMEM((blk + 2 * HALO_ROWS, xw), F32),
                        pltpu.VMEM((blk + 2 * HALO_ROWS, bcw), F32)],
        compiler_params=_cparams(("arbitrary",)),
        name="ssd_conv",
    )(U, U, U, U, U, U, wx, bx, wbc, bbc)


def _ssd_kernel(*refs, rev, final):
    if final:
        (x_ref, bc_ref, sm_ref, dtb_ref, apad_ref, e_ref, z_ref, dsk_ref, ng_ref, prev_ref,
         o_ref, st_ref, y_ref) = refs
    else:
        x_ref, bc_ref, sm_ref, dtb_ref, apad_ref, e_ref, o_ref, st_ref = refs
        y_ref = o_ref
    blk, inner = x_ref.shape
    N = SSD_STATE
    P = SSD_HEAD_DIM
    C = SSD_CHUNK
    gw = inner // SSD_GROUPS
    pair = 2 * P

    @pl.when(pl.program_id(1) == 0)
    def _():
        st_ref[...] = jnp.zeros_like(st_ref)

    dt16 = _softplus(sm_ref[...] + dtb_ref[...])
    cs16 = _dot_exact_lhs(_block_tri(blk, C, rev), dt16 * apad_ref[...])
    e = e_ref[...]
    dt_full = _dot_exact_rhs(dt16, e)
    cs_full = _dot_exact_rhs(cs16, e)

    ri = lax.broadcasted_iota(jnp.int32, (C, pair), 0)
    ci = lax.broadcasted_iota(jnp.int32, (C, pair), 1)
    cm = jnp.where(ci >= P, ci - P, ci)
    eye2 = cm == ri
    causal2 = (cm >= ri) if rev else (cm <= ri)
    left = ci < P
    zero = jnp.zeros((C, pair), BF16)

    n_chunks = blk // C
    for step in range(n_chunks):
        c = n_chunks - 1 - step if rev else step
        rs = slice(c * C, (c + 1) * C)
        last = c * C if rev else (c + 1) * C - 1
        for g in range(SSD_GROUPS):
            gs = slice(g * gw, (g + 1) * gw)
            bm = bc_ref[rs, g * N:(g + 1) * N]
            cmat = bc_ref[rs, SSD_GROUPS * N + g * N:SSD_GROUPS * N + (g + 1) * N]
            cb2 = _dot_nt(cmat, jnp.concatenate([bm, bm], axis=0))
            cs_g = cs_full[rs, gs]
            cs_last = cs_full[last:last + 1, gs]
            xdt = x_ref[rs, gs].astype(F32) * dt_full[rs, gs]
            st = st_ref[g]
            y_inter = _dot(cmat, st.astype(BF16)) * jnp.exp(cs_g)
            w = (xdt * jnp.exp(cs_last - cs_g)).astype(BF16)
            st_ref[g] = st * jnp.exp(cs_last) + _dot_tn(bm, w)
            xdt_b = xdt.astype(BF16)
            for pr in range(gw // pair):
                ps = slice(pr * pair, (pr + 1) * pair)
                col = cs_g[:, ps]
                rowv = jnp.sum(jnp.where(eye2, col, 0.0), axis=0, keepdims=True)
                lmat = jnp.exp(jnp.where(causal2, col - rowv, -jnp.inf))
                mm = (lmat * cb2).astype(BF16)
                xp = xdt_b[:, ps]
                rhs = jnp.concatenate([jnp.where(left, xp, zero), jnp.where(left, zero, xp)], axis=0)
                y = _dot(mm, rhs) + y_inter[:, ps]
                os_ = slice(g * gw + pr * pair, g * gw + (pr + 1) * pair)
                if final:
                    y_ref[rs, os_] = y + prev_ref[rs, os_] + x_ref[rs, os_].astype(F32) * dsk_ref[:, os_]
                else:
                    y_ref[rs, os_] = y

    if final:
        rc = 64
        for r in range(blk // rc):
            rs = slice(r * rc, (r + 1) * rc)
            for g in range(SSD_GROUPS):
                gs = slice(g * gw, (g + 1) * gw)
                yg = y_ref[rs, gs] * _silu(z_ref[rs, gs].astype(F32))
                ms = jnp.mean(yg * yg, axis=-1, keepdims=True)
                o_ref[rs, gs] = (yg * lax.rsqrt(ms + EPS) * ng_ref[:, gs]).astype(o_ref.dtype)


def _ssd_dir(lay, xc, bcc, small, U, cols, dtb, apad, expand, rev, dsk=None, norm_g=None, prev=None):
    B, NS, blk = lay.B, lay.NS, SEQ_BLOCK
    inner = xc.shape[1]
    bcw = bcc.shape[1]
    final = prev is not None
    rowi = lambda b, n: lay.row_block(b, lay.seq_order(n, rev))
    rows = lambda w: pl.BlockSpec((blk, w), lambda b, n: (rowi(b, n), 0))
    const = lambda shape: pl.BlockSpec(shape, lambda b, n: (0,) * len(shape))
    in_specs = [rows(inner), rows(bcw), rows(SMALL_COLS), const((1, SMALL_COLS)), const((1, SMALL_COLS)),
                const((SMALL_COLS, inner))]
    args = [xc, bcc, small, dtb, apad, expand]
    if final:
        zc = cols["ssd_z"]
        in_specs += [pl.BlockSpec((blk, inner), lambda b, n: (rowi(b, n), zc[0] // zc[1])),
                     const((1, inner)), const((1, inner)), rows(inner)]
        args += [U, dsk, norm_g, prev]
    return pl.pallas_call(
        functools.partial(_ssd_kernel, rev=rev, final=final),
        out_shape=jax.ShapeDtypeStruct((lay.R, inner), BF16 if final else F32),
        grid=(B, NS),
        in_specs=in_specs,
        out_specs=rows(inner),
        scratch_shapes=[pltpu.VMEM((SSD_GROUPS, SSD_STATE, inner // SSD_GROUPS), F32)]
        + ([pltpu.VMEM((blk, inner), F32)] if final else []),
        compiler_params=_cparams(("arbitrary", "arbitrary")),
        name="ssd_bwd" if rev else "ssd_fwd",
    )(*args)


def _rope_tables(CTX, SEQ):
    hd = ATT_HEAD_DIM
    pairs = hd // 4
    rows = SEQ // GRID_W
    row = jnp.repeat(jnp.arange(rows, dtype=F32), GRID_W)
    col = jnp.tile(jnp.arange(GRID_W, dtype=F32), rows)
    inv = jnp.exp(-math.log(ROPE_THETA) * jnp.arange(pairs, dtype=F32) / pairs)
    ar = row[:, None] * inv
    ac = col[:, None] * inv
    ang = jnp.concatenate([ar, ar, ac, ac], axis=-1)
    cos, sin = jnp.cos(ang), jnp.sin(ang)
    first = (jnp.arange(hd) % (2 * pairs)) < pairs
    sa = jnp.where(first, -sin, 0.0)
    sb = jnp.where(first, 0.0, sin)
    ones = jnp.ones((CTX, hd), F32)
    zeros = jnp.zeros((CTX, hd), F32)
    return (jnp.concatenate([ones, cos]), jnp.concatenate([zeros, sa]), jnp.concatenate([zeros, sb]))


def _column_plan(D):
    mix = D // 2
    att_q = mix
    att_kv = ATT_KV_HEADS * ATT_HEAD_DIM
    gla_v = mix
    gla_qk = mix // 2
    ssd_inner = mix
    ssd_bc = 2 * SSD_GROUPS * SSD_STATE
    order = [("att_q", att_q), ("gla_v", gla_v), ("gla_gate", gla_v), ("ssd_z", ssd_inner),
             ("ssd_x", ssd_inner), ("gla_q", gla_qk), ("gla_k", gla_qk), ("att_k", att_kv),
             ("att_v", att_kv), ("ssd_bc", ssd_bc)]
    cols, off = {}, 0
    for name, w in order:
        assert off % w == 0
        cols[name] = (off, w)
        off += w
    return cols, off


def _source_columns(D):
    mix = D // 2
    att_kv = ATT_KV_HEADS * ATT_HEAD_DIM
    ssd_bc = SSD_GROUPS * SSD_STATE
    heads = mix // SSD_HEAD_DIM
    widths = [("att_q", mix), ("att_k", att_kv), ("att_v", att_kv), ("gla_q", mix // 2), ("gla_k", mix // 2),
              ("gla_v", mix), ("gla_gate", mix), ("gla_lr", 2 * GLA_LOWRANK), ("ssd_z", mix),
              ("ssd_x", mix), ("ssd_bc", 2 * ssd_bc), ("ssd_dt", 2 * heads), ("gates", N_BRANCH * D)]
    src, off = {}, 0
    for name, w in widths:
        src[name] = (off, w)
        off += w
    return src, off


def kernel(x, c, ctx, c_ctx, norm1, norm2, w_ada, b_ada, w_in, att_q_norm, att_k_norm, gla_w_decay,
           gla_b_decay, gla_norm, ssd_conv_w, ssd_conv_b, ssd_dt_bias, ssd_a_log, ssd_d, ssd_norm,
           w_branch, w_out, w_ff1, w_ff2, final_norm):
    B, SEQ, D = x.shape
    CTX = ctx.shape[1]
    depth = w_in.shape[0]
    lay = _Layout(B, CTX, SEQ, D)
    cols, n_main = _column_plan(D)
    src, n_src = _source_columns(D)
    assert n_src == w_in.shape[2]
    mix = D // 2
    ssd_heads = mix // SSD_HEAD_DIM
    lr_w = 2 * GLA_LOWRANK
    assert lr_w + 2 * ssd_heads <= SMALL_COLS

    X = jnp.concatenate([ctx.reshape(B * CTX, D), x.reshape(B * SEQ, D)], axis=0)
    cvec = jnp.concatenate([c, c_ctx[None, :], jnp.zeros((MOD_ROWS - B - 1, D), F32)], axis=0)
    mods = _ada(cvec, w_ada, b_ada).reshape(depth, MOD_ROWS, 1, N_MOD * D)
    rope = _rope_tables(CTX, SEQ)

    expand = np.zeros((2, SMALL_COLS, mix), np.float32)
    for d in range(2):
        for h in range(ssd_heads):
            expand[d, lr_w + d * ssd_heads + h, h * SSD_HEAD_DIM:(h + 1) * SSD_HEAD_DIM] = 1.0
    expand = jnp.asarray(expand, BF16)

    row0_last = (B * CTX) // lay.tm
    out = None
    for l in range(depth):
        wl = w_in[l]
        take = lambda name: wl[:, src[name][0]:src[name][0] + src[name][1]]
        w_main = jnp.concatenate([take(n) for n, _ in sorted(cols.items(), key=lambda kv: kv[1][0])],
                                 axis=1).astype(BF16)
        w_small = jnp.concatenate([take("gla_lr"), take("ssd_dt"),
                                   jnp.zeros((D, SMALL_COLS - lr_w - 2 * ssd_heads), F32)], axis=1).astype(BF16)
        w_gate = take("gates").astype(BF16)
        mod = mods[l]
        g1 = norm1[l][None, :]

        U, small = _inproj(lay, X, mod, g1, w_main, w_small)

        qt, kk, vt = _qkprep(lay, U, cols, att_q_norm[l][None, :], att_k_norm[l][None, :], rope)
        o_att = _attention(lay, qt, kk, vt)

        gla_out = None
        for d in range(2):
            wd = jnp.zeros((SMALL_COLS, mix // 2), F32).at[d * GLA_LOWRANK:(d + 1) * GLA_LOWRANK].set(
                gla_w_decay[l, d]).astype(BF16)
            bd = gla_b_decay[l, d][None, :]
            if d == 0:
                gla_out = _gla_dir(lay, U, small, cols, wd, bd, rev=False)
            else:
                gla_out = _gla_dir(lay, U, small, cols, wd, bd, rev=True,
                                   norm_g=gla_norm[l][None, :], prev=gla_out)
        o_gla = gla_out

        cw, cbias = ssd_conv_w[l], ssd_conv_b[l]
        xc, bcc = _conv(lay, U, cols, cw[:, :mix], cbias[None, :mix], cw[:, mix:], cbias[None, mix:])
        ssd_out = None
        for d in range(2):
            lanes = slice(lr_w + d * ssd_heads, lr_w + (d + 1) * ssd_heads)
            dtb = jnp.zeros((1, SMALL_COLS), F32).at[0, lanes].set(ssd_dt_bias[l, d])
            apad = jnp.zeros((1, SMALL_COLS), F32).at[0, lanes].set(-jnp.exp(ssd_a_log[l, d]))
            if d == 0:
                ssd_out = _ssd_dir(lay, xc, bcc, small, U, cols, dtb, apad, expand[d], rev=False)
            else:
                ssd_out = _ssd_dir(lay, xc, bcc, small, U, cols, dtb, apad, expand[d], rev=True,
                                   dsk=jnp.repeat(ssd_d[l], SSD_HEAD_DIM)[None, :],
                                   norm_g=ssd_norm[l][None, :], prev=ssd_out)
        o_ssd = ssd_out

        last = l == depth - 1
        row0 = row0_last if last else 0
        m = _merge(lay, X, mod, g1, o_att, o_gla, o_ssd, w_gate, w_branch[l].astype(BF16), row0)
        X = _outproj(lay, m, w_out[l].astype(BF16), X, mod, row0)
        if last:
            out = _mlp(lay, X, mod, norm2[l][None, :], w_ff1[l].astype(BF16), w_ff2[l].astype(BF16), row0,
                       final_gain=final_norm[None, :])
        else:
            X = _mlp(lay, X, mod, norm2[l][None, :], w_ff1[l].astype(BF16), w_ff2[l].astype(BF16), row0)
    return out.reshape(B, SEQ, D)
```

```python
import functools
import math

import jax
import jax.numpy as jnp
import numpy as np
from jax import lax
from jax.experimental import pallas as pl
from jax.experimental.pallas import tpu as pltpu

F32 = jnp.float32
BF16 = jnp.bfloat16

EPS = 1e-6
GRID_W = 64
N_BRANCH = 3
N_MOD = 6
ATT_HEAD_DIM = 128
ATT_KV_HEADS = 2
ROPE_THETA = 10000.0
GLA_HEADS = 4
GLA_LOWRANK = 16
GLA_TAU = 16.0
GLA_CHUNK = 64
SSD_HEAD_DIM = 64
SSD_GROUPS = 2
SSD_STATE = 128
SSD_CONV = 5
SSD_CHUNK = 64

SEQ_BLOCK = 256
RESIDENT_TM = 256
MOD_ROWS = 8
SMALL_COLS = 128
HALO_ROWS = 16
ATT_ONES_ROWS = 16
VMEM_LIMIT = 56 * 1024 * 1024


def _cparams(sem):
    return pltpu.CompilerParams(dimension_semantics=sem, vmem_limit_bytes=VMEM_LIMIT)


def _dot(a, b):
    return jnp.dot(a, b, preferred_element_type=F32)


def _dot_nt(a, b):
    return lax.dot_general(a, b, (((1,), (1,)), ((), ())), preferred_element_type=F32)


def _dot_tn(a, b):
    return lax.dot_general(a, b, (((0,), (0,)), ((), ())), preferred_element_type=F32)


def _split3(x):
    hi = x.astype(BF16)
    r = x - hi.astype(F32)
    mid = r.astype(BF16)
    lo = (r - mid.astype(F32)).astype(BF16)
    return hi, mid, lo


def _dot_exact_lhs(a, x):
    hi, mid, lo = _split3(x)
    return _dot(a, lo) + _dot(a, mid) + _dot(a, hi)


def _dot_exact_rhs(x, e):
    hi, mid, lo = _split3(x)
    return _dot(lo, e) + _dot(mid, e) + _dot(hi, e)


def _silu(x):
    return x * jax.nn.sigmoid(x)


def _softplus(x):
    return jnp.maximum(x, 0.0) + jnp.log1p(jnp.exp(-jnp.abs(x)))


def _block_tri(n, chunk, rev):
    r = lax.broadcasted_iota(jnp.int32, (n, n), 0)
    c = lax.broadcasted_iota(jnp.int32, (n, n), 1)
    shift = chunk.bit_length() - 1
    assert chunk == 1 << shift
    same = jnp.right_shift(r, shift) == jnp.right_shift(c, shift)
    tri = (c >= r) if rev else (c <= r)
    return jnp.where(same, jnp.where(tri, 1.0, 0.0), 0.0).astype(BF16)


class _Layout:
    def __init__(self, B, CTX, SEQ, D):
        assert CTX % SEQ_BLOCK == 0 and SEQ % SEQ_BLOCK == 0
        self.B, self.CTX, self.SEQ, self.D = B, CTX, SEQ, D
        self.R = B * (CTX + SEQ)
        self.NC = CTX // SEQ_BLOCK
        self.NL = SEQ // SEQ_BLOCK
        self.NS = self.NC + self.NL
        self.tm = next(t for t in (512, 256) if (B * CTX) % t == 0 and SEQ % t == 0)
        assert B + 1 <= MOD_ROWS

    def row_block(self, b, s):
        return jnp.where(s < self.NC, b * self.NC + s, self.B * self.NC + b * self.NL + (s - self.NC))

    def seq_order(self, n, rev):
        if not rev:
            return n
        return jnp.where(n < self.NC, self.NC - 1 - n, self.NC + self.NL - 1 - (n - self.NC))

    def mod_row(self, i, tm):
        nctx = (self.B * self.CTX) // tm
        return jnp.where(i < nctx, self.B, (i - nctx) // (self.SEQ // tm))


def _ada_kernel(c_ref, w_ref, b_ref, o_ref):
    s = _silu(c_ref[...]).astype(BF16)
    o_ref[0] = _dot(s, w_ref[0].astype(BF16)) + b_ref[0]


def _ada(cvec, w_ada, b_ada):
    depth, D, n = w_ada.shape
    tn = 1024
    return pl.pallas_call(
        _ada_kernel,
        out_shape=jax.ShapeDtypeStruct((depth, MOD_ROWS, n), F32),
        grid=(depth, n // tn),
        in_specs=[
            pl.BlockSpec((MOD_ROWS, D), lambda l, j: (0, 0)),
            pl.BlockSpec((1, D, tn), lambda l, j: (l, 0, j)),
            pl.BlockSpec((1, 1, tn), lambda l, j: (l, 0, j)),
        ],
        out_specs=pl.BlockSpec((1, MOD_ROWS, tn), lambda l, j: (l, 0, j)),
        compiler_params=_cparams(("arbitrary", "arbitrary")),
        name="ada",
    )(cvec, w_ada, b_ada.reshape(depth, 1, n))


def _norm_mod(x_ref, g_ref, mod_ref, k, h_ref):
    D = x_ref.shape[-1]
    shift = mod_ref[:, k * D:(k + 1) * D]
    a = g_ref[...] * (1.0 + mod_ref[:, (k + 1) * D:(k + 2) * D])
    rc = 64

    def body(r, carry):
        sl = pl.ds(pl.multiple_of(r * rc, rc), rc)
        xf = x_ref[sl, :]
        ms = jnp.mean(xf * xf, axis=-1, keepdims=True)
        h_ref[sl, :] = (xf * lax.rsqrt(ms + EPS) * a + shift).astype(h_ref.dtype)
        return carry

    lax.fori_loop(0, x_ref.shape[0] // rc, body, 0)


def _resident(shape, layer=None):
    if layer is None:
        return pl.BlockSpec(shape, lambda *_: (0,) * len(shape), pipeline_mode=pl.Buffered(1))
    return pl.BlockSpec((None,) + tuple(shape), lambda *_: (layer,) + (0,) * len(shape),
                        pipeline_mode=pl.Buffered(1))


def _inproj_kernel(x_ref, mod_ref, g_ref, w_ref, ws_ref, u_ref, small_ref, h_ref, *, tn):
    _norm_mod(x_ref, g_ref, mod_ref, 0, h_ref)
    h = h_ref[...]
    small_ref[...] = _dot(h, ws_ref[...])
    for j in range(w_ref.shape[1] // tn):
        sl = slice(j * tn, (j + 1) * tn)
        u_ref[:, sl] = _dot(h, w_ref[:, sl]).astype(u_ref.dtype)


def _inproj(lay, X, mod, g, w_main, w_small, l):
    _, D, n = w_main.shape
    tm = RESIDENT_TM
    return pl.pallas_call(
        functools.partial(_inproj_kernel, tn=512),
        out_shape=(jax.ShapeDtypeStruct((lay.R, n), BF16),
                   jax.ShapeDtypeStruct((lay.R, SMALL_COLS), F32)),
        grid=(lay.R // tm,),
        in_specs=[
            pl.BlockSpec((tm, D), lambda i: (i, 0)),
            pl.BlockSpec((None, 1, N_MOD * D), lambda i: (lay.mod_row(i, tm), 0, 0)),
            _resident((1, D)),
            _resident((D, n), l),
            _resident((D, SMALL_COLS), l),
        ],
        out_specs=(pl.BlockSpec((tm, n), lambda i: (i, 0)),
                   pl.BlockSpec((tm, SMALL_COLS), lambda i: (i, 0))),
        scratch_shapes=[pltpu.VMEM((tm, D), BF16)],
        compiler_params=_cparams(("arbitrary",)),
        name="in_proj",
    )(X, mod, g, w_main, w_small)


def _merge_kernel(x_ref, mod_ref, g_ref, oa_ref, og_ref, os_ref, wg_ref, wb_ref, m_ref, h_ref, *, tn):
    _norm_mod(x_ref, g_ref, mod_ref, 0, h_ref)
    h = h_ref[...]
    D = m_ref.shape[1]
    for j in range(D // tn):
        acc = None
        for b, o_ref in enumerate((oa_ref, og_ref, os_ref)):
            gate = _dot(h, wg_ref[:, b * D + j * tn:b * D + (j + 1) * tn])
            t = jax.nn.sigmoid(gate) * _dot(o_ref[...], wb_ref[b, :, j * tn:(j + 1) * tn])
            acc = t if acc is None else acc + t
        m_ref[:, j * tn:(j + 1) * tn] = acc.astype(m_ref.dtype)


def _merge(lay, X, mod, g, o_att, o_gla, o_ssd, w_gate, w_branch, l, skip_rows):
    D = lay.D
    W = o_att.shape[1]
    tm = RESIDENT_TM
    row0 = skip_rows // tm
    rows = lambda i: (i + row0, 0)
    return pl.pallas_call(
        functools.partial(_merge_kernel, tn=512),
        out_shape=jax.ShapeDtypeStruct((lay.R, D), BF16),
        grid=(lay.R // tm - row0,),
        in_specs=[
            pl.BlockSpec((tm, D), rows),
            pl.BlockSpec((None, 1, N_MOD * D), lambda i: (lay.mod_row(i + row0, tm), 0, 0)),
            _resident((1, D)),
            pl.BlockSpec((tm, W), rows), pl.BlockSpec((tm, W), rows), pl.BlockSpec((tm, W), rows),
            _resident(w_gate.shape[1:], l), _resident(w_branch.shape[1:], l),
        ],
        out_specs=pl.BlockSpec((tm, D), rows),
        scratch_shapes=[pltpu.VMEM((tm, D), BF16)],
        compiler_params=_cparams(("arbitrary",)),
        name="merge",
    )(X, mod, g, o_att, o_gla, o_ssd, w_gate, w_branch)


def _mlp_kernel(x_ref, m_ref, wo_ref, mod_ref, g_ref, w1_ref, w2_ref, *rest, final):
    if final:
        fg_ref, o_ref, x1_ref, h_ref = rest
    else:
        o_ref, x1_ref, h_ref = rest
    k = pl.program_id(1)
    D = x_ref.shape[-1]

    @pl.when(k == 0)
    def _():
        tn = 512
        for j in range(D // tn):
            sl = slice(j * tn, (j + 1) * tn)
            x1_ref[:, sl] = x_ref[:, sl] + mod_ref[:, 2 * D + j * tn:2 * D + (j + 1) * tn] * _dot(
                m_ref[...], wo_ref[:, sl])
        _norm_mod(x1_ref, g_ref, mod_ref, 3, h_ref)
        o_ref[...] = jnp.zeros_like(o_ref)

    f = jnp.square(jnp.maximum(_dot(h_ref[...], w1_ref[...]), 0.0)).astype(BF16)
    o_ref[...] += _dot(f, w2_ref[...])

    @pl.when(k == pl.num_programs(1) - 1)
    def _():
        gate = mod_ref[:, 5 * D:6 * D]
        rc = 64

        def body(r, carry):
            sl = pl.ds(pl.multiple_of(r * rc, rc), rc)
            y = x1_ref[sl, :] + gate * o_ref[sl, :]
            if final:
                ms = jnp.mean(y * y, axis=-1, keepdims=True)
                y = y * lax.rsqrt(ms + EPS) * fg_ref[...]
            o_ref[sl, :] = y
            return carry

        lax.fori_loop(0, x_ref.shape[0] // rc, body, 0)


def _mlp(lay, X, m, w_out, mod, g, w1, w2, l, skip_rows, final_gain=None):
    _, D, FF = w1.shape
    tm, tf = lay.tm, 1024
    row0 = skip_rows // tm
    final = final_gain is not None
    out_rows = lay.R - skip_rows if final else lay.R
    out_off = 0 if final else row0
    rows = lambda i, k: (i + row0, 0)
    in_specs = [
        pl.BlockSpec((tm, D), rows),
        pl.BlockSpec((tm, D), rows),
        _resident((D, D), l),
        pl.BlockSpec((None, 1, N_MOD * D), lambda i, k: (lay.mod_row(i + row0, tm), 0, 0)),
        _resident((1, D)),
        pl.BlockSpec((None, D, tf), lambda i, k: (l, 0, k)),
        pl.BlockSpec((None, tf, D), lambda i, k: (l, k, 0)),
    ]
    args = [X, m, w_out, mod, g, w1, w2]
    if final:
        in_specs.append(_resident((1, D)))
        args.append(final_gain)
    return pl.pallas_call(
        functools.partial(_mlp_kernel, final=final),
        out_shape=jax.ShapeDtypeStruct((out_rows, D), F32),
        grid=(lay.R // tm - row0, FF // tf),
        in_specs=in_specs,
        out_specs=pl.BlockSpec((tm, D), lambda i, k: (i + out_off, 0)),
        scratch_shapes=[pltpu.VMEM((tm, D), F32), pltpu.VMEM((tm, D), BF16)],
        compiler_params=_cparams(("arbitrary", "arbitrary")),
        name="mlp",
    )(*args)


def _qkprep_kernel(q_ref, k_ref, v_ref, qg_ref, kg_ref, cos_ref, sa_ref, sb_ref,
                   qt_ref, ko_ref, vt_ref):
    cos, sa, sb = cos_ref[...], sa_ref[...], sb_ref[...]
    hd = ATT_HEAD_DIM

    def prep(x, g):
        xf = x.astype(F32)
        y = xf * lax.rsqrt(jnp.mean(xf * xf, axis=-1, keepdims=True) + EPS) * g
        return y * cos + pltpu.roll(y, hd - hd // 4, 1) * sa + pltpu.roll(y, hd // 4, 1) * sb

    q_scale = (hd ** -0.5) * math.log2(math.e)
    for h in range(q_ref.shape[1] // hd):
        sl = slice(h * hd, (h + 1) * hd)
        qt_ref[h] = (prep(q_ref[:, sl], qg_ref[...]) * q_scale).T.astype(qt_ref.dtype)
    for g in range(k_ref.shape[1] // hd):
        sl = slice(g * hd, (g + 1) * hd)
        ko_ref[:, sl] = prep(k_ref[:, sl], kg_ref[...]).astype(ko_ref.dtype)
        vt_ref[g, 0:hd, :] = v_ref[:, sl].astype(F32).T.astype(vt_ref.dtype)
        vt_ref[g, hd:, :] = jnp.ones((ATT_ONES_ROWS, v_ref.shape[0]), vt_ref.dtype)


def _qkprep(lay, U, cols, q_gain, k_gain, rope):
    B, NS, blk = lay.B, lay.NS, SEQ_BLOCK
    att_q = cols["att_q"][1]
    att_kv = cols["att_k"][1]
    hd = ATT_HEAD_DIM
    kvh = att_kv // hd
    heads = att_q // hd
    row = lambda b, n: lay.row_block(b, n)
    tab = pl.BlockSpec((blk, hd), lambda b, n: (n, 0))
    return pl.pallas_call(
        _qkprep_kernel,
        out_shape=(jax.ShapeDtypeStruct((lay.R // blk, heads, hd, blk), BF16),
                   jax.ShapeDtypeStruct((B, NS * blk, att_kv), BF16),
                   jax.ShapeDtypeStruct((B, kvh, NS, hd + ATT_ONES_ROWS, blk), BF16)),
        grid=(B, NS),
        in_specs=[
            pl.BlockSpec((blk, att_q), lambda b, n: (row(b, n), cols["att_q"][0] // att_q)),
            pl.BlockSpec((blk, att_kv), lambda b, n: (row(b, n), cols["att_k"][0] // att_kv)),
            pl.BlockSpec((blk, att_kv), lambda b, n: (row(b, n), cols["att_v"][0] // att_kv)),
            pl.BlockSpec((1, hd), lambda b, n: (0, 0)),
            pl.BlockSpec((1, hd), lambda b, n: (0, 0)),
            tab, tab, tab,
        ],
        out_specs=(pl.BlockSpec((None, heads, hd, blk), lambda b, n: (row(b, n), 0, 0, 0)),
                   pl.BlockSpec((None, blk, att_kv), lambda b, n: (b, n, 0)),
                   pl.BlockSpec((None, kvh, None, hd + ATT_ONES_ROWS, blk), lambda b, n: (b, 0, n, 0, 0))),
        compiler_params=_cparams(("arbitrary", "arbitrary")),
        name="qk_prep",
    )(U, U, U, q_gain, k_gain, *rope)


def _attn_kernel(qt_ref, k_ref, vt_ref, o_ref, m_ref, acc_ref, s0_ref, s1_ref, *, n_ctx_blocks):
    n = pl.program_id(2)
    heads, hd, tq = qt_ref.shape
    nblk, vrows, tk = vt_ref.shape
    n_kv = jnp.where(n < n_ctx_blocks, n_ctx_blocks, nblk)
    m_ref[...] = jnp.full(m_ref.shape, -1e30, F32)
    acc_ref[...] = jnp.zeros(acc_ref.shape, F32)

    def scores(c, s_ref):
        c = jnp.minimum(c, n_kv - 1)
        k = k_ref[pl.ds(pl.multiple_of(c * tk, tk), tk), :]
        for h in range(heads):
            s_ref[h] = _dot(k, qt_ref[h])

    def update(c, s_ref):
        vt = vt_ref[c]
        for h in range(heads):
            st = s_ref[h]
            m_old = m_ref[h]
            m_new = jnp.maximum(m_old, jnp.max(st, axis=0, keepdims=True))
            pt = jnp.exp2(st - m_new).astype(BF16)
            acc_ref[h] = jnp.exp2(m_old - m_new) * acc_ref[h] + _dot(vt, pt)
            m_ref[h] = m_new

    scores(0, s0_ref)

    def body(i, carry):
        scores(2 * i + 1, s1_ref)
        update(2 * i, s0_ref)
        scores(2 * i + 2, s0_ref)
        update(2 * i + 1, s1_ref)
        return carry

    lax.fori_loop(0, n_kv // 2, body, 0)

    @pl.when(n_kv % 2 == 1)
    def _():
        update(n_kv - 1, s0_ref)

    for h in range(heads):
        acc = acc_ref[h]
        ot = acc[0:hd] / acc[hd:hd + 1]
        o_ref[:, h * hd:(h + 1) * hd] = ot.T.astype(o_ref.dtype)


def _attention(lay, qt, kk, vt):
    B, kvh, NS, vrows, blk = vt.shape
    _, heads, hd, _ = qt.shape
    group = heads // kvh
    return pl.pallas_call(
        functools.partial(_attn_kernel, n_ctx_blocks=lay.NC),
        out_shape=jax.ShapeDtypeStruct((lay.R, heads * hd), BF16),
        grid=(B, kvh, NS),
        in_specs=[
            pl.BlockSpec((None, group, hd, blk), lambda b, g, n: (lay.row_block(b, n), g, 0, 0)),
            pl.BlockSpec((None, NS * blk, hd), lambda b, g, n: (b, 0, g)),
            pl.BlockSpec((None, None, NS, vrows, blk), lambda b, g, n: (b, g, 0, 0, 0)),
        ],
        out_specs=pl.BlockSpec((blk, group * hd), lambda b, g, n: (lay.row_block(b, n), g)),
        scratch_shapes=[pltpu.VMEM((group, 1, blk), F32), pltpu.VMEM((group, vrows, blk), F32),
                        pltpu.VMEM((group, blk, blk), F32), pltpu.VMEM((group, blk, blk), F32)],
        compiler_params=_cparams(("arbitrary", "arbitrary", "arbitrary")),
        name="attention",
    )(qt, kk, vt)


def _gla_kernel(*refs, rev, final, dk, dv):
    if final:
        q_ref, k_ref, v_ref, sm_ref, wd_ref, bd_ref, gate_ref, ng_ref, prev_ref, o_ref, st_ref = refs
    else:
        q_ref, k_ref, v_ref, sm_ref, wd_ref, bd_ref, o_ref, st_ref = refs
    blk = q_ref.shape[0]
    heads = q_ref.shape[1] // dk
    C = GLA_CHUNK

    @pl.when(pl.program_id(1) == 0)
    def _():
        st_ref[...] = jnp.zeros_like(st_ref)

    z = _dot(sm_ref[...].astype(BF16), wd_ref[...]) + bd_ref[...]
    la = (jnp.minimum(z, 0.0) - jnp.log1p(jnp.exp(-jnp.abs(z)))) * (1.0 / GLA_TAU)
    bcum = _dot_exact_lhs(_block_tri(blk, C, rev), la)

    ri = lax.broadcasted_iota(jnp.int32, (C, C), 0)
    ci = lax.broadcasted_iota(jnp.int32, (C, C), 1)
    causal = (ci >= ri) if rev else (ci <= ri)

    n_chunks = blk // C
    for step in range(n_chunks):
        c = n_chunks - 1 - step if rev else step
        rs = slice(c * C, (c + 1) * C)
        last = c * C if rev else (c + 1) * C - 1
        for h in range(heads):
            ks = slice(h * dk, (h + 1) * dk)
            vs = slice(h * dv, (h + 1) * dv)
            b = bcum[rs, ks]
            b_last = bcum[last:last + 1, ks]
            q = q_ref[rs, ks].astype(F32) * (dk ** -0.5)
            k = k_ref[rs, ks].astype(F32)
            v = v_ref[rs, vs]
            q_dec = (q * jnp.exp(b)).astype(BF16)
            k_dec = (k * jnp.exp(-b)).astype(BF16)
            k_st = (k * jnp.exp(b_last - b)).astype(BF16)
            att = jnp.where(causal, _dot_nt(q_dec, k_dec), 0.0).astype(BF16)
            st = st_ref[h]
            o = _dot(att, v) + _dot_nt(q_dec, st.astype(BF16))
            st_ref[h] = st * jnp.exp(b_last) + _dot_tn(v, k_st)
            if final:
                tot = o + prev_ref[rs, vs]
                y = tot * lax.rsqrt(jnp.mean(tot * tot, axis=-1, keepdims=True) + EPS) * ng_ref[...]
                o_ref[rs, vs] = (y * _silu(gate_ref[rs, vs].astype(F32))).astype(o_ref.dtype)
            else:
                o_ref[rs, vs] = o


def _gla_dir(lay, U, small, cols, w_dec, b_dec, rev, norm_g=None, prev=None):
    B, NS, blk = lay.B, lay.NS, SEQ_BLOCK
    qk_w = cols["gla_q"][1]
    v_w = cols["gla_v"][1]
    dk, dv = qk_w // GLA_HEADS, v_w // GLA_HEADS
    final = prev is not None
    rowi = lambda b, n: lay.row_block(b, lay.seq_order(n, rev))
    ucol = lambda name: pl.BlockSpec((blk, cols[name][1]), lambda b, n: (rowi(b, n), cols[name][0] // cols[name][1]))
    const = lambda shape: pl.BlockSpec(shape, lambda b, n: (0,) * len(shape))
    in_specs = [ucol("gla_q"), ucol("gla_k"), ucol("gla_v"),
                pl.BlockSpec((blk, SMALL_COLS), lambda b, n: (rowi(b, n), 0)),
                const((SMALL_COLS, qk_w)), const((1, qk_w))]
    args = [U, U, U, small, w_dec, b_dec]
    if final:
        in_specs += [ucol("gla_gate"), const((1, dv)), pl.BlockSpec((blk, v_w), lambda b, n: (rowi(b, n), 0))]
        args += [U, norm_g, prev]
    return pl.pallas_call(
        functools.partial(_gla_kernel, rev=rev, final=final, dk=dk, dv=dv),
        out_shape=jax.ShapeDtypeStruct((lay.R, v_w), BF16 if final else F32),
        grid=(B, NS),
        in_specs=in_specs,
        out_specs=pl.BlockSpec((blk, v_w), lambda b, n: (rowi(b, n), 0)),
        scratch_shapes=[pltpu.VMEM((GLA_HEADS, dv, dk), F32)],
        compiler_params=_cparams(("arbitrary", "arbitrary")),
        name="gla_bwd" if rev else "gla_fwd",
    )(*args)


def _conv_kernel(x_ref, xp_ref, xn_ref, bc_ref, bcp_ref, bcn_ref, wx_ref, bx_ref, wbc_ref, bbc_ref,
                 xo_ref, bco_ref, ex_ref, ebc_ref, *, n_ctx_total, NC, NL):
    i = pl.program_id(0)
    in_ctx = i < n_ctx_total
    p = jnp.where(in_ctx, i % NC, jnp.maximum(i - n_ctx_total, 0) % NL)
    seg = jnp.where(in_ctx, NC, NL)
    has_left = (p > 0).astype(F32)
    has_right = (p < seg - 1).astype(F32)
    blk = x_ref.shape[0]
    pad = SSD_CONV // 2

    def run(m_ref, p_ref, n_ref, w_ref, b_ref, e_ref, o_ref):
        e_ref[0:HALO_ROWS] = p_ref[...].astype(F32) * has_left
        e_ref[HALO_ROWS:HALO_ROWS + blk] = m_ref[...].astype(F32)
        e_ref[HALO_ROWS + blk:] = n_ref[...].astype(F32) * has_right
        acc = b_ref[...]
        for j in range(SSD_CONV):
            acc = acc + w_ref[j:j + 1, :] * e_ref[HALO_ROWS - pad + j:HALO_ROWS - pad + j + blk, :]
        o_ref[...] = _silu(acc).astype(o_ref.dtype)

    run(x_ref, xp_ref, xn_ref, wx_ref, bx_ref, ex_ref, xo_ref)
    run(bc_ref, bcp_ref, bcn_ref, wbc_ref, bbc_ref, ebc_ref, bco_ref)


def _conv(lay, U, cols, wx, bx, wbc, bbc):
    blk = SEQ_BLOCK
    nb = lay.R // blk
    hb = blk // HALO_ROWS
    nh = lay.R // HALO_ROWS
    xw = cols["ssd_x"][1]
    bcw = cols["ssd_bc"][1]
    xc, bcc = cols["ssd_x"][0] // xw, cols["ssd_bc"][0] // bcw
    prev = lambda i: jnp.maximum(i * hb - 1, 0)
    nxt = lambda i: jnp.minimum((i + 1) * hb, nh - 1)
    const = lambda shape: pl.BlockSpec(shape, lambda i: (0,) * len(shape))
    return pl.pallas_call(
        functools.partial(_conv_kernel, n_ctx_total=lay.B * lay.NC, NC=lay.NC, NL=lay.NL),
        out_shape=(jax.ShapeDtypeStruct((lay.R, xw), BF16), jax.ShapeDtypeStruct((lay.R, bcw), BF16)),
        grid=(nb,),
        in_specs=[
            pl.BlockSpec((blk, xw), lambda i: (i, xc)),
            pl.BlockSpec((HALO_ROWS, xw), lambda i: (prev(i), xc)),
            pl.BlockSpec((HALO_ROWS, xw), lambda i: (nxt(i), xc)),
            pl.BlockSpec((blk, bcw), lambda i: (i, bcc)),
            pl.BlockSpec((HALO_ROWS, bcw), lambda i: (prev(i), bcc)),
            pl.BlockSpec((HALO_ROWS, bcw), lambda i: (nxt(i), bcc)),
            const((SSD_CONV, xw)), const((1, xw)), const((SSD_CONV, bcw)), const((1, bcw)),
        ],
        out_specs=(pl.BlockSpec((blk, xw), lambda i: (i, 0)), pl.BlockSpec((blk, bcw), lambda i: (i, 0))),
        scratch_shapes=[pltpu.VMEM((blk + 2 * HALO_ROWS, xw), F32),
                        pltpu.VMEM((blk + 2 * HALO_ROWS, bcw), F32)],
        compiler_params=_cparams(("arbitrary",)),
        name="ssd_conv",
    )(U, U, U, U, U, U, wx, bx, wbc, bbc)


def _ssd_kernel(*refs, rev, final):
    if final:
        (x_ref, bc_ref, sm_ref, dtb_ref, apad_ref, e_ref, z_ref, dsk_ref, ng_ref, prev_ref,
         o_ref, st_ref, y_ref) = refs
    else:
        x_ref, bc_ref, sm_ref, dtb_ref, apad_ref, e_ref, o_ref, st_ref = refs
        y_ref = o_ref
    blk, inner = x_ref.shape
    N = SSD_STATE
    P = SSD_HEAD_DIM
    C = SSD_CHUNK
    gw = inner // SSD_GROUPS
    pair = 2 * P

    @pl.when(pl.program_id(1) == 0)
    def _():
        st_ref[...] = jnp.zeros_like(st_ref)

    dt16 = _softplus(sm_ref[...] + dtb_ref[...])
    cs16 = _dot_exact_lhs(_block_tri(blk, C, rev), dt16 * apad_ref[...])
    e = e_ref[...]
    dt_full = _dot_exact_rhs(dt16, e)
    cs_full = _dot_exact_rhs(cs16, e)

    ri = lax.broadcasted_iota(jnp.int32, (C, pair), 0)
    ci = lax.broadcasted_iota(jnp.int32, (C, pair), 1)
    cm = jnp.where(ci >= P, ci - P, ci)
    eye2 = cm == ri
    causal2 = (cm >= ri) if rev else (cm <= ri)
    left = ci < P
    zero = jnp.zeros((C, pair), BF16)

    n_chunks = blk // C
    for step in range(n_chunks):
        c = n_chunks - 1 - step if rev else step
        rs = slice(c * C, (c + 1) * C)
        last = c * C if rev else (c + 1) * C - 1
        for g in range(SSD_GROUPS):
            gs = slice(g * gw, (g + 1) * gw)
            bm = bc_ref[rs, g * N:(g + 1) * N]
            cmat = bc_ref[rs, SSD_GROUPS * N + g * N:SSD_GROUPS * N + (g + 1) * N]
            cb2 = _dot_nt(cmat, jnp.concatenate([bm, bm], axis=0))
            cs_g = cs_full[rs, gs]
            cs_last = cs_full[last:last + 1, gs]
            xdt = x_ref[rs, gs].astype(F32) * dt_full[rs, gs]
            st = st_ref[g]
            y_inter = _dot(cmat, st.astype(BF16)) * jnp.exp(cs_g)
            w = (xdt * jnp.exp(cs_last - cs_g)).astype(BF16)
            st_ref[g] = st * jnp.exp(cs_last) + _dot_tn(bm, w)
            xdt_b = xdt.astype(BF16)
            for pr in range(gw // pair):
                ps = slice(pr * pair, (pr + 1) * pair)
                col = cs_g[:, ps]
                rowv = jnp.sum(jnp.where(eye2, col, 0.0), axis=0, keepdims=True)
                lmat = jnp.exp(jnp.where(causal2, col - rowv, -jnp.inf))
                mm = (lmat * cb2).astype(BF16)
                xp = xdt_b[:, ps]
                rhs = jnp.concatenate([jnp.where(left, xp, zero), jnp.where(left, zero, xp)], axis=0)
                y = _dot(mm, rhs) + y_inter[:, ps]
                os_ = slice(g * gw + pr * pair, g * gw + (pr + 1) * pair)
                if final:
                    y_ref[rs, os_] = y + prev_ref[rs, os_] + x_ref[rs, os_].astype(F32) * dsk_ref[:, os_]
                else:
                    y_ref[rs, os_] = y

    if final:
        rc = 64
        for r in range(blk // rc):
            rs = slice(r * rc, (r + 1) * rc)
            for g in range(SSD_GROUPS):
                gs = slice(g * gw, (g + 1) * gw)
                yg = y_ref[rs, gs] * _silu(z_ref[rs, gs].astype(F32))
                ms = jnp.mean(yg * yg, axis=-1, keepdims=True)
                o_ref[rs, gs] = (yg * lax.rsqrt(ms + EPS) * ng_ref[:, gs]).astype(o_ref.dtype)


def _ssd_dir(lay, xc, bcc, small, U, cols, dtb, apad, expand, rev, dsk=None, norm_g=None, prev=None):
    B, NS, blk = lay.B, lay.NS, SEQ_BLOCK
    inner = xc.shape[1]
    bcw = bcc.shape[1]
    final = prev is not None
    rowi = lambda b, n: lay.row_block(b, lay.seq_order(n, rev))
    rows = lambda w: pl.BlockSpec((blk, w), lambda b, n: (rowi(b, n), 0))
    const = lambda shape: pl.BlockSpec(shape, lambda b, n: (0,) * len(shape))
    in_specs = [rows(inner), rows(bcw), rows(SMALL_COLS), const((1, SMALL_COLS)), const((1, SMALL_COLS)),
                const((SMALL_COLS, inner))]
    args = [xc, bcc, small, dtb, apad, expand]
    if final:
        zc = cols["ssd_z"]
        in_specs += [pl.BlockSpec((blk, inner), lambda b, n: (rowi(b, n), zc[0] // zc[1])),
                     const((1, inner)), const((1, inner)), rows(inner)]
        args += [U, dsk, norm_g, prev]
    return pl.pallas_call(
        functools.partial(_ssd_kernel, rev=rev, final=final),
        out_shape=jax.ShapeDtypeStruct((lay.R, inner), BF16 if final else F32),
        grid=(B, NS),
        in_specs=in_specs,
        out_specs=rows(inner),
        scratch_shapes=[pltpu.VMEM((SSD_GROUPS, SSD_STATE, inner // SSD_GROUPS), F32)]
        + ([pltpu.VMEM((blk, inner), F32)] if final else []),
        compiler_params=_cparams(("arbitrary", "arbitrary")),
        name="ssd_bwd" if rev else "ssd_fwd",
    )(*args)


def _rope_tables(CTX, SEQ):
    hd = ATT_HEAD_DIM
    pairs = hd // 4
    rows = SEQ // GRID_W
    row = jnp.repeat(jnp.arange(rows, dtype=F32), GRID_W)
    col = jnp.tile(jnp.arange(GRID_W, dtype=F32), rows)
    inv = jnp.exp(-math.log(ROPE_THETA) * jnp.arange(pairs, dtype=F32) / pairs)
    ar = row[:, None] * inv
    ac = col[:, None] * inv
    ang = jnp.concatenate([ar, ar, ac, ac], axis=-1)
    cos, sin = jnp.cos(ang), jnp.sin(ang)
    first = (jnp.arange(hd) % (2 * pairs)) < pairs
    sa = jnp.where(first, -sin, 0.0)
    sb = jnp.where(first, 0.0, sin)
    ones = jnp.ones((CTX, hd), F32)
    zeros = jnp.zeros((CTX, hd), F32)
    return (jnp.concatenate([ones, cos]), jnp.concatenate([zeros, sa]), jnp.concatenate([zeros, sb]))


def _column_plan(D):
    mix = D // 2
    att_q = mix
    att_kv = ATT_KV_HEADS * ATT_HEAD_DIM
    gla_v = mix
    gla_qk = mix // 2
    ssd_inner = mix
    ssd_bc = 2 * SSD_GROUPS * SSD_STATE
    order = [("att_q", att_q), ("gla_v", gla_v), ("gla_gate", gla_v), ("ssd_z", ssd_inner),
             ("ssd_x", ssd_inner), ("gla_q", gla_qk), ("gla_k", gla_qk), ("att_k", att_kv),
             ("att_v", att_kv), ("ssd_bc", ssd_bc)]
    cols, off = {}, 0
    for name, w in order:
        assert off % w == 0
        cols[name] = (off, w)
        off += w
    return cols, off


def _source_columns(D):
    mix = D // 2
    att_kv = ATT_KV_HEADS * ATT_HEAD_DIM
    ssd_bc = SSD_GROUPS * SSD_STATE
    heads = mix // SSD_HEAD_DIM
    widths = [("att_q", mix), ("att_k", att_kv), ("att_v", att_kv), ("gla_q", mix // 2), ("gla_k", mix // 2),
              ("gla_v", mix), ("gla_gate", mix), ("gla_lr", 2 * GLA_LOWRANK), ("ssd_z", mix),
              ("ssd_x", mix), ("ssd_bc", 2 * ssd_bc), ("ssd_dt", 2 * heads), ("gates", N_BRANCH * D)]
    src, off = {}, 0
    for name, w in widths:
        src[name] = (off, w)
        off += w
    return src, off


def kernel(x, c, ctx, c_ctx, norm1, norm2, w_ada, b_ada, w_in, att_q_norm, att_k_norm, gla_w_decay,
           gla_b_decay, gla_norm, ssd_conv_w, ssd_conv_b, ssd_dt_bias, ssd_a_log, ssd_d, ssd_norm,
           w_branch, w_out, w_ff1, w_ff2, final_norm):
    B, SEQ, D = x.shape
    CTX = ctx.shape[1]
    depth = w_in.shape[0]
    lay = _Layout(B, CTX, SEQ, D)
    cols, n_main = _column_plan(D)
    src, n_src = _source_columns(D)
    assert n_src == w_in.shape[2]
    mix = D // 2
    ssd_heads = mix // SSD_HEAD_DIM
    lr_w = 2 * GLA_LOWRANK
    assert lr_w + 2 * ssd_heads <= SMALL_COLS

    X = jnp.concatenate([ctx.reshape(B * CTX, D), x.reshape(B * SEQ, D)], axis=0)
    cvec = jnp.concatenate([c, c_ctx[None, :], jnp.zeros((MOD_ROWS - B - 1, D), F32)], axis=0)
    mods = _ada(cvec, w_ada, b_ada).reshape(depth, MOD_ROWS, 1, N_MOD * D)
    rope = _rope_tables(CTX, SEQ)

    expand = np.zeros((2, SMALL_COLS, mix), np.float32)
    for d in range(2):
        for h in range(ssd_heads):
            expand[d, lr_w + d * ssd_heads + h, h * SSD_HEAD_DIM:(h + 1) * SSD_HEAD_DIM] = 1.0
    expand = jnp.asarray(expand, BF16)

    take = lambda name: w_in[:, :, src[name][0]:src[name][0] + src[name][1]]
    w_main = jnp.concatenate([take(n) for n, _ in sorted(cols.items(), key=lambda kv: kv[1][0])],
                             axis=2).astype(BF16)
    w_small = jnp.concatenate([take("gla_lr"), take("ssd_dt"),
                               jnp.zeros((depth, D, SMALL_COLS - lr_w - 2 * ssd_heads), F32)],
                              axis=2).astype(BF16)
    w_gate = take("gates").astype(BF16)
    w_branch_b, w_out_b = w_branch.astype(BF16), w_out.astype(BF16)
    w_ff1_b, w_ff2_b = w_ff1.astype(BF16), w_ff2.astype(BF16)

    out = None
    for l in range(depth):
        mod = mods[l]
        g1 = norm1[l][None, :]

        U, small = _inproj(lay, X, mod, g1, w_main, w_small, l)

        qt, kk, vt = _qkprep(lay, U, cols, att_q_norm[l][None, :], att_k_norm[l][None, :], rope)
        o_att = _attention(lay, qt, kk, vt)

        gla_out = None
        for d in range(2):
            wd = jnp.zeros((SMALL_COLS, mix // 2), F32).at[d * GLA_LOWRANK:(d + 1) * GLA_LOWRANK].set(
                gla_w_decay[l, d]).astype(BF16)
            bd = gla_b_decay[l, d][None, :]
            if d == 0:
                gla_out = _gla_dir(lay, U, small, cols, wd, bd, rev=False)
            else:
                gla_out = _gla_dir(lay, U, small, cols, wd, bd, rev=True,
                                   norm_g=gla_norm[l][None, :], prev=gla_out)
        o_gla = gla_out

        cw, cbias = ssd_conv_w[l], ssd_conv_b[l]
        xc, bcc = _conv(lay, U, cols, cw[:, :mix], cbias[None, :mix], cw[:, mix:], cbias[None, mix:])
        ssd_out = None
        for d in range(2):
            lanes = slice(lr_w + d * ssd_heads, lr_w + (d + 1) * ssd_heads)
            dtb = jnp.zeros((1, SMALL_COLS), F32).at[0, lanes].set(ssd_dt_bias[l, d])
            apad = jnp.zeros((1, SMALL_COLS), F32).at[0, lanes].set(-jnp.exp(ssd_a_log[l, d]))
            if d == 0:
                ssd_out = _ssd_dir(lay, xc, bcc, small, U, cols, dtb, apad, expand[d], rev=False)
            else:
                ssd_out = _ssd_dir(lay, xc, bcc, small, U, cols, dtb, apad, expand[d], rev=True,
                                   dsk=jnp.repeat(ssd_d[l], SSD_HEAD_DIM)[None, :],
                                   norm_g=ssd_norm[l][None, :], prev=ssd_out)
        o_ssd = ssd_out

        last = l == depth - 1
        skip = B * CTX if last else 0
        m = _merge(lay, X, mod, g1, o_att, o_gla, o_ssd, w_gate, w_branch_b, l, skip)
        mlp_args = (lay, X, m, w_out_b, mod, norm2[l][None, :], w_ff1_b, w_ff2_b, l, skip)
        if last:
            out = _mlp(*mlp_args, final_gain=final_norm[None, :])
        else:
            X = _mlp(*mlp_args)
    return out.reshape(B, SEQ, D)
```

```python
import functools
import math

import jax
import jax.numpy as jnp
import numpy as np
from jax import lax
from jax.experimental import pallas as pl
from jax.experimental.pallas import tpu as pltpu

F32 = jnp.float32
BF16 = jnp.bfloat16

EPS = 1e-6
GRID_W = 64
N_BRANCH = 3
N_MOD = 6
ATT_HEAD_DIM = 128
ATT_KV_HEADS = 2
ROPE_THETA = 10000.0
GLA_HEADS = 4
GLA_LOWRANK = 16
GLA_TAU = 16.0
GLA_CHUNK = 64
SSD_HEAD_DIM = 64
SSD_GROUPS = 2
SSD_STATE = 128
SSD_CONV = 5
SSD_CHUNK = 64

SEQ_BLOCK = 256
RESIDENT_TM = 256
MOD_ROWS = 8
SMALL_COLS = 128
HALO_ROWS = 16
ATT_ONES_ROWS = 16
VMEM_LIMIT = 56 * 1024 * 1024


def _cparams(sem):
    return pltpu.CompilerParams(dimension_semantics=sem, vmem_limit_bytes=VMEM_LIMIT)


def _dot(a, b):
    return jnp.dot(a, b, preferred_element_type=F32)


def _dot_nt(a, b):
    return lax.dot_general(a, b, (((1,), (1,)), ((), ())), preferred_element_type=F32)


def _dot_tn(a, b):
    return lax.dot_general(a, b, (((0,), (0,)), ((), ())), preferred_element_type=F32)


def _split3(x):
    hi = x.astype(BF16)
    r = x - hi.astype(F32)
    mid = r.astype(BF16)
    lo = (r - mid.astype(F32)).astype(BF16)
    return hi, mid, lo


def _dot_exact_lhs(a, x):
    hi, mid, lo = _split3(x)
    return _dot(a, lo) + _dot(a, mid) + _dot(a, hi)


def _dot_spread(x, e):
    hi = x.astype(BF16)
    lo = (x - hi.astype(F32)).astype(BF16)
    return _dot(lo, e) + _dot(hi, e)


def _silu(x):
    return x * jax.nn.sigmoid(x)


def _softplus(x):
    return jnp.maximum(x, 0.0) + jnp.log1p(jnp.exp(-jnp.abs(x)))


def _block_tri(n, chunk, rev):
    r = lax.broadcasted_iota(jnp.int32, (n, n), 0)
    c = lax.broadcasted_iota(jnp.int32, (n, n), 1)
    shift = chunk.bit_length() - 1
    assert chunk == 1 << shift
    same = jnp.right_shift(r, shift) == jnp.right_shift(c, shift)
    tri = (c >= r) if rev else (c <= r)
    return jnp.where(same, jnp.where(tri, 1.0, 0.0), 0.0).astype(BF16)


class _Layout:
    def __init__(self, B, CTX, SEQ, D):
        assert CTX % SEQ_BLOCK == 0 and SEQ % SEQ_BLOCK == 0
        self.B, self.CTX, self.SEQ, self.D = B, CTX, SEQ, D
        self.R = B * (CTX + SEQ)
        self.NC = CTX // SEQ_BLOCK
        self.NL = SEQ // SEQ_BLOCK
        self.NS = self.NC + self.NL
        self.tm = next(t for t in (512, 256) if (B * CTX) % t == 0 and SEQ % t == 0)
        assert B + 1 <= MOD_ROWS

    def row_block(self, b, s):
        return jnp.where(s < self.NC, b * self.NC + s, self.B * self.NC + b * self.NL + (s - self.NC))

    def batch_seq(self, i):
        nctx = self.B * self.NC
        j = jnp.maximum(i - nctx, 0)
        return (jnp.where(i < nctx, i // self.NC, j // self.NL),
                jnp.where(i < nctx, i % self.NC, self.NC + j % self.NL))

    def seq_order(self, n, rev):
        if not rev:
            return n
        return jnp.where(n < self.NC, self.NC - 1 - n, self.NC + self.NL - 1 - (n - self.NC))

    def mod_row(self, i, tm):
        nctx = (self.B * self.CTX) // tm
        return jnp.where(i < nctx, self.B, (i - nctx) // (self.SEQ // tm))


def _ada_kernel(c_ref, w_ref, b_ref, o_ref):
    s = _silu(c_ref[...]).astype(BF16)
    o_ref[0] = _dot(s, w_ref[0].astype(BF16)) + b_ref[0]


def _ada(cvec, w_ada, b_ada):
    depth, D, n = w_ada.shape
    tn = 1024
    return pl.pallas_call(
        _ada_kernel,
        out_shape=jax.ShapeDtypeStruct((depth, MOD_ROWS, n), F32),
        grid=(depth, n // tn),
        in_specs=[
            pl.BlockSpec((MOD_ROWS, D), lambda l, j: (0, 0)),
            pl.BlockSpec((1, D, tn), lambda l, j: (l, 0, j)),
            pl.BlockSpec((1, 1, tn), lambda l, j: (l, 0, j)),
        ],
        out_specs=pl.BlockSpec((1, MOD_ROWS, tn), lambda l, j: (l, 0, j)),
        compiler_params=_cparams(("arbitrary", "arbitrary")),
        name="ada",
    )(cvec, w_ada, b_ada.reshape(depth, 1, n))


def _norm_mod(x_ref, g_ref, mod_ref, k, h_ref):
    D = x_ref.shape[-1]
    shift = mod_ref[:, k * D:(k + 1) * D]
    a = g_ref[...] * (1.0 + mod_ref[:, (k + 1) * D:(k + 2) * D])
    rc = 64

    def body(r, carry):
        sl = pl.ds(pl.multiple_of(r * rc, rc), rc)
        xf = x_ref[sl, :]
        ms = jnp.mean(xf * xf, axis=-1, keepdims=True)
        h_ref[sl, :] = (xf * lax.rsqrt(ms + EPS) * a + shift).astype(h_ref.dtype)
        return carry

    lax.fori_loop(0, x_ref.shape[0] // rc, body, 0)


def _resident(shape, layer=None):
    if layer is None:
        return pl.BlockSpec(shape, lambda *_: (0,) * len(shape), pipeline_mode=pl.Buffered(1))
    return pl.BlockSpec((None,) + tuple(shape), lambda *_: (layer,) + (0,) * len(shape),
                        pipeline_mode=pl.Buffered(1))


def _inproj_kernel(x_ref, mod_ref, g_ref, w_ref, ws_ref, u_ref, small_ref, h_ref, *, tn):
    _norm_mod(x_ref, g_ref, mod_ref, 0, h_ref)
    h = h_ref[...]
    small_ref[...] = _dot(h, ws_ref[...])
    for j in range(w_ref.shape[1] // tn):
        sl = slice(j * tn, (j + 1) * tn)
        u_ref[:, sl] = _dot(h, w_ref[:, sl]).astype(u_ref.dtype)


def _inproj(lay, X, mod, g, w_main, w_small, l):
    _, D, n = w_main.shape
    tm = RESIDENT_TM
    return pl.pallas_call(
        functools.partial(_inproj_kernel, tn=512),
        out_shape=(jax.ShapeDtypeStruct((lay.R, n), BF16),
                   jax.ShapeDtypeStruct((lay.R, SMALL_COLS), F32)),
        grid=(lay.R // tm,),
        in_specs=[
            pl.BlockSpec((tm, D), lambda i: (i, 0)),
            pl.BlockSpec((None, 1, N_MOD * D), lambda i: (lay.mod_row(i, tm), 0, 0)),
            _resident((1, D)),
            _resident((D, n), l),
            _resident((D, SMALL_COLS), l),
        ],
        out_specs=(pl.BlockSpec((tm, n), lambda i: (i, 0)),
                   pl.BlockSpec((tm, SMALL_COLS), lambda i: (i, 0))),
        scratch_shapes=[pltpu.VMEM((tm, D), BF16)],
        compiler_params=_cparams(("arbitrary",)),
        name="in_proj",
    )(X, mod, g, w_main, w_small)


def _merge_kernel(x_ref, mod_ref, g_ref, oa_ref, og_ref, os_ref, wg_ref, wb_ref, m_ref, h_ref, *, tn):
    _norm_mod(x_ref, g_ref, mod_ref, 0, h_ref)
    h = h_ref[...]
    D = m_ref.shape[1]
    for j in range(D // tn):
        acc = None
        for b, o_ref in enumerate((oa_ref, og_ref, os_ref)):
            gate = _dot(h, wg_ref[:, b * D + j * tn:b * D + (j + 1) * tn])
            t = jax.nn.sigmoid(gate) * _dot(o_ref[...], wb_ref[b, :, j * tn:(j + 1) * tn])
            acc = t if acc is None else acc + t
        m_ref[:, j * tn:(j + 1) * tn] = acc.astype(m_ref.dtype)


def _merge(lay, X, mod, g, o_att, o_gla, o_ssd, w_gate, w_branch, l, skip_rows):
    D = lay.D
    W = o_att.shape[1]
    tm = RESIDENT_TM
    assert tm == SEQ_BLOCK
    row0 = skip_rows // tm
    rows = lambda i: (i + row0, 0)
    seq_rows = pl.BlockSpec((None, tm, W), lambda i: (*lay.batch_seq(i + row0), 0))
    return pl.pallas_call(
        functools.partial(_merge_kernel, tn=512),
        out_shape=jax.ShapeDtypeStruct((lay.R, D), BF16),
        grid=(lay.R // tm - row0,),
        in_specs=[
            pl.BlockSpec((tm, D), rows),
            pl.BlockSpec((None, 1, N_MOD * D), lambda i: (lay.mod_row(i + row0, tm), 0, 0)),
            _resident((1, D)),
            pl.BlockSpec((tm, W), rows), seq_rows, seq_rows,
            _resident(w_gate.shape[1:], l), _resident(w_branch.shape[1:], l),
        ],
        out_specs=pl.BlockSpec((tm, D), rows),
        scratch_shapes=[pltpu.VMEM((tm, D), BF16)],
        compiler_params=_cparams(("arbitrary",)),
        name="merge",
    )(X, mod, g, o_att, o_gla, o_ssd, w_gate, w_branch)


def _mlp_kernel(x_ref, m_ref, wo_ref, mod_ref, g_ref, w1_ref, w2_ref, *rest, final):
    if final:
        fg_ref, o_ref, x1_ref, h_ref = rest
    else:
        o_ref, x1_ref, h_ref = rest
    k = pl.program_id(1)
    D = x_ref.shape[-1]

    @pl.when(k == 0)
    def _():
        tn = 512
        for j in range(D // tn):
            sl = slice(j * tn, (j + 1) * tn)
            x1_ref[:, sl] = x_ref[:, sl] + mod_ref[:, 2 * D + j * tn:2 * D + (j + 1) * tn] * _dot(
                m_ref[...], wo_ref[:, sl])
        _norm_mod(x1_ref, g_ref, mod_ref, 3, h_ref)
        o_ref[...] = jnp.zeros_like(o_ref)

    f = jnp.square(jnp.maximum(_dot(h_ref[...], w1_ref[...]), 0.0)).astype(BF16)
    o_ref[...] += _dot(f, w2_ref[...])

    @pl.when(k == pl.num_programs(1) - 1)
    def _():
        gate = mod_ref[:, 5 * D:6 * D]
        rc = 64

        def body(r, carry):
            sl = pl.ds(pl.multiple_of(r * rc, rc), rc)
            y = x1_ref[sl, :] + gate * o_ref[sl, :]
            if final:
                ms = jnp.mean(y * y, axis=-1, keepdims=True)
                y = y * lax.rsqrt(ms + EPS) * fg_ref[...]
            o_ref[sl, :] = y
            return carry

        lax.fori_loop(0, x_ref.shape[0] // rc, body, 0)


def _mlp(lay, X, m, w_out, mod, g, w1, w2, l, skip_rows, final_gain=None):
    _, D, FF = w1.shape
    tm, tf = lay.tm, 1024
    row0 = skip_rows // tm
    final = final_gain is not None
    out_rows = lay.R - skip_rows if final else lay.R
    out_off = 0 if final else row0
    rows = lambda i, k: (i + row0, 0)
    in_specs = [
        pl.BlockSpec((tm, D), rows),
        pl.BlockSpec((tm, D), rows),
        _resident((D, D), l),
        pl.BlockSpec((None, 1, N_MOD * D), lambda i, k: (lay.mod_row(i + row0, tm), 0, 0)),
        _resident((1, D)),
        pl.BlockSpec((None, D, tf), lambda i, k: (l, 0, k)),
        pl.BlockSpec((None, tf, D), lambda i, k: (l, k, 0)),
    ]
    args = [X, m, w_out, mod, g, w1, w2]
    if final:
        in_specs.append(_resident((1, D)))
        args.append(final_gain)
    return pl.pallas_call(
        functools.partial(_mlp_kernel, final=final),
        out_shape=jax.ShapeDtypeStruct((out_rows, D), F32),
        grid=(lay.R // tm - row0, FF // tf),
        in_specs=in_specs,
        out_specs=pl.BlockSpec((tm, D), lambda i, k: (i + out_off, 0)),
        scratch_shapes=[pltpu.VMEM((tm, D), F32), pltpu.VMEM((tm, D), BF16)],
        compiler_params=_cparams(("arbitrary", "arbitrary")),
        name="mlp",
    )(*args)


def _qkprep_kernel(q_ref, k_ref, v_ref, qg_ref, kg_ref, cos_ref, sa_ref, sb_ref,
                   qt_ref, ko_ref, vt_ref):
    cos, sa, sb = cos_ref[...], sa_ref[...], sb_ref[...]
    hd = ATT_HEAD_DIM

    def prep(x, g):
        xf = x.astype(F32)
        y = xf * lax.rsqrt(jnp.mean(xf * xf, axis=-1, keepdims=True) + EPS) * g
        return y * cos + pltpu.roll(y, hd - hd // 4, 1) * sa + pltpu.roll(y, hd // 4, 1) * sb

    q_scale = (hd ** -0.5) * math.log2(math.e)
    for h in range(q_ref.shape[1] // hd):
        sl = slice(h * hd, (h + 1) * hd)
        qt_ref[h] = (prep(q_ref[:, sl], qg_ref[...]) * q_scale).T.astype(qt_ref.dtype)
    for g in range(k_ref.shape[1] // hd):
        sl = slice(g * hd, (g + 1) * hd)
        ko_ref[:, sl] = prep(k_ref[:, sl], kg_ref[...]).astype(ko_ref.dtype)
        vt_ref[g, 0:hd, :] = v_ref[:, sl].astype(F32).T.astype(vt_ref.dtype)
        vt_ref[g, hd:, :] = jnp.ones((ATT_ONES_ROWS, v_ref.shape[0]), vt_ref.dtype)


def _qkprep(lay, U, cols, q_gain, k_gain, rope):
    B, NS, blk = lay.B, lay.NS, SEQ_BLOCK
    att_q = cols["att_q"][1]
    att_kv = cols["att_k"][1]
    hd = ATT_HEAD_DIM
    kvh = att_kv // hd
    heads = att_q // hd
    row = lambda b, n: lay.row_block(b, n)
    tab = pl.BlockSpec((blk, hd), lambda b, n: (n, 0))
    return pl.pallas_call(
        _qkprep_kernel,
        out_shape=(jax.ShapeDtypeStruct((lay.R // blk, heads, hd, blk), BF16),
                   jax.ShapeDtypeStruct((B, NS * blk, att_kv), BF16),
                   jax.ShapeDtypeStruct((B, kvh, NS, hd + ATT_ONES_ROWS, blk), BF16)),
        grid=(B, NS),
        in_specs=[
            pl.BlockSpec((blk, att_q), lambda b, n: (row(b, n), cols["att_q"][0] // att_q)),
            pl.BlockSpec((blk, att_kv), lambda b, n: (row(b, n), cols["att_k"][0] // att_kv)),
            pl.BlockSpec((blk, att_kv), lambda b, n: (row(b, n), cols["att_v"][0] // att_kv)),
            pl.BlockSpec((1, hd), lambda b, n: (0, 0)),
            pl.BlockSpec((1, hd), lambda b, n: (0, 0)),
            tab, tab, tab,
        ],
        out_specs=(pl.BlockSpec((None, heads, hd, blk), lambda b, n: (row(b, n), 0, 0, 0)),
                   pl.BlockSpec((None, blk, att_kv), lambda b, n: (b, n, 0)),
                   pl.BlockSpec((None, kvh, None, hd + ATT_ONES_ROWS, blk), lambda b, n: (b, 0, n, 0, 0))),
        compiler_params=_cparams(("arbitrary", "arbitrary")),
        name="qk_prep",
    )(U, U, U, q_gain, k_gain, *rope)


def _attn_kernel(qt_ref, k_ref, vt_ref, o_ref, m_ref, acc_ref, s0_ref, s1_ref, *, n_ctx_blocks):
    n = pl.program_id(2)
    heads, hd, tq = qt_ref.shape
    nblk, vrows, tk = vt_ref.shape
    n_kv = jnp.where(n < n_ctx_blocks, n_ctx_blocks, nblk)
    m_ref[...] = jnp.full(m_ref.shape, -1e30, F32)
    acc_ref[...] = jnp.zeros(acc_ref.shape, F32)

    def scores(c, s_ref):
        c = jnp.minimum(c, n_kv - 1)
        k = k_ref[pl.ds(pl.multiple_of(c * tk, tk), tk), :]
        for h in range(heads):
            s_ref[h] = _dot(k, qt_ref[h])

    def update(c, s_ref):
        vt = vt_ref[c]
        for h in range(heads):
            st = s_ref[h]
            m_old = m_ref[h]
            m_new = jnp.maximum(m_old, jnp.max(st, axis=0, keepdims=True))
            pt = jnp.exp2(st - m_new).astype(BF16)
            acc_ref[h] = jnp.exp2(m_old - m_new) * acc_ref[h] + _dot(vt, pt)
            m_ref[h] = m_new

    scores(0, s0_ref)

    def body(i, carry):
        scores(2 * i + 1, s1_ref)
        update(2 * i, s0_ref)
        scores(2 * i + 2, s0_ref)
        update(2 * i + 1, s1_ref)
        return carry

    lax.fori_loop(0, n_kv // 2, body, 0)

    @pl.when(n_kv % 2 == 1)
    def _():
        update(n_kv - 1, s0_ref)

    for h in range(heads):
        acc = acc_ref[h]
        ot = acc[0:hd] / acc[hd:hd + 1]
        o_ref[:, h * hd:(h + 1) * hd] = ot.T.astype(o_ref.dtype)


def _attention(lay, qt, kk, vt):
    B, kvh, NS, vrows, blk = vt.shape
    _, heads, hd, _ = qt.shape
    group = heads // kvh
    return pl.pallas_call(
        functools.partial(_attn_kernel, n_ctx_blocks=lay.NC),
        out_shape=jax.ShapeDtypeStruct((lay.R, heads * hd), BF16),
        grid=(B, kvh, NS),
        in_specs=[
            pl.BlockSpec((None, group, hd, blk), lambda b, g, n: (lay.row_block(b, n), g, 0, 0)),
            pl.BlockSpec((None, NS * blk, hd), lambda b, g, n: (b, 0, g)),
            pl.BlockSpec((None, None, NS, vrows, blk), lambda b, g, n: (b, g, 0, 0, 0)),
        ],
        out_specs=pl.BlockSpec((blk, group * hd), lambda b, g, n: (lay.row_block(b, n), g)),
        scratch_shapes=[pltpu.VMEM((group, 1, blk), F32), pltpu.VMEM((group, vrows, blk), F32),
                        pltpu.VMEM((group, blk, blk), F32), pltpu.VMEM((group, blk, blk), F32)],
        compiler_params=_cparams(("arbitrary", "arbitrary", "arbitrary")),
        name="attention",
    )(qt, kk, vt)


def _gla_kernel(*refs, nb, rev, final, dk, dv):
    per_b = 5 if final else 4
    seqs = [refs[i * per_b:(i + 1) * per_b] for i in range(nb)]
    rest = refs[nb * per_b:]
    if final:
        wd_ref, bd_ref, ng_ref, prev_ref, o_ref, st_ref = rest
    else:
        wd_ref, bd_ref, o_ref, st_ref = rest
    blk = seqs[0][0].shape[0]
    heads = seqs[0][0].shape[1] // dk
    C = GLA_CHUNK

    @pl.when(pl.program_id(0) == 0)
    def _():
        st_ref[...] = jnp.zeros_like(st_ref)

    tri = _block_tri(blk, C, rev)
    bcums = []
    for bi in range(nb):
        z = _dot(seqs[bi][3][...].astype(BF16), wd_ref[...]) + bd_ref[...]
        la = (jnp.minimum(z, 0.0) - jnp.log1p(jnp.exp(-jnp.abs(z)))) * (1.0 / GLA_TAU)
        bcums.append(_dot_exact_lhs(tri, la))

    ri = lax.broadcasted_iota(jnp.int32, (C, C), 0)
    ci = lax.broadcasted_iota(jnp.int32, (C, C), 1)
    causal = (ci >= ri) if rev else (ci <= ri)

    n_chunks = blk // C
    for step in range(n_chunks):
        c = n_chunks - 1 - step if rev else step
        rs = slice(c * C, (c + 1) * C)
        last = c * C if rev else (c + 1) * C - 1
        units = [(h, bi) for h in range(heads) for bi in range(nb)]
        scores, inter = [], []
        for h, bi in units:
            ks = slice(h * dk, (h + 1) * dk)
            q_ref, k_ref, v_ref = seqs[bi][:3]
            b = bcums[bi][rs, ks]
            b_last = bcums[bi][last:last + 1, ks]
            q = q_ref[rs, ks].astype(F32) * (dk ** -0.5)
            k = k_ref[rs, ks].astype(F32)
            q_dec = (q * jnp.exp(b)).astype(BF16)
            k_dec = (k * jnp.exp(-b)).astype(BF16)
            k_st = (k * jnp.exp(b_last - b)).astype(BF16)
            st = st_ref[bi * heads + h]
            scores.append(_dot_nt(q_dec, k_dec))
            inter.append(_dot_nt(q_dec, st.astype(BF16)))
            st_ref[bi * heads + h] = st * jnp.exp(b_last) + _dot_tn(v_ref[rs, h * dv:(h + 1) * dv], k_st)
        for (h, bi), sc, it in zip(units, scores, inter):
            vs = slice(h * dv, (h + 1) * dv)
            att = jnp.where(causal, sc, 0.0).astype(BF16)
            o = _dot(att, seqs[bi][2][rs, vs]) + it
            if final:
                tot = o + prev_ref[bi, rs, vs]
                y = tot * lax.rsqrt(jnp.mean(tot * tot, axis=-1, keepdims=True) + EPS) * ng_ref[...]
                o_ref[bi, rs, vs] = (y * _silu(seqs[bi][4][rs, vs].astype(F32))).astype(o_ref.dtype)
            else:
                o_ref[bi, rs, vs] = o


def _gla_dir(lay, U, small, cols, w_dec, b_dec, rev, norm_g=None, prev=None):
    B, NS, blk = lay.B, lay.NS, SEQ_BLOCK
    qk_w = cols["gla_q"][1]
    v_w = cols["gla_v"][1]
    dk, dv = qk_w // GLA_HEADS, v_w // GLA_HEADS
    final = prev is not None
    const = lambda shape: pl.BlockSpec(shape, lambda n: (0,) * len(shape))
    seq = lambda n: lay.seq_order(n, rev)

    def ucol(name, b):
        off, w = cols[name]
        return pl.BlockSpec((blk, w), lambda n: (lay.row_block(b, seq(n)), off // w))

    in_specs, args = [], []
    for b in range(B):
        in_specs += [ucol("gla_q", b), ucol("gla_k", b), ucol("gla_v", b),
                     pl.BlockSpec((blk, SMALL_COLS), lambda n, b=b: (lay.row_block(b, seq(n)), 0))]
        args += [U, U, U, small]
        if final:
            in_specs.append(ucol("gla_gate", b))
            args.append(U)
    in_specs += [const((SMALL_COLS, qk_w)), const((1, qk_w))]
    args += [w_dec, b_dec]
    per_batch = pl.BlockSpec((B, blk, v_w), lambda n: (0, seq(n), 0))
    if final:
        in_specs += [const((1, dv)), per_batch]
        args += [norm_g, prev]
    return pl.pallas_call(
        functools.partial(_gla_kernel, nb=B, rev=rev, final=final, dk=dk, dv=dv),
        out_shape=jax.ShapeDtypeStruct((B, NS * blk, v_w), BF16 if final else F32),
        grid=(NS,),
        in_specs=in_specs,
        out_specs=per_batch,
        scratch_shapes=[pltpu.VMEM((B * GLA_HEADS, dv, dk), F32)],
        compiler_params=_cparams(("arbitrary",)),
        name="gla_bwd" if rev else "gla_fwd",
    )(*args)


def _conv_kernel(x_ref, xp_ref, xn_ref, bc_ref, bcp_ref, bcn_ref, wx_ref, bx_ref, wbc_ref, bbc_ref,
                 xo_ref, bco_ref, ex_ref, ebc_ref, *, n_ctx_total, NC, NL):
    i = pl.program_id(0)
    in_ctx = i < n_ctx_total
    p = jnp.where(in_ctx, i % NC, jnp.maximum(i - n_ctx_total, 0) % NL)
    seg = jnp.where(in_ctx, NC, NL)
    has_left = (p > 0).astype(F32)
    has_right = (p < seg - 1).astype(F32)
    blk = x_ref.shape[0]
    pad = SSD_CONV // 2

    def run(m_ref, p_ref, n_ref, w_ref, b_ref, e_ref, o_ref):
        e_ref[0:HALO_ROWS] = p_ref[...].astype(F32) * has_left
        e_ref[HALO_ROWS:HALO_ROWS + blk] = m_ref[...].astype(F32)
        e_ref[HALO_ROWS + blk:] = n_ref[...].astype(F32) * has_right
        acc = b_ref[...]
        for j in range(SSD_CONV):
            acc = acc + w_ref[j:j + 1, :] * e_ref[HALO_ROWS - pad + j:HALO_ROWS - pad + j + blk, :]
        o_ref[...] = _silu(acc).astype(o_ref.dtype)

    run(x_ref, xp_ref, xn_ref, wx_ref, bx_ref, ex_ref, xo_ref)
    run(bc_ref, bcp_ref, bcn_ref, wbc_ref, bbc_ref, ebc_ref, bco_ref)


def _conv(lay, U, cols, wx, bx, wbc, bbc):
    blk = SEQ_BLOCK
    nb = lay.R // blk
    hb = blk // HALO_ROWS
    nh = lay.R // HALO_ROWS
    xw = cols["ssd_x"][1]
    bcw = cols["ssd_bc"][1]
    xc, bcc = cols["ssd_x"][0] // xw, cols["ssd_bc"][0] // bcw
    prev = lambda i: jnp.maximum(i * hb - 1, 0)
    nxt = lambda i: jnp.minimum((i + 1) * hb, nh - 1)
    const = lambda shape: pl.BlockSpec(shape, lambda i: (0,) * len(shape))
    return pl.pallas_call(
        functools.partial(_conv_kernel, n_ctx_total=lay.B * lay.NC, NC=lay.NC, NL=lay.NL),
        out_shape=(jax.ShapeDtypeStruct((lay.R, xw), BF16), jax.ShapeDtypeStruct((lay.R, bcw), BF16)),
        grid=(nb,),
        in_specs=[
            pl.BlockSpec((blk, xw), lambda i: (i, xc)),
            pl.BlockSpec((HALO_ROWS, xw), lambda i: (prev(i), xc)),
            pl.BlockSpec((HALO_ROWS, xw), lambda i: (nxt(i), xc)),
            pl.BlockSpec((blk, bcw), lambda i: (i, bcc)),
            pl.BlockSpec((HALO_ROWS, bcw), lambda i: (prev(i), bcc)),
            pl.BlockSpec((HALO_ROWS, bcw), lambda i: (nxt(i), bcc)),
            const((SSD_CONV, xw)), const((1, xw)), const((SSD_CONV, bcw)), const((1, bcw)),
        ],
        out_specs=(pl.BlockSpec((blk, xw), lambda i: (i, 0)), pl.BlockSpec((blk, bcw), lambda i: (i, 0))),
        scratch_shapes=[pltpu.VMEM((blk + 2 * HALO_ROWS, xw), F32),
                        pltpu.VMEM((blk + 2 * HALO_ROWS, bcw), F32)],
        compiler_params=_cparams(("arbitrary",)),
        name="ssd_conv",
    )(U, U, U, U, U, U, wx, bx, wbc, bbc)


def _ssd_kernel(*refs, nb, rev, final):
    per_b = 4 if final else 3
    seqs = [refs[i * per_b:(i + 1) * per_b] for i in range(nb)]
    rest = refs[nb * per_b:]
    if final:
        dtb_ref, apad_ref, e_ref, dsk_ref, ng_ref, prev_ref, o_ref, st_ref, y_ref = rest
    else:
        dtb_ref, apad_ref, e_ref, o_ref, st_ref = rest
        y_ref = o_ref
    blk, inner = seqs[0][0].shape
    N = SSD_STATE
    P = SSD_HEAD_DIM
    C = SSD_CHUNK
    gw = inner // SSD_GROUPS
    pair = 2 * P

    @pl.when(pl.program_id(0) == 0)
    def _():
        st_ref[...] = jnp.zeros_like(st_ref)

    tri = _block_tri(blk, C, rev)
    e = e_ref[...]
    dt_fulls, cs_fulls = [], []
    for bi in range(nb):
        dt16 = _softplus(seqs[bi][2][...] + dtb_ref[...])
        cs16 = _dot_exact_lhs(tri, dt16 * apad_ref[...])
        dt_fulls.append(_dot_spread(dt16, e))
        cs_fulls.append(_dot_spread(cs16, e))

    ri = lax.broadcasted_iota(jnp.int32, (C, pair), 0)
    ci = lax.broadcasted_iota(jnp.int32, (C, pair), 1)
    cm = jnp.where(ci >= P, ci - P, ci)
    eye2 = cm == ri
    causal2 = (cm >= ri) if rev else (cm <= ri)
    left = ci < P
    zero = jnp.zeros((C, pair), BF16)

    n_chunks = blk // C
    for step in range(n_chunks):
        c = n_chunks - 1 - step if rev else step
        rs = slice(c * C, (c + 1) * C)
        last = c * C if rev else (c + 1) * C - 1
        units = [(g, bi) for g in range(SSD_GROUPS) for bi in range(nb)]
        held = []
        for g, bi in units:
            gs = slice(g * gw, (g + 1) * gw)
            x_ref, bc_ref = seqs[bi][:2]
            bm = bc_ref[rs, g * N:(g + 1) * N]
            cmat = bc_ref[rs, SSD_GROUPS * N + g * N:SSD_GROUPS * N + (g + 1) * N]
            cb2 = _dot_nt(cmat, jnp.concatenate([bm, bm], axis=0))
            cs_g = cs_fulls[bi][rs, gs]
            cs_last = cs_fulls[bi][last:last + 1, gs]
            xdt = x_ref[rs, gs].astype(F32) * dt_fulls[bi][rs, gs]
            st = st_ref[bi * SSD_GROUPS + g]
            y_inter = _dot(cmat, st.astype(BF16)) * jnp.exp(cs_g)
            w = (xdt * jnp.exp(cs_last - cs_g)).astype(BF16)
            st_ref[bi * SSD_GROUPS + g] = st * jnp.exp(cs_last) + _dot_tn(bm, w)
            held.append((cb2, cs_g, y_inter, xdt.astype(BF16)))
        for (g, bi), (cb2, cs_g, y_inter, xdt_b) in zip(units, held):
            x_ref = seqs[bi][0]
            for pr in range(gw // pair):
                ps = slice(pr * pair, (pr + 1) * pair)
                col = cs_g[:, ps]
                rowv = jnp.sum(jnp.where(eye2, col, 0.0), axis=0, keepdims=True)
                lmat = jnp.exp(jnp.where(causal2, col - rowv, -jnp.inf))
                mm = (lmat * cb2).astype(BF16)
                xp = xdt_b[:, ps]
                rhs = jnp.concatenate([jnp.where(left, xp, zero), jnp.where(left, zero, xp)], axis=0)
                y = _dot(mm, rhs) + y_inter[:, ps]
                os_ = slice(g * gw + pr * pair, g * gw + (pr + 1) * pair)
                if final:
                    y_ref[bi, rs, os_] = (y + prev_ref[bi, rs, os_]
                                          + x_ref[rs, os_].astype(F32) * dsk_ref[:, os_])
                else:
                    y_ref[bi, rs, os_] = y

    if final:
        rc = 64
        for bi in range(nb):
            z_ref = seqs[bi][3]
            for r in range(blk // rc):
                rs = slice(r * rc, (r + 1) * rc)
                for g in range(SSD_GROUPS):
                    gs = slice(g * gw, (g + 1) * gw)
                    yg = y_ref[bi, rs, gs] * _silu(z_ref[rs, gs].astype(F32))
                    ms = jnp.mean(yg * yg, axis=-1, keepdims=True)
                    o_ref[bi, rs, gs] = (yg * lax.rsqrt(ms + EPS) * ng_ref[:, gs]).astype(o_ref.dtype)


def _ssd_dir(lay, xc, bcc, small, U, cols, dtb, apad, expand, rev, dsk=None, norm_g=None, prev=None):
    B, NS, blk = lay.B, lay.NS, SEQ_BLOCK
    inner = xc.shape[1]
    bcw = bcc.shape[1]
    final = prev is not None
    seq = lambda n: lay.seq_order(n, rev)
    const = lambda shape: pl.BlockSpec(shape, lambda n: (0,) * len(shape))

    def rows(w, b, col=0):
        return pl.BlockSpec((blk, w), lambda n: (lay.row_block(b, seq(n)), col))

    in_specs, args = [], []
    for b in range(B):
        in_specs += [rows(inner, b), rows(bcw, b), rows(SMALL_COLS, b)]
        args += [xc, bcc, small]
        if final:
            zc = cols["ssd_z"]
            in_specs.append(rows(inner, b, zc[0] // zc[1]))
            args.append(U)
    in_specs += [const((1, SMALL_COLS)), const((1, SMALL_COLS)), const((SMALL_COLS, inner))]
    args += [dtb, apad, expand]
    per_batch = pl.BlockSpec((B, blk, inner), lambda n: (0, seq(n), 0))
    if final:
        in_specs += [const((1, inner)), const((1, inner)), per_batch]
        args += [dsk, norm_g, prev]
    return pl.pallas_call(
        functools.partial(_ssd_kernel, nb=B, rev=rev, final=final),
        out_shape=jax.ShapeDtypeStruct((B, NS * blk, inner), BF16 if final else F32),
        grid=(NS,),
        in_specs=in_specs,
        out_specs=per_batch,
        scratch_shapes=[pltpu.VMEM((B * SSD_GROUPS, SSD_STATE, inner // SSD_GROUPS), F32)]
        + ([pltpu.VMEM((B, blk, inner), F32)] if final else []),
        compiler_params=_cparams(("arbitrary",)),
        name="ssd_bwd" if rev else "ssd_fwd",
    )(*args)


def _rope_tables(CTX, SEQ):
    hd = ATT_HEAD_DIM
    pairs = hd // 4
    rows = SEQ // GRID_W
    row = jnp.repeat(jnp.arange(rows, dtype=F32), GRID_W)
    col = jnp.tile(jnp.arange(GRID_W, dtype=F32), rows)
    inv = jnp.exp(-math.log(ROPE_THETA) * jnp.arange(pairs, dtype=F32) / pairs)
    ar = row[:, None] * inv
    ac = col[:, None] * inv
    ang = jnp.concatenate([ar, ar, ac, ac], axis=-1)
    cos, sin = jnp.cos(ang), jnp.sin(ang)
    first = (jnp.arange(hd) % (2 * pairs)) < pairs
    sa = jnp.where(first, -sin, 0.0)
    sb = jnp.where(first, 0.0, sin)
    ones = jnp.ones((CTX, hd), F32)
    zeros = jnp.zeros((CTX, hd), F32)
    return (jnp.concatenate([ones, cos]), jnp.concatenate([zeros, sa]), jnp.concatenate([zeros, sb]))


def _column_plan(D):
    mix = D // 2
    att_q = mix
    att_kv = ATT_KV_HEADS * ATT_HEAD_DIM
    gla_v = mix
    gla_qk = mix // 2
    ssd_inner = mix
    ssd_bc = 2 * SSD_GROUPS * SSD_STATE
    order = [("att_q", att_q), ("gla_v", gla_v), ("gla_gate", gla_v), ("ssd_z", ssd_inner),
             ("ssd_x", ssd_inner), ("gla_q", gla_qk), ("gla_k", gla_qk), ("att_k", att_kv),
             ("att_v", att_kv), ("ssd_bc", ssd_bc)]
    cols, off = {}, 0
    for name, w in order:
        assert off % w == 0
        cols[name] = (off, w)
        off += w
    return cols, off


def _source_columns(D):
    mix = D // 2
    att_kv = ATT_KV_HEADS * ATT_HEAD_DIM
    ssd_bc = SSD_GROUPS * SSD_STATE
    heads = mix // SSD_HEAD_DIM
    widths = [("att_q", mix), ("att_k", att_kv), ("att_v", att_kv), ("gla_q", mix // 2), ("gla_k", mix // 2),
              ("gla_v", mix), ("gla_gate", mix), ("gla_lr", 2 * GLA_LOWRANK), ("ssd_z", mix),
              ("ssd_x", mix), ("ssd_bc", 2 * ssd_bc), ("ssd_dt", 2 * heads), ("gates", N_BRANCH * D)]
    src, off = {}, 0
    for name, w in widths:
        src[name] = (off, w)
        off += w
    return src, off


def kernel(x, c, ctx, c_ctx, norm1, norm2, w_ada, b_ada, w_in, att_q_norm, att_k_norm, gla_w_decay,
           gla_b_decay, gla_norm, ssd_conv_w, ssd_conv_b, ssd_dt_bias, ssd_a_log, ssd_d, ssd_norm,
           w_branch, w_out, w_ff1, w_ff2, final_norm):
    B, SEQ, D = x.shape
    CTX = ctx.shape[1]
    depth = w_in.shape[0]
    lay = _Layout(B, CTX, SEQ, D)
    cols, n_main = _column_plan(D)
    src, n_src = _source_columns(D)
    assert n_src == w_in.shape[2]
    mix = D // 2
    ssd_heads = mix // SSD_HEAD_DIM
    lr_w = 2 * GLA_LOWRANK
    assert lr_w + 2 * ssd_heads <= SMALL_COLS

    X = jnp.concatenate([ctx.reshape(B * CTX, D), x.reshape(B * SEQ, D)], axis=0)
    cvec = jnp.concatenate([c, c_ctx[None, :], jnp.zeros((MOD_ROWS - B - 1, D), F32)], axis=0)
    mods = _ada(cvec, w_ada, b_ada).reshape(depth, MOD_ROWS, 1, N_MOD * D)
    rope = _rope_tables(CTX, SEQ)

    expand = np.zeros((2, SMALL_COLS, mix), np.float32)
    for d in range(2):
        for h in range(ssd_heads):
            expand[d, lr_w + d * ssd_heads + h, h * SSD_HEAD_DIM:(h + 1) * SSD_HEAD_DIM] = 1.0
    expand = jnp.asarray(expand, BF16)

    take = lambda name: w_in[:, :, src[name][0]:src[name][0] + src[name][1]]
    w_main = jnp.concatenate([take(n) for n, _ in sorted(cols.items(), key=lambda kv: kv[1][0])],
                             axis=2).astype(BF16)
    w_small = jnp.concatenate([take("gla_lr"), take("ssd_dt"),
                               jnp.zeros((depth, D, SMALL_COLS - lr_w - 2 * ssd_heads), F32)],
                              axis=2).astype(BF16)
    w_gate = take("gates").astype(BF16)
    w_branch_b, w_out_b = w_branch.astype(BF16), w_out.astype(BF16)
    w_ff1_b, w_ff2_b = w_ff1.astype(BF16), w_ff2.astype(BF16)

    out = None
    for l in range(depth):
        mod = mods[l]
        g1 = norm1[l][None, :]

        U, small = _inproj(lay, X, mod, g1, w_main, w_small, l)

        qt, kk, vt = _qkprep(lay, U, cols, att_q_norm[l][None, :], att_k_norm[l][None, :], rope)
        o_att = _attention(lay, qt, kk, vt)

        gla_out = None
        for d in range(2):
            wd = jnp.zeros((SMALL_COLS, mix // 2), F32).at[d * GLA_LOWRANK:(d + 1) * GLA_LOWRANK].set(
                gla_w_decay[l, d]).astype(BF16)
            bd = gla_b_decay[l, d][None, :]
            if d == 0:
                gla_out = _gla_dir(lay, U, small, cols, wd, bd, rev=False)
            else:
                gla_out = _gla_dir(lay, U, small, cols, wd, bd, rev=True,
                                   norm_g=gla_norm[l][None, :], prev=gla_out)
        o_gla = gla_out

        cw, cbias = ssd_conv_w[l], ssd_conv_b[l]
        xc, bcc = _conv(lay, U, cols, cw[:, :mix], cbias[None, :mix], cw[:, mix:], cbias[None, mix:])
        ssd_out = None
        for d in range(2):
            lanes = slice(lr_w + d * ssd_heads, lr_w + (d + 1) * ssd_heads)
            dtb = jnp.zeros((1, SMALL_COLS), F32).at[0, lanes].set(ssd_dt_bias[l, d])
            apad = jnp.zeros((1, SMALL_COLS), F32).at[0, lanes].set(-jnp.exp(ssd_a_log[l, d]))
            if d == 0:
                ssd_out = _ssd_dir(lay, xc, bcc, small, U, cols, dtb, apad, expand[d], rev=False)
            else:
                ssd_out = _ssd_dir(lay, xc, bcc, small, U, cols, dtb, apad, expand[d], rev=True,
                                   dsk=jnp.repeat(ssd_d[l], SSD_HEAD_DIM)[None, :],
                                   norm_g=ssd_norm[l][None, :], prev=ssd_out)
        o_ssd = ssd_out

        last = l == depth - 1
        skip = B * CTX if last else 0
        m = _merge(lay, X, mod, g1, o_att, o_gla, o_ssd, w_gate, w_branch_b, l, skip)
        mlp_args = (lay, X, m, w_out_b, mod, norm2[l][None, :], w_ff1_b, w_ff2_b, l, skip)
        if last:
            out = _mlp(*mlp_args, final_gain=final_norm[None, :])
        else:
            X = _mlp(*mlp_args)
    return out.reshape(B, SEQ, D)
```

```python
import functools
import math

import jax
import jax.numpy as jnp
import numpy as np
from jax import lax
from jax.experimental import pallas as pl
from jax.experimental.pallas import tpu as pltpu

F32 = jnp.float32
BF16 = jnp.bfloat16

EPS = 1e-6
GRID_W = 64
N_BRANCH = 3
N_MOD = 6
ATT_HEAD_DIM = 128
ATT_KV_HEADS = 2
ROPE_THETA = 10000.0
GLA_HEADS = 4
GLA_LOWRANK = 16
GLA_TAU = 16.0
GLA_CHUNK = 64
SSD_HEAD_DIM = 64
SSD_GROUPS = 2
SSD_STATE = 128
SSD_CONV = 5
SSD_CHUNK = 64

SEQ_BLOCK = 256
RESIDENT_TM = 256
MOD_ROWS = 8
SMALL_COLS = 128
HALO_ROWS = 16
ATT_ONES_ROWS = 16
ATT_PAIR_UNROLL = 4
VMEM_LIMIT = 56 * 1024 * 1024


def _cparams(sem):
    return pltpu.CompilerParams(dimension_semantics=sem, vmem_limit_bytes=VMEM_LIMIT)


def _dot(a, b):
    return jnp.dot(a, b, preferred_element_type=F32)


def _dot_nt(a, b):
    return lax.dot_general(a, b, (((1,), (1,)), ((), ())), preferred_element_type=F32)


def _dot_tn(a, b):
    return lax.dot_general(a, b, (((0,), (0,)), ((), ())), preferred_element_type=F32)


def _split3(x):
    hi = x.astype(BF16)
    r = x - hi.astype(F32)
    mid = r.astype(BF16)
    lo = (r - mid.astype(F32)).astype(BF16)
    return hi, mid, lo


def _dot_exact_lhs(a, x):
    hi, mid, lo = _split3(x)
    return _dot(a, lo) + _dot(a, mid) + _dot(a, hi)


def _dot_spread(x, e):
    hi = x.astype(BF16)
    lo = (x - hi.astype(F32)).astype(BF16)
    return _dot(lo, e) + _dot(hi, e)


def _silu(x):
    return x * jax.nn.sigmoid(x)


def _softplus(x):
    return jnp.maximum(x, 0.0) + jnp.log1p(jnp.exp(-jnp.abs(x)))


def _block_tri(n, chunk, rev):
    r = lax.broadcasted_iota(jnp.int32, (n, n), 0)
    c = lax.broadcasted_iota(jnp.int32, (n, n), 1)
    shift = chunk.bit_length() - 1
    assert chunk == 1 << shift
    same = jnp.right_shift(r, shift) == jnp.right_shift(c, shift)
    tri = (c >= r) if rev else (c <= r)
    return jnp.where(same, jnp.where(tri, 1.0, 0.0), 0.0).astype(BF16)


class _Layout:
    def __init__(self, B, CTX, SEQ, D):
        assert CTX % SEQ_BLOCK == 0 and SEQ % SEQ_BLOCK == 0
        self.B, self.CTX, self.SEQ, self.D = B, CTX, SEQ, D
        self.R = B * (CTX + SEQ)
        self.NC = CTX // SEQ_BLOCK
        self.NL = SEQ // SEQ_BLOCK
        self.NS = self.NC + self.NL
        self.tm = next(t for t in (512, 256) if (B * CTX) % t == 0 and SEQ % t == 0)
        assert B + 1 <= MOD_ROWS

    def row_block(self, b, s):
        return jnp.where(s < self.NC, b * self.NC + s, self.B * self.NC + b * self.NL + (s - self.NC))

    def batch_seq(self, i):
        nctx = self.B * self.NC
        j = jnp.maximum(i - nctx, 0)
        return (jnp.where(i < nctx, i // self.NC, j // self.NL),
                jnp.where(i < nctx, i % self.NC, self.NC + j % self.NL))

    def seq_order(self, n, rev):
        if not rev:
            return n
        return jnp.where(n < self.NC, self.NC - 1 - n, self.NC + self.NL - 1 - (n - self.NC))

    def mod_row(self, i, tm):
        nctx = (self.B * self.CTX) // tm
        return jnp.where(i < nctx, self.B, (i - nctx) // (self.SEQ // tm))


def _ada_kernel(c_ref, w_ref, b_ref, o_ref):
    s = _silu(c_ref[...]).astype(BF16)
    o_ref[0] = _dot(s, w_ref[0].astype(BF16)) + b_ref[0]


def _ada(cvec, w_ada, b_ada):
    depth, D, n = w_ada.shape
    tn = 1024
    return pl.pallas_call(
        _ada_kernel,
        out_shape=jax.ShapeDtypeStruct((depth, MOD_ROWS, n), F32),
        grid=(depth, n // tn),
        in_specs=[
            pl.BlockSpec((MOD_ROWS, D), lambda l, j: (0, 0)),
            pl.BlockSpec((1, D, tn), lambda l, j: (l, 0, j)),
            pl.BlockSpec((1, 1, tn), lambda l, j: (l, 0, j)),
        ],
        out_specs=pl.BlockSpec((1, MOD_ROWS, tn), lambda l, j: (l, 0, j)),
        compiler_params=_cparams(("arbitrary", "arbitrary")),
        name="ada",
    )(cvec, w_ada, b_ada.reshape(depth, 1, n))


def _norm_mod(x_ref, g_ref, mod_ref, k, h_ref):
    D = x_ref.shape[-1]
    shift = mod_ref[:, k * D:(k + 1) * D]
    a = g_ref[...] * (1.0 + mod_ref[:, (k + 1) * D:(k + 2) * D])
    rc = 64

    def body(r, carry):
        sl = pl.ds(pl.multiple_of(r * rc, rc), rc)
        xf = x_ref[sl, :]
        ms = jnp.mean(xf * xf, axis=-1, keepdims=True)
        h_ref[sl, :] = (xf * lax.rsqrt(ms + EPS) * a + shift).astype(h_ref.dtype)
        return carry

    lax.fori_loop(0, x_ref.shape[0] // rc, body, 0)


def _resident(shape, layer=None):
    if layer is None:
        return pl.BlockSpec(shape, lambda *_: (0,) * len(shape), pipeline_mode=pl.Buffered(1))
    return pl.BlockSpec((None,) + tuple(shape), lambda *_: (layer,) + (0,) * len(shape),
                        pipeline_mode=pl.Buffered(1))


def _inproj_kernel(x_ref, mod_ref, g_ref, wa_ref, wb_ref, ws_ref, qg_ref, kg_ref, cos_ref, sa_ref, sb_ref,
                   u_ref, small_ref, qt_ref, ko_ref, vt_ref, h_ref, *, tn, att_q, att_kv):
    _norm_mod(x_ref, g_ref, mod_ref, 0, h_ref)
    h = h_ref[...]
    hd = ATT_HEAD_DIM
    cos, sa, sb = cos_ref[...], sa_ref[...], sb_ref[...]

    def prep(y, g):
        y = y * lax.rsqrt(jnp.mean(y * y, axis=-1, keepdims=True) + EPS) * g
        return y * cos + pltpu.roll(y, hd - hd // 4, 1) * sa + pltpu.roll(y, hd // 4, 1) * sb

    q_scale = (hd ** -0.5) * math.log2(math.e)
    for j in range(att_q // tn):
        q = _dot(h, wa_ref[:, j * tn:(j + 1) * tn])
        for hh in range(tn // hd):
            qh = prep(q[:, hh * hd:(hh + 1) * hd], qg_ref[...]) * q_scale
            qt_ref[j * (tn // hd) + hh] = qh.T.astype(qt_ref.dtype)
    kv = _dot(h, wa_ref[:, att_q:att_q + 2 * att_kv])
    for g in range(att_kv // hd):
        sl = slice(g * hd, (g + 1) * hd)
        ko_ref[:, sl] = prep(kv[:, sl], kg_ref[...]).astype(ko_ref.dtype)
        vt_ref[g, 0:hd, :] = kv[:, att_kv + g * hd:att_kv + (g + 1) * hd].T.astype(vt_ref.dtype)
        vt_ref[g, hd:, :] = jnp.ones((ATT_ONES_ROWS, x_ref.shape[0]), vt_ref.dtype)

    small_ref[...] = _dot(h, ws_ref[...])
    a0 = att_q + 2 * att_kv
    na = wa_ref.shape[1] - a0
    for j in range(na // tn):
        u_ref[:, j * tn:(j + 1) * tn] = _dot(h, wa_ref[:, a0 + j * tn:a0 + (j + 1) * tn]).astype(u_ref.dtype)
    for j in range(wb_ref.shape[1] // tn):
        u_ref[:, na + j * tn:na + (j + 1) * tn] = _dot(h, wb_ref[:, j * tn:(j + 1) * tn]).astype(u_ref.dtype)


def _inproj(lay, X, mod, g, w_a, w_b, w_small, q_gain, k_gain, rope, l):
    _, D, n_a = w_a.shape
    n_b = w_b.shape[2]
    tm = RESIDENT_TM
    assert tm == SEQ_BLOCK
    hd = ATT_HEAD_DIM
    att_q = D // 2
    att_kv = ATT_KV_HEADS * hd
    heads = att_q // hd
    n_u = n_a - att_q - 2 * att_kv + n_b
    B, NS = lay.B, lay.NS
    bs = lay.batch_seq
    tab = pl.BlockSpec((tm, hd), lambda i: (bs(i)[1], 0))
    return pl.pallas_call(
        functools.partial(_inproj_kernel, tn=512, att_q=att_q, att_kv=att_kv),
        out_shape=(jax.ShapeDtypeStruct((lay.R, n_u), BF16),
                   jax.ShapeDtypeStruct((lay.R, SMALL_COLS), F32),
                   jax.ShapeDtypeStruct((lay.R // tm, heads, hd, tm), BF16),
                   jax.ShapeDtypeStruct((B, NS * tm, att_kv), BF16),
                   jax.ShapeDtypeStruct((B, ATT_KV_HEADS, NS, hd + ATT_ONES_ROWS, tm), BF16)),
        grid=(lay.R // tm,),
        in_specs=[
            pl.BlockSpec((tm, D), lambda i: (i, 0)),
            pl.BlockSpec((None, 1, N_MOD * D), lambda i: (lay.mod_row(i, tm), 0, 0)),
            _resident((1, D)),
            _resident((D, n_a), l),
            _resident((D, n_b), l),
            _resident((D, SMALL_COLS), l),
            _resident((1, hd)), _resident((1, hd)),
            tab, tab, tab,
        ],
        out_specs=(pl.BlockSpec((tm, n_u), lambda i: (i, 0)),
                   pl.BlockSpec((tm, SMALL_COLS), lambda i: (i, 0)),
                   pl.BlockSpec((None, heads, hd, tm), lambda i: (i, 0, 0, 0)),
                   pl.BlockSpec((None, tm, att_kv), lambda i: (*bs(i), 0)),
                   pl.BlockSpec((None, ATT_KV_HEADS, None, hd + ATT_ONES_ROWS, tm),
                                lambda i: (bs(i)[0], 0, bs(i)[1], 0, 0))),
        scratch_shapes=[pltpu.VMEM((tm, D), BF16)],
        compiler_params=_cparams(("arbitrary",)),
        name="in_proj",
    )(X, mod, g, w_a, w_b, w_small, q_gain, k_gain, *rope)


def _merge_kernel(x_ref, mod_ref, g_ref, oa_ref, og_ref, os_ref, wg_ref, wb_ref, m_ref, h_ref, *, tn):
    _norm_mod(x_ref, g_ref, mod_ref, 0, h_ref)
    h = h_ref[...]
    D = m_ref.shape[1]
    for j in range(D // tn):
        acc = None
        for b, o_ref in enumerate((oa_ref, og_ref, os_ref)):
            gate = _dot(h, wg_ref[:, b * D + j * tn:b * D + (j + 1) * tn])
            t = jax.nn.sigmoid(gate) * _dot(o_ref[...], wb_ref[b, :, j * tn:(j + 1) * tn])
            acc = t if acc is None else acc + t
        m_ref[:, j * tn:(j + 1) * tn] = acc.astype(m_ref.dtype)


def _merge(lay, X, mod, g, o_att, o_gla, o_ssd, w_gate, w_branch, l, skip_rows):
    D = lay.D
    W = o_att.shape[1]
    tm = RESIDENT_TM
    assert tm == SEQ_BLOCK
    row0 = skip_rows // tm
    rows = lambda i: (i + row0, 0)
    seq_rows = pl.BlockSpec((None, tm, W), lambda i: (*lay.batch_seq(i + row0), 0))
    return pl.pallas_call(
        functools.partial(_merge_kernel, tn=512),
        out_shape=jax.ShapeDtypeStruct((lay.R, D), BF16),
        grid=(lay.R // tm - row0,),
        in_specs=[
            pl.BlockSpec((tm, D), rows),
            pl.BlockSpec((None, 1, N_MOD * D), lambda i: (lay.mod_row(i + row0, tm), 0, 0)),
            _resident((1, D)),
            pl.BlockSpec((tm, W), rows), seq_rows, seq_rows,
            _resident(w_gate.shape[1:], l), _resident(w_branch.shape[1:], l),
        ],
        out_specs=pl.BlockSpec((tm, D), rows),
        scratch_shapes=[pltpu.VMEM((tm, D), BF16)],
        compiler_params=_cparams(("arbitrary",)),
        name="merge",
    )(X, mod, g, o_att, o_gla, o_ssd, w_gate, w_branch)


def _mlp_kernel(x_ref, m_ref, wo_ref, mod_ref, g_ref, w1_ref, w2_ref, *rest, final):
    if final:
        fg_ref, o_ref, x1_ref, h_ref = rest
    else:
        o_ref, x1_ref, h_ref = rest
    k = pl.program_id(1)
    D = x_ref.shape[-1]

    @pl.when(k == 0)
    def _():
        tn = 512
        for j in range(D // tn):
            sl = slice(j * tn, (j + 1) * tn)
            x1_ref[:, sl] = x_ref[:, sl] + mod_ref[:, 2 * D + j * tn:2 * D + (j + 1) * tn] * _dot(
                m_ref[...], wo_ref[:, sl])
        _norm_mod(x1_ref, g_ref, mod_ref, 3, h_ref)
        o_ref[...] = jnp.zeros_like(o_ref)

    f = jnp.square(jnp.maximum(_dot(h_ref[...], w1_ref[...]), 0.0)).astype(BF16)
    o_ref[...] += _dot(f, w2_ref[...])

    @pl.when(k == pl.num_programs(1) - 1)
    def _():
        gate = mod_ref[:, 5 * D:6 * D]
        rc = 64

        def body(r, carry):
            sl = pl.ds(pl.multiple_of(r * rc, rc), rc)
            y = x1_ref[sl, :] + gate * o_ref[sl, :]
            if final:
                ms = jnp.mean(y * y, axis=-1, keepdims=True)
                y = y * lax.rsqrt(ms + EPS) * fg_ref[...]
            o_ref[sl, :] = y
            return carry

        lax.fori_loop(0, x_ref.shape[0] // rc, body, 0)


def _mlp(lay, X, m, w_out, mod, g, w1, w2, l, skip_rows, final_gain=None):
    _, D, FF = w1.shape
    tm, tf = lay.tm, 1024
    row0 = skip_rows // tm
    final = final_gain is not None
    out_rows = lay.R - skip_rows if final else lay.R
    out_off = 0 if final else row0
    rows = lambda i, k: (i + row0, 0)
    in_specs = [
        pl.BlockSpec((tm, D), rows),
        pl.BlockSpec((tm, D), rows),
        _resident((D, D), l),
        pl.BlockSpec((None, 1, N_MOD * D), lambda i, k: (lay.mod_row(i + row0, tm), 0, 0)),
        _resident((1, D)),
        pl.BlockSpec((None, D, tf), lambda i, k: (l, 0, k)),
        pl.BlockSpec((None, tf, D), lambda i, k: (l, k, 0)),
    ]
    args = [X, m, w_out, mod, g, w1, w2]
    if final:
        in_specs.append(_resident((1, D)))
        args.append(final_gain)
    return pl.pallas_call(
        functools.partial(_mlp_kernel, final=final),
        out_shape=jax.ShapeDtypeStruct((out_rows, D), F32),
        grid=(lay.R // tm - row0, FF // tf),
        in_specs=in_specs,
        out_specs=pl.BlockSpec((tm, D), lambda i, k: (i + out_off, 0)),
        scratch_shapes=[pltpu.VMEM((tm, D), F32), pltpu.VMEM((tm, D), BF16)],
        compiler_params=_cparams(("arbitrary", "arbitrary")),
        name="mlp",
    )(*args)


def _attn_kernel(qt_ref, k_ref, vt_ref, o_ref, m_ref, acc_ref, s0_ref, s1_ref, *, n_ctx_blocks):
    n = pl.program_id(2)
    heads, hd, tq = qt_ref.shape
    nblk, vrows, tk = vt_ref.shape
    n_kv = jnp.where(n < n_ctx_blocks, n_ctx_blocks, nblk)
    m_ref[...] = jnp.full(m_ref.shape, -1e30, F32)
    acc_ref[...] = jnp.zeros(acc_ref.shape, F32)

    def scores(c, s_ref):
        c = jnp.minimum(c, n_kv - 1)
        k = k_ref[pl.ds(pl.multiple_of(c * tk, tk), tk), :]
        for h in range(heads):
            s_ref[h] = _dot(k, qt_ref[h])

    def update(c, s_ref):
        vt = vt_ref[c]
        for h in range(heads):
            st = s_ref[h]
            m_old = m_ref[h]
            m_new = jnp.maximum(m_old, jnp.max(st, axis=0, keepdims=True))
            pt = jnp.exp2((st - m_new).astype(BF16))
            acc_ref[h] = jnp.exp2(m_old - m_new) * acc_ref[h] + _dot(vt, pt)
            m_ref[h] = m_new

    scores(0, s0_ref)

    def pairs(c0, n_pairs):
        for j in range(n_pairs):
            scores(c0 + 2 * j + 1, s1_ref)
            update(c0 + 2 * j, s0_ref)
            scores(c0 + 2 * j + 2, s0_ref)
            update(c0 + 2 * j + 1, s1_ref)

    n_pairs = n_kv // 2
    unroll = ATT_PAIR_UNROLL

    def body(i, carry):
        pairs(2 * unroll * i, unroll)
        return carry

    lax.fori_loop(0, n_pairs // unroll, body, 0)
    done = (n_pairs // unroll) * unroll
    piece = unroll // 2
    while piece >= 1:
        @pl.when(((n_pairs - done) & piece) != 0)
        def _(done=done, piece=piece):
            pairs(2 * done, piece)

        done = done + ((n_pairs - done) & piece)
        piece //= 2

    @pl.when(n_kv % 2 == 1)
    def _():
        update(n_kv - 1, s0_ref)

    for h in range(heads):
        acc = acc_ref[h]
        ot = acc[0:hd] / acc[hd:hd + 1]
        o_ref[:, h * hd:(h + 1) * hd] = ot.T.astype(o_ref.dtype)


def _attention(lay, qt, kk, vt):
    B, kvh, NS, vrows, blk = vt.shape
    _, heads, hd, _ = qt.shape
    group = heads // kvh
    return pl.pallas_call(
        functools.partial(_attn_kernel, n_ctx_blocks=lay.NC),
        out_shape=jax.ShapeDtypeStruct((lay.R, heads * hd), BF16),
        grid=(B, kvh, NS),
        in_specs=[
            pl.BlockSpec((None, group, hd, blk), lambda b, g, n: (lay.row_block(b, n), g, 0, 0)),
            pl.BlockSpec((None, NS * blk, hd), lambda b, g, n: (b, 0, g)),
            pl.BlockSpec((None, None, NS, vrows, blk), lambda b, g, n: (b, g, 0, 0, 0)),
        ],
        out_specs=pl.BlockSpec((blk, group * hd), lambda b, g, n: (lay.row_block(b, n), g)),
        scratch_shapes=[pltpu.VMEM((group, 1, blk), F32), pltpu.VMEM((group, vrows, blk), F32),
                        pltpu.VMEM((group, blk, blk), F32), pltpu.VMEM((group, blk, blk), F32)],
        compiler_params=_cparams(("arbitrary", "arbitrary", "arbitrary")),
        name="attention",
    )(qt, kk, vt)


def _gla_kernel(*refs, nb, rev, final, dk, dv):
    per_b = 5 if final else 4
    seqs = [refs[i * per_b:(i + 1) * per_b] for i in range(nb)]
    rest = refs[nb * per_b:]
    if final:
        wd_ref, bd_ref, ng_ref, prev_ref, o_ref, st_ref = rest
    else:
        wd_ref, bd_ref, o_ref, st_ref = rest
    blk = seqs[0][0].shape[0]
    heads = seqs[0][0].shape[1] // dk
    C = GLA_CHUNK

    @pl.when(pl.program_id(0) == 0)
    def _():
        st_ref[...] = jnp.zeros_like(st_ref)

    tri = _block_tri(blk, C, rev)
    bcums = []
    for bi in range(nb):
        z = _dot(seqs[bi][3][...].astype(BF16), wd_ref[...]) + bd_ref[...]
        la = (jnp.minimum(z, 0.0) - jnp.log1p(jnp.exp(-jnp.abs(z)))) * (1.0 / GLA_TAU)
        bcums.append(_dot_exact_lhs(tri, la))

    ri = lax.broadcasted_iota(jnp.int32, (C, C), 0)
    ci = lax.broadcasted_iota(jnp.int32, (C, C), 1)
    causal = (ci >= ri) if rev else (ci <= ri)

    n_chunks = blk // C
    for step in range(n_chunks):
        c = n_chunks - 1 - step if rev else step
        rs = slice(c * C, (c + 1) * C)
        last = c * C if rev else (c + 1) * C - 1
        units = [(h, bi) for h in range(heads) for bi in range(nb)]
        scores, inter = [], []
        for h, bi in units:
            ks = slice(h * dk, (h + 1) * dk)
            q_ref, k_ref, v_ref = seqs[bi][:3]
            b = bcums[bi][rs, ks]
            b_last = bcums[bi][last:last + 1, ks]
            q = q_ref[rs, ks].astype(F32) * (dk ** -0.5)
            k = k_ref[rs, ks].astype(F32)
            q_dec = (q * jnp.exp(b)).astype(BF16)
            k_dec = (k * jnp.exp(-b)).astype(BF16)
            k_st = (k * jnp.exp(b_last - b)).astype(BF16)
            st = st_ref[bi * heads + h]
            scores.append(_dot_nt(q_dec, k_dec))
            inter.append(_dot_nt(q_dec, st.astype(BF16)))
            st_ref[bi * heads + h] = st * jnp.exp(b_last) + _dot_tn(v_ref[rs, h * dv:(h + 1) * dv], k_st)
        for (h, bi), sc, it in zip(units, scores, inter):
            vs = slice(h * dv, (h + 1) * dv)
            att = jnp.where(causal, sc, 0.0).astype(BF16)
            o = _dot(att, seqs[bi][2][rs, vs]) + it
            if final:
                tot = o + prev_ref[bi, rs, vs]
                y = tot * lax.rsqrt(jnp.mean(tot * tot, axis=-1, keepdims=True) + EPS) * ng_ref[...]
                o_ref[bi, rs, vs] = (y * _silu(seqs[bi][4][rs, vs].astype(F32))).astype(o_ref.dtype)
            else:
                o_ref[bi, rs, vs] = o


def _gla_dir(lay, U, small, cols, w_dec, b_dec, rev, norm_g=None, prev=None):
    B, NS, blk = lay.B, lay.NS, SEQ_BLOCK
    qk_w = cols["gla_q"][1]
    v_w = cols["gla_v"][1]
    dk, dv = qk_w // GLA_HEADS, v_w // GLA_HEADS
    final = prev is not None
    const = lambda shape: pl.BlockSpec(shape, lambda n: (0,) * len(shape))
    seq = lambda n: lay.seq_order(n, rev)

    def ucol(name, b):
        off, w = cols[name]
        return pl.BlockSpec((blk, w), lambda n: (lay.row_block(b, seq(n)), off // w))

    in_specs, args = [], []
    for b in range(B):
        in_specs += [ucol("gla_q", b), ucol("gla_k", b), ucol("gla_v", b),
                     pl.BlockSpec((blk, SMALL_COLS), lambda n, b=b: (lay.row_block(b, seq(n)), 0))]
        args += [U, U, U, small]
        if final:
            in_specs.append(ucol("gla_gate", b))
            args.append(U)
    in_specs += [const((SMALL_COLS, qk_w)), const((1, qk_w))]
    args += [w_dec, b_dec]
    per_batch = pl.BlockSpec((B, blk, v_w), lambda n: (0, seq(n), 0))
    if final:
        in_specs += [const((1, dv)), per_batch]
        args += [norm_g, prev]
    return pl.pallas_call(
        functools.partial(_gla_kernel, nb=B, rev=rev, final=final, dk=dk, dv=dv),
        out_shape=jax.ShapeDtypeStruct((B, NS * blk, v_w), BF16 if final else F32),
        grid=(NS,),
        in_specs=in_specs,
        out_specs=per_batch,
        scratch_shapes=[pltpu.VMEM((B * GLA_HEADS, dv, dk), F32)],
        compiler_params=_cparams(("arbitrary",)),
        name="gla_bwd" if rev else "gla_fwd",
    )(*args)


def _conv_kernel(x_ref, xp_ref, xn_ref, bc_ref, bcp_ref, bcn_ref, wx_ref, bx_ref, wbc_ref, bbc_ref,
                 xo_ref, bco_ref, ex_ref, ebc_ref, *, n_ctx_total, NC, NL):
    i = pl.program_id(0)
    in_ctx = i < n_ctx_total
    p = jnp.where(in_ctx, i % NC, jnp.maximum(i - n_ctx_total, 0) % NL)
    seg = jnp.where(in_ctx, NC, NL)
    has_left = (p > 0).astype(F32)
    has_right = (p < seg - 1).astype(F32)
    blk = x_ref.shape[0]
    pad = SSD_CONV // 2

    def run(m_ref, p_ref, n_ref, w_ref, b_ref, e_ref, o_ref):
        e_ref[0:HALO_ROWS] = p_ref[...].astype(F32) * has_left
        e_ref[HALO_ROWS:HALO_ROWS + blk] = m_ref[...].astype(F32)
        e_ref[HALO_ROWS + blk:] = n_ref[...].astype(F32) * has_right
        acc = b_ref[...]
        for j in range(SSD_CONV):
            acc = acc + w_ref[j:j + 1, :] * e_ref[HALO_ROWS - pad + j:HALO_ROWS - pad + j + blk, :]
        o_ref[...] = _silu(acc).astype(o_ref.dtype)

    run(x_ref, xp_ref, xn_ref, wx_ref, bx_ref, ex_ref, xo_ref)
    run(bc_ref, bcp_ref, bcn_ref, wbc_ref, bbc_ref, ebc_ref, bco_ref)


def _conv(lay, U, cols, wx, bx, wbc, bbc):
    blk = SEQ_BLOCK
    nb = lay.R // blk
    hb = blk // HALO_ROWS
    nh = lay.R // HALO_ROWS
    xw = cols["ssd_x"][1]
    bcw = cols["ssd_bc"][1]
    xc, bcc = cols["ssd_x"][0] // xw, cols["ssd_bc"][0] // bcw
    prev = lambda i: jnp.maximum(i * hb - 1, 0)
    nxt = lambda i: jnp.minimum((i + 1) * hb, nh - 1)
    const = lambda shape: pl.BlockSpec(shape, lambda i: (0,) * len(shape))
    return pl.pallas_call(
        functools.partial(_conv_kernel, n_ctx_total=lay.B * lay.NC, NC=lay.NC, NL=lay.NL),
        out_shape=(jax.ShapeDtypeStruct((lay.R, xw), BF16), jax.ShapeDtypeStruct((lay.R, bcw), BF16)),
        grid=(nb,),
        in_specs=[
            pl.BlockSpec((blk, xw), lambda i: (i, xc)),
            pl.BlockSpec((HALO_ROWS, xw), lambda i: (prev(i), xc)),
            pl.BlockSpec((HALO_ROWS, xw), lambda i: (nxt(i), xc)),
            pl.BlockSpec((blk, bcw), lambda i: (i, bcc)),
            pl.BlockSpec((HALO_ROWS, bcw), lambda i: (prev(i), bcc)),
            pl.BlockSpec((HALO_ROWS, bcw), lambda i: (nxt(i), bcc)),
            const((SSD_CONV, xw)), const((1, xw)), const((SSD_CONV, bcw)), const((1, bcw)),
        ],
        out_specs=(pl.BlockSpec((blk, xw), lambda i: (i, 0)), pl.BlockSpec((blk, bcw), lambda i: (i, 0))),
        scratch_shapes=[pltpu.VMEM((blk + 2 * HALO_ROWS, xw), F32),
                        pltpu.VMEM((blk + 2 * HALO_ROWS, bcw), F32)],
        compiler_params=_cparams(("arbitrary",)),
        name="ssd_conv",
    )(U, U, U, U, U, U, wx, bx, wbc, bbc)


def _ssd_kernel(*refs, nb, rev, final):
    per_b = 4 if final else 3
    seqs = [refs[i * per_b:(i + 1) * per_b] for i in range(nb)]
    rest = refs[nb * per_b:]
    if final:
        dtb_ref, apad_ref, e_ref, dsk_ref, ng_ref, prev_ref, o_ref, st_ref, y_ref = rest
    else:
        dtb_ref, apad_ref, e_ref, o_ref, st_ref = rest
        y_ref = o_ref
    blk, inner = seqs[0][0].shape
    N = SSD_STATE
    P = SSD_HEAD_DIM
    C = SSD_CHUNK
    gw = inner // SSD_GROUPS
    pair = 2 * P

    @pl.when(pl.program_id(0) == 0)
    def _():
        st_ref[...] = jnp.zeros_like(st_ref)

    tri = _block_tri(blk, C, rev)
    e = e_ref[...]
    dt_fulls, cs_fulls = [], []
    for bi in range(nb):
        dt16 = _softplus(seqs[bi][2][...] + dtb_ref[...])
        cs16 = _dot_exact_lhs(tri, dt16 * apad_ref[...])
        dt_fulls.append(_dot_spread(dt16, e))
        cs_fulls.append(_dot_spread(cs16, e))

    ri = lax.broadcasted_iota(jnp.int32, (C, pair), 0)
    ci = lax.broadcasted_iota(jnp.int32, (C, pair), 1)
    cm = jnp.where(ci >= P, ci - P, ci)
    eye2 = cm == ri
    causal2 = (cm >= ri) if rev else (cm <= ri)
    left = ci < P
    zero = jnp.zeros((C, pair), BF16)

    n_chunks = blk // C
    for step in range(n_chunks):
        c = n_chunks - 1 - step if rev else step
        rs = slice(c * C, (c + 1) * C)
        last = c * C if rev else (c + 1) * C - 1
        units = [(g, bi) for g in range(SSD_GROUPS) for bi in range(nb)]
        held = []
        for g, bi in units:
            gs = slice(g * gw, (g + 1) * gw)
            x_ref, bc_ref = seqs[bi][:2]
            bm = bc_ref[rs, g * N:(g + 1) * N]
            cmat = bc_ref[rs, SSD_GROUPS * N + g * N:SSD_GROUPS * N + (g + 1) * N]
            cb2 = _dot_nt(cmat, jnp.concatenate([bm, bm], axis=0))
            cs_g = cs_fulls[bi][rs, gs]
            cs_last = cs_fulls[bi][last:last + 1, gs]
            xdt = x_ref[rs, gs].astype(F32) * dt_fulls[bi][rs, gs]
            st = st_ref[bi * SSD_GROUPS + g]
            y_inter = _dot(cmat, st.astype(BF16)) * jnp.exp(cs_g)
            w = (xdt * jnp.exp(cs_last - cs_g)).astype(BF16)
            st_ref[bi * SSD_GROUPS + g] = st * jnp.exp(cs_last) + _dot_tn(bm, w)
            held.append((cb2, cs_g, y_inter, xdt.astype(BF16)))
        for (g, bi), (cb2, cs_g, y_inter, xdt_b) in zip(units, held):
            x_ref = seqs[bi][0]
            for pr in range(gw // pair):
                ps = slice(pr * pair, (pr + 1) * pair)
                col = cs_g[:, ps]
                rowv = jnp.sum(jnp.where(eye2, col, 0.0), axis=0, keepdims=True)
                lmat = jnp.exp(jnp.where(causal2, col - rowv, -jnp.inf))
                mm = (lmat * cb2).astype(BF16)
                xp = xdt_b[:, ps]
                rhs = jnp.concatenate([jnp.where(left, xp, zero), jnp.where(left, zero, xp)], axis=0)
                y = _dot(mm, rhs) + y_inter[:, ps]
                os_ = slice(g * gw + pr * pair, g * gw + (pr + 1) * pair)
                if final:
                    y_ref[bi, rs, os_] = (y + prev_ref[bi, rs, os_]
                                          + x_ref[rs, os_].astype(F32) * dsk_ref[:, os_])
                else:
                    y_ref[bi, rs, os_] = y

    if final:
        rc = 64
        for bi in range(nb):
            z_ref = seqs[bi][3]
            for r in range(blk // rc):
                rs = slice(r * rc, (r + 1) * rc)
                for g in range(SSD_GROUPS):
                    gs = slice(g * gw, (g + 1) * gw)
                    yg = y_ref[bi, rs, gs] * _silu(z_ref[rs, gs].astype(F32))
                    ms = jnp.mean(yg * yg, axis=-1, keepdims=True)
                    o_ref[bi, rs, gs] = (yg * lax.rsqrt(ms + EPS) * ng_ref[:, gs]).astype(o_ref.dtype)


def _ssd_dir(lay, xc, bcc, small, U, cols, dtb, apad, expand, rev, dsk=None, norm_g=None, prev=None):
    B, NS, blk = lay.B, lay.NS, SEQ_BLOCK
    inner = xc.shape[1]
    bcw = bcc.shape[1]
    final = prev is not None
    seq = lambda n: lay.seq_order(n, rev)
    const = lambda shape: pl.BlockSpec(shape, lambda n: (0,) * len(shape))

    def rows(w, b, col=0):
        return pl.BlockSpec((blk, w), lambda n: (lay.row_block(b, seq(n)), col))

    in_specs, args = [], []
    for b in range(B):
        in_specs += [rows(inner, b), rows(bcw, b), rows(SMALL_COLS, b)]
        args += [xc, bcc, small]
        if final:
            zc = cols["ssd_z"]
            in_specs.append(rows(inner, b, zc[0] // zc[1]))
            args.append(U)
    in_specs += [const((1, SMALL_COLS)), const((1, SMALL_COLS)), const((SMALL_COLS, inner))]
    args += [dtb, apad, expand]
    per_batch = pl.BlockSpec((B, blk, inner), lambda n: (0, seq(n), 0))
    if final:
        in_specs += [const((1, inner)), const((1, inner)), per_batch]
        args += [dsk, norm_g, prev]
    return pl.pallas_call(
        functools.partial(_ssd_kernel, nb=B, rev=rev, final=final),
        out_shape=jax.ShapeDtypeStruct((B, NS * blk, inner), BF16 if final else F32),
        grid=(NS,),
        in_specs=in_specs,
        out_specs=per_batch,
        scratch_shapes=[pltpu.VMEM((B * SSD_GROUPS, SSD_STATE, inner // SSD_GROUPS), F32)]
        + ([pltpu.VMEM((B, blk, inner), F32)] if final else []),
        compiler_params=_cparams(("arbitrary",)),
        name="ssd_bwd" if rev else "ssd_fwd",
    )(*args)


def _rope_tables(CTX, SEQ):
    hd = ATT_HEAD_DIM
    pairs = hd // 4
    rows = SEQ // GRID_W
    row = jnp.repeat(jnp.arange(rows, dtype=F32), GRID_W)
    col = jnp.tile(jnp.arange(GRID_W, dtype=F32), rows)
    inv = jnp.exp(-math.log(ROPE_THETA) * jnp.arange(pairs, dtype=F32) / pairs)
    ar = row[:, None] * inv
    ac = col[:, None] * inv
    ang = jnp.concatenate([ar, ar, ac, ac], axis=-1)
    cos, sin = jnp.cos(ang), jnp.sin(ang)
    first = (jnp.arange(hd) % (2 * pairs)) < pairs
    sa = jnp.where(first, -sin, 0.0)
    sb = jnp.where(first, 0.0, sin)
    ones = jnp.ones((CTX, hd), F32)
    zeros = jnp.zeros((CTX, hd), F32)
    return (jnp.concatenate([ones, cos]), jnp.concatenate([zeros, sa]), jnp.concatenate([zeros, sb]))


def _column_plan(D):
    mix = D // 2
    gla_v = mix
    gla_qk = mix // 2
    ssd_inner = mix
    ssd_bc = 2 * SSD_GROUPS * SSD_STATE
    order = [("gla_q", gla_qk), ("gla_k", gla_qk), ("gla_v", gla_v), ("gla_gate", gla_v),
             ("ssd_z", ssd_inner), ("ssd_x", ssd_inner), ("ssd_bc", ssd_bc)]
    cols, off = {}, 0
    for name, w in order:
        assert off % w == 0
        cols[name] = (off, w)
        off += w
    return cols, off


def _source_columns(D):
    mix = D // 2
    att_kv = ATT_KV_HEADS * ATT_HEAD_DIM
    ssd_bc = SSD_GROUPS * SSD_STATE
    heads = mix // SSD_HEAD_DIM
    widths = [("att_q", mix), ("att_k", att_kv), ("att_v", att_kv), ("gla_q", mix // 2), ("gla_k", mix // 2),
              ("gla_v", mix), ("gla_gate", mix), ("gla_lr", 2 * GLA_LOWRANK), ("ssd_z", mix),
              ("ssd_x", mix), ("ssd_bc", 2 * ssd_bc), ("ssd_dt", 2 * heads), ("gates", N_BRANCH * D)]
    src, off = {}, 0
    for name, w in widths:
        src[name] = (off, w)
        off += w
    return src, off


def kernel(x, c, ctx, c_ctx, norm1, norm2, w_ada, b_ada, w_in, att_q_norm, att_k_norm, gla_w_decay,
           gla_b_decay, gla_norm, ssd_conv_w, ssd_conv_b, ssd_dt_bias, ssd_a_log, ssd_d, ssd_norm,
           w_branch, w_out, w_ff1, w_ff2, final_norm):
    B, SEQ, D = x.shape
    CTX = ctx.shape[1]
    depth = w_in.shape[0]
    lay = _Layout(B, CTX, SEQ, D)
    cols, n_main = _column_plan(D)
    src, n_src = _source_columns(D)
    assert n_src == w_in.shape[2]
    mix = D // 2
    ssd_heads = mix // SSD_HEAD_DIM
    lr_w = 2 * GLA_LOWRANK
    assert lr_w + 2 * ssd_heads <= SMALL_COLS

    X = jnp.concatenate([ctx.reshape(B * CTX, D), x.reshape(B * SEQ, D)], axis=0)
    cvec = jnp.concatenate([c, c_ctx[None, :], jnp.zeros((MOD_ROWS - B - 1, D), F32)], axis=0)
    mods = _ada(cvec, w_ada, b_ada).reshape(depth, MOD_ROWS, 1, N_MOD * D)
    rope = _rope_tables(CTX, SEQ)

    expand = np.zeros((2, SMALL_COLS, mix), np.float32)
    for d in range(2):
        for h in range(ssd_heads):
            expand[d, lr_w + d * ssd_heads + h, h * SSD_HEAD_DIM:(h + 1) * SSD_HEAD_DIM] = 1.0
    expand = jnp.asarray(expand, BF16)

    take = lambda name: w_in[:, :, src[name][0]:src[name][0] + src[name][1]]
    lead = ["att_q", "att_k", "att_v", "gla_q", "gla_k", "gla_v", "gla_gate"]
    assert [src[n][0] for n in lead] == list(np.cumsum([0] + [src[n][1] for n in lead[:-1]]))
    assert [cols[n][0] for n in lead[3:]] == [src[n][0] - src["gla_q"][0] for n in lead[3:]]
    w_a = w_in[:, :, :src["gla_gate"][0] + src["gla_gate"][1]].astype(BF16)
    w_b = jnp.concatenate([take("ssd_z"), take("ssd_x"), take("ssd_bc")], axis=2).astype(BF16)
    assert cols["ssd_z"][0] == w_a.shape[2] - src["gla_q"][0] and cols["ssd_x"][0] == cols["ssd_z"][0] + mix
    w_small = jnp.concatenate([take("gla_lr"), take("ssd_dt"),
                               jnp.zeros((depth, D, SMALL_COLS - lr_w - 2 * ssd_heads), F32)],
                              axis=2).astype(BF16)
    w_gate = take("gates").astype(BF16)
    w_branch_b, w_out_b = w_branch.astype(BF16), w_out.astype(BF16)
    w_ff1_b, w_ff2_b = w_ff1.astype(BF16), w_ff2.astype(BF16)

    out = None
    for l in range(depth):
        mod = mods[l]
        g1 = norm1[l][None, :]

        U, small, qt, kk, vt = _inproj(lay, X, mod, g1, w_a, w_b, w_small, att_q_norm[l][None, :],
                                       att_k_norm[l][None, :], rope, l)

        o_att = _attention(lay, qt, kk, vt)

        gla_out = None
        for d in range(2):
            wd = jnp.zeros((SMALL_COLS, mix // 2), F32).at[d * GLA_LOWRANK:(d + 1) * GLA_LOWRANK].set(
                gla_w_decay[l, d]).astype(BF16)
            bd = gla_b_decay[l, d][None, :]
            if d == 0:
                gla_out = _gla_dir(lay, U, small, cols, wd, bd, rev=False)
            else:
                gla_out = _gla_dir(lay, U, small, cols, wd, bd, rev=True,
                                   norm_g=gla_norm[l][None, :], prev=gla_out)
        o_gla = gla_out

        cw, cbias = ssd_conv_w[l], ssd_conv_b[l]
        xc, bcc = _conv(lay, U, cols, cw[:, :mix], cbias[None, :mix], cw[:, mix:], cbias[None, mix:])
        ssd_out = None
        for d in range(2):
            lanes = slice(lr_w + d * ssd_heads, lr_w + (d + 1) * ssd_heads)
            dtb = jnp.zeros((1, SMALL_COLS), F32).at[0, lanes].set(ssd_dt_bias[l, d])
            apad = jnp.zeros((1, SMALL_COLS), F32).at[0, lanes].set(-jnp.exp(ssd_a_log[l, d]))
            if d == 0:
                ssd_out = _ssd_dir(lay, xc, bcc, small, U, cols, dtb, apad, expand[d], rev=False)
            else:
                ssd_out = _ssd_dir(lay, xc, bcc, small, U, cols, dtb, apad, expand[d], rev=True,
                                   dsk=jnp.repeat(ssd_d[l], SSD_HEAD_DIM)[None, :],
                                   norm_g=ssd_norm[l][None, :], prev=ssd_out)
        o_ssd = ssd_out

        last = l == depth - 1
        skip = B * CTX if last else 0
        m = _merge(lay, X, mod, g1, o_att, o_gla, o_ssd, w_gate, w_branch_b, l, skip)
        mlp_args = (lay, X, m, w_out_b, mod, norm2[l][None, :], w_ff1_b, w_ff2_b, l, skip)
        if last:
            out = _mlp(*mlp_args, final_gain=final_norm[None, :])
        else:
            X = _mlp(*mlp_args)
    return out.reshape(B, SEQ, D)
```

```python
import functools
import math

import jax
import jax.numpy as jnp
import numpy as np
from jax import lax
from jax.experimental import pallas as pl
from jax.experimental.pallas import tpu as pltpu

F32 = jnp.float32
BF16 = jnp.bfloat16

EPS = 1e-6
GRID_W = 64
N_BRANCH = 3
N_MOD = 6
ATT_HEAD_DIM = 128
ATT_KV_HEADS = 2
ROPE_THETA = 10000.0
GLA_HEADS = 4
GLA_LOWRANK = 16
GLA_TAU = 16.0
GLA_CHUNK = 64
SSD_HEAD_DIM = 64
SSD_GROUPS = 2
SSD_STATE = 128
SSD_CONV = 5
SSD_CHUNK = 64

SEQ_BLOCK = 256
RESIDENT_TM = 256
MOD_ROWS = 8
SMALL_COLS = 128
HALO_ROWS = 16
ATT_ONES_ROWS = 16
ATT_PAIR_UNROLL = 4
VMEM_LIMIT = 56 * 1024 * 1024


def _cparams(sem):
    return pltpu.CompilerParams(dimension_semantics=sem, vmem_limit_bytes=VMEM_LIMIT)


def _dot(a, b):
    return jnp.dot(a, b, preferred_element_type=F32)


def _dot_nt(a, b):
    return lax.dot_general(a, b, (((1,), (1,)), ((), ())), preferred_element_type=F32)


def _dot_tn(a, b):
    return lax.dot_general(a, b, (((0,), (0,)), ((), ())), preferred_element_type=F32)


def _split3(x):
    hi = x.astype(BF16)
    r = x - hi.astype(F32)
    mid = r.astype(BF16)
    lo = (r - mid.astype(F32)).astype(BF16)
    return hi, mid, lo


def _dot_exact_lhs(a, x):
    hi, mid, lo = _split3(x)
    return _dot(a, lo) + _dot(a, mid) + _dot(a, hi)


def _dot_spread(x, e):
    hi = x.astype(BF16)
    lo = (x - hi.astype(F32)).astype(BF16)
    return _dot(lo, e) + _dot(hi, e)


def _silu(x):
    return x * jax.nn.sigmoid(x)


def _softplus(x):
    return jnp.maximum(x, 0.0) + jnp.log1p(jnp.exp(-jnp.abs(x)))


def _block_tri(n, chunk, rev):
    r = lax.broadcasted_iota(jnp.int32, (n, n), 0)
    c = lax.broadcasted_iota(jnp.int32, (n, n), 1)
    shift = chunk.bit_length() - 1
    assert chunk == 1 << shift
    same = jnp.right_shift(r, shift) == jnp.right_shift(c, shift)
    tri = (c >= r) if rev else (c <= r)
    return jnp.where(same, jnp.where(tri, 1.0, 0.0), 0.0).astype(BF16)


class _Layout:
    def __init__(self, B, CTX, SEQ, D):
        assert CTX % SEQ_BLOCK == 0 and SEQ % SEQ_BLOCK == 0
        self.B, self.CTX, self.SEQ, self.D = B, CTX, SEQ, D
        self.R = B * (CTX + SEQ)
        self.NC = CTX // SEQ_BLOCK
        self.NL = SEQ // SEQ_BLOCK
        self.NS = self.NC + self.NL
        self.tm = next(t for t in (512, 256) if (B * CTX) % t == 0 and SEQ % t == 0)
        assert B + 1 <= MOD_ROWS

    def row_block(self, b, s):
        return jnp.where(s < self.NC, b * self.NC + s, self.B * self.NC + b * self.NL + (s - self.NC))

    def batch_seq(self, i):
        nctx = self.B * self.NC
        j = jnp.maximum(i - nctx, 0)
        return (jnp.where(i < nctx, i // self.NC, j // self.NL),
                jnp.where(i < nctx, i % self.NC, self.NC + j % self.NL))

    def seq_order(self, n, rev):
        if not rev:
            return n
        return jnp.where(n < self.NC, self.NC - 1 - n, self.NC + self.NL - 1 - (n - self.NC))

    def mod_row(self, i, tm):
        nctx = (self.B * self.CTX) // tm
        return jnp.where(i < nctx, self.B, (i - nctx) // (self.SEQ // tm))


def _ada_kernel(c_ref, w_ref, b_ref, o_ref):
    s = _silu(c_ref[...]).astype(BF16)
    o_ref[0] = _dot(s, w_ref[0].astype(BF16)) + b_ref[0]


def _ada(cvec, w_ada, b_ada):
    depth, D, n = w_ada.shape
    tn = 1024
    return pl.pallas_call(
        _ada_kernel,
        out_shape=jax.ShapeDtypeStruct((depth, MOD_ROWS, n), F32),
        grid=(depth, n // tn),
        in_specs=[
            pl.BlockSpec((MOD_ROWS, D), lambda l, j: (0, 0)),
            pl.BlockSpec((1, D, tn), lambda l, j: (l, 0, j)),
            pl.BlockSpec((1, 1, tn), lambda l, j: (l, 0, j)),
        ],
        out_specs=pl.BlockSpec((1, MOD_ROWS, tn), lambda l, j: (l, 0, j)),
        compiler_params=_cparams(("arbitrary", "arbitrary")),
        name="ada",
    )(cvec, w_ada, b_ada.reshape(depth, 1, n))


def _cast_split_kernel(w_ref, a_ref, r_ref):
    n_a = a_ref.shape[1]
    a_ref[...] = w_ref[:, :n_a].astype(a_ref.dtype)
    r_ref[...] = w_ref[:, n_a:].astype(r_ref.dtype)


def _cast_split(w, n_a):
    depth, rows, n = w.shape
    tr = 256
    return pl.pallas_call(
        _cast_split_kernel,
        out_shape=(jax.ShapeDtypeStruct((depth, rows, n_a), BF16),
                   jax.ShapeDtypeStruct((depth, rows, n - n_a), BF16)),
        grid=(depth, rows // tr),
        in_specs=[pl.BlockSpec((None, tr, n), lambda l, r: (l, r, 0))],
        out_specs=(pl.BlockSpec((None, tr, n_a), lambda l, r: (l, r, 0)),
                   pl.BlockSpec((None, tr, n - n_a), lambda l, r: (l, r, 0))),
        compiler_params=_cparams(("arbitrary", "arbitrary")),
        name="cast_w_in",
    )(w)


def _norm_mod(x_ref, g_ref, mod_ref, k, h_ref):
    D = x_ref.shape[-1]
    shift = mod_ref[:, k * D:(k + 1) * D]
    a = g_ref[...] * (1.0 + mod_ref[:, (k + 1) * D:(k + 2) * D])
    rc = 64

    def body(r, carry):
        sl = pl.ds(pl.multiple_of(r * rc, rc), rc)
        xf = x_ref[sl, :]
        ms = jnp.mean(xf * xf, axis=-1, keepdims=True)
        h_ref[sl, :] = (xf * lax.rsqrt(ms + EPS) * a + shift).astype(h_ref.dtype)
        return carry

    lax.fori_loop(0, x_ref.shape[0] // rc, body, 0)


def _resident(shape, layer=None):
    if layer is None:
        return pl.BlockSpec(shape, lambda *_: (0,) * len(shape), pipeline_mode=pl.Buffered(1))
    return pl.BlockSpec((None,) + tuple(shape), lambda *_: (layer,) + (0,) * len(shape),
                        pipeline_mode=pl.Buffered(1))


def _inproj_kernel(x_ref, mod_ref, g_ref, wa_ref, wb_ref, ws_ref, qg_ref, kg_ref, cos_ref, sa_ref, sb_ref,
                   u_ref, small_ref, qt_ref, ko_ref, vt_ref, h_ref, *, tn, att_q, att_kv):
    _norm_mod(x_ref, g_ref, mod_ref, 0, h_ref)
    h = h_ref[...]
    hd = ATT_HEAD_DIM
    cos, sa, sb = cos_ref[...], sa_ref[...], sb_ref[...]

    def prep(y, g):
        y = y * lax.rsqrt(jnp.mean(y * y, axis=-1, keepdims=True) + EPS) * g
        return y * cos + pltpu.roll(y, hd - hd // 4, 1) * sa + pltpu.roll(y, hd // 4, 1) * sb

    q_scale = (hd ** -0.5) * math.log2(math.e)
    for j in range(att_q // tn):
        q = _dot(h, wa_ref[:, j * tn:(j + 1) * tn])
        for hh in range(tn // hd):
            qh = prep(q[:, hh * hd:(hh + 1) * hd], qg_ref[...]) * q_scale
            qt_ref[j * (tn // hd) + hh] = qh.T.astype(qt_ref.dtype)
    kv = _dot(h, wa_ref[:, att_q:att_q + 2 * att_kv])
    for g in range(att_kv // hd):
        sl = slice(g * hd, (g + 1) * hd)
        ko_ref[:, sl] = prep(kv[:, sl], kg_ref[...]).astype(ko_ref.dtype)
        vt_ref[g, 0:hd, :] = kv[:, att_kv + g * hd:att_kv + (g + 1) * hd].T.astype(vt_ref.dtype)
        vt_ref[g, hd:, :] = jnp.ones((ATT_ONES_ROWS, x_ref.shape[0]), vt_ref.dtype)

    small_ref[...] = _dot(h, ws_ref[...])
    a0 = att_q + 2 * att_kv
    na = wa_ref.shape[1] - a0
    for j in range(na // tn):
        u_ref[:, j * tn:(j + 1) * tn] = _dot(h, wa_ref[:, a0 + j * tn:a0 + (j + 1) * tn]).astype(u_ref.dtype)
    for j in range(wb_ref.shape[1] // tn):
        u_ref[:, na + j * tn:na + (j + 1) * tn] = _dot(h, wb_ref[:, j * tn:(j + 1) * tn]).astype(u_ref.dtype)


def _inproj(lay, X, mod, g, w_a, w_b, w_small, q_gain, k_gain, rope, l):
    _, D, n_a = w_a.shape
    n_b = w_b.shape[2]
    tm = RESIDENT_TM
    assert tm == SEQ_BLOCK
    hd = ATT_HEAD_DIM
    att_q = D // 2
    att_kv = ATT_KV_HEADS * hd
    heads = att_q // hd
    n_u = n_a - att_q - 2 * att_kv + n_b
    B, NS = lay.B, lay.NS
    bs = lay.batch_seq
    tab = pl.BlockSpec((tm, hd), lambda i: (bs(i)[1], 0))
    return pl.pallas_call(
        functools.partial(_inproj_kernel, tn=512, att_q=att_q, att_kv=att_kv),
        out_shape=(jax.ShapeDtypeStruct((lay.R, n_u), BF16),
                   jax.ShapeDtypeStruct((lay.R, SMALL_COLS), F32),
                   jax.ShapeDtypeStruct((lay.R // tm, heads, hd, tm), BF16),
                   jax.ShapeDtypeStruct((B, NS * tm, att_kv), BF16),
                   jax.ShapeDtypeStruct((B, ATT_KV_HEADS, NS, hd + ATT_ONES_ROWS, tm), BF16)),
        grid=(lay.R // tm,),
        in_specs=[
            pl.BlockSpec((tm, D), lambda i: (i, 0)),
            pl.BlockSpec((None, 1, N_MOD * D), lambda i: (lay.mod_row(i, tm), 0, 0)),
            _resident((1, D)),
            _resident((D, n_a), l),
            _resident((D, n_b), l),
            _resident((D, SMALL_COLS), l),
            _resident((1, hd)), _resident((1, hd)),
            tab, tab, tab,
        ],
        out_specs=(pl.BlockSpec((tm, n_u), lambda i: (i, 0)),
                   pl.BlockSpec((tm, SMALL_COLS), lambda i: (i, 0)),
                   pl.BlockSpec((None, heads, hd, tm), lambda i: (i, 0, 0, 0)),
                   pl.BlockSpec((None, tm, att_kv), lambda i: (*bs(i), 0)),
                   pl.BlockSpec((None, ATT_KV_HEADS, None, hd + ATT_ONES_ROWS, tm),
                                lambda i: (bs(i)[0], 0, bs(i)[1], 0, 0))),
        scratch_shapes=[pltpu.VMEM((tm, D), BF16)],
        compiler_params=_cparams(("arbitrary",)),
        name="in_proj",
    )(X, mod, g, w_a, w_b, w_small, q_gain, k_gain, *rope)


def _merge_kernel(x_ref, mod_ref, g_ref, oa_ref, og_ref, os_ref, wg_ref, wb_ref, m_ref, h_ref, *, tn):
    _norm_mod(x_ref, g_ref, mod_ref, 0, h_ref)
    h = h_ref[...]
    D = m_ref.shape[1]
    for j in range(D // tn):
        acc = None
        for b, o_ref in enumerate((oa_ref, og_ref, os_ref)):
            gate = _dot(h, wg_ref[:, b * D + j * tn:b * D + (j + 1) * tn])
            t = jax.nn.sigmoid(gate) * _dot(o_ref[...], wb_ref[b, :, j * tn:(j + 1) * tn])
            acc = t if acc is None else acc + t
        m_ref[:, j * tn:(j + 1) * tn] = acc.astype(m_ref.dtype)


def _merge(lay, X, mod, g, o_att, o_gla, o_ssd, w_gate, w_branch, l, skip_rows):
    D = lay.D
    W = o_att.shape[1]
    tm = RESIDENT_TM
    assert tm == SEQ_BLOCK
    row0 = skip_rows // tm
    rows = lambda i: (i + row0, 0)
    seq_rows = pl.BlockSpec((None, tm, W), lambda i: (*lay.batch_seq(i + row0), 0))
    return pl.pallas_call(
        functools.partial(_merge_kernel, tn=512),
        out_shape=jax.ShapeDtypeStruct((lay.R, D), BF16),
        grid=(lay.R // tm - row0,),
        in_specs=[
            pl.BlockSpec((tm, D), rows),
            pl.BlockSpec((None, 1, N_MOD * D), lambda i: (lay.mod_row(i + row0, tm), 0, 0)),
            _resident((1, D)),
            pl.BlockSpec((tm, W), rows), seq_rows, seq_rows,
            _resident(w_gate.shape[1:], l), _resident(w_branch.shape[1:], l),
        ],
        out_specs=pl.BlockSpec((tm, D), rows),
        scratch_shapes=[pltpu.VMEM((tm, D), BF16)],
        compiler_params=_cparams(("arbitrary",)),
        name="merge",
    )(X, mod, g, o_att, o_gla, o_ssd, w_gate, w_branch)


def _mlp_kernel(x_ref, m_ref, wo_ref, mod_ref, g_ref, w1_ref, w2_ref, *rest, final):
    if final:
        fg_ref, o_ref, x1_ref, h_ref = rest
    else:
        o_ref, x1_ref, h_ref = rest
    k = pl.program_id(1)
    D = x_ref.shape[-1]

    @pl.when(k == 0)
    def _():
        tn = 512
        for j in range(D // tn):
            sl = slice(j * tn, (j + 1) * tn)
            x1_ref[:, sl] = x_ref[:, sl] + mod_ref[:, 2 * D + j * tn:2 * D + (j + 1) * tn] * _dot(
                m_ref[...], wo_ref[:, sl])
        _norm_mod(x1_ref, g_ref, mod_ref, 3, h_ref)
        o_ref[...] = jnp.zeros_like(o_ref)

    f = jnp.square(jnp.maximum(_dot(h_ref[...], w1_ref[...]), 0.0)).astype(BF16)
    o_ref[...] += _dot(f, w2_ref[...])

    @pl.when(k == pl.num_programs(1) - 1)
    def _():
        gate = mod_ref[:, 5 * D:6 * D]
        rc = 64

        def body(r, carry):
            sl = pl.ds(pl.multiple_of(r * rc, rc), rc)
            y = x1_ref[sl, :] + gate * o_ref[sl, :]
            if final:
                ms = jnp.mean(y * y, axis=-1, keepdims=True)
                y = y * lax.rsqrt(ms + EPS) * fg_ref[...]
            o_ref[sl, :] = y
            return carry

        lax.fori_loop(0, x_ref.shape[0] // rc, body, 0)


def _mlp(lay, X, m, w_out, mod, g, w1, w2, l, skip_rows, final_gain=None):
    _, D, FF = w1.shape
    tm, tf = lay.tm, 1024
    row0 = skip_rows // tm
    final = final_gain is not None
    out_rows = lay.R - skip_rows if final else lay.R
    out_off = 0 if final else row0
    rows = lambda i, k: (i + row0, 0)
    in_specs = [
        pl.BlockSpec((tm, D), rows),
        pl.BlockSpec((tm, D), rows),
        _resident((D, D), l),
        pl.BlockSpec((None, 1, N_MOD * D), lambda i, k: (lay.mod_row(i + row0, tm), 0, 0)),
        _resident((1, D)),
        pl.BlockSpec((None, D, tf), lambda i, k: (l, 0, k)),
        pl.BlockSpec((None, tf, D), lambda i, k: (l, k, 0)),
    ]
    args = [X, m, w_out, mod, g, w1, w2]
    if final:
        in_specs.append(_resident((1, D)))
        args.append(final_gain)
    return pl.pallas_call(
        functools.partial(_mlp_kernel, final=final),
        out_shape=jax.ShapeDtypeStruct((out_rows, D), F32),
        grid=(lay.R // tm - row0, FF // tf),
        in_specs=in_specs,
        out_specs=pl.BlockSpec((tm, D), lambda i, k: (i + out_off, 0)),
        scratch_shapes=[pltpu.VMEM((tm, D), F32), pltpu.VMEM((tm, D), BF16)],
        compiler_params=_cparams(("arbitrary", "arbitrary")),
        name="mlp",
    )(*args)


def _attn_kernel(qt_ref, k_ref, vt_ref, o_ref, m_ref, acc_ref, s0_ref, s1_ref, *, n_ctx_blocks):
    n = pl.program_id(2)
    heads, hd, tq = qt_ref.shape
    nblk, vrows, tk = vt_ref.shape
    n_kv = jnp.where(n < n_ctx_blocks, n_ctx_blocks, nblk)
    m_ref[...] = jnp.full(m_ref.shape, -1e30, F32)
    acc_ref[...] = jnp.zeros(acc_ref.shape, F32)

    def scores(c, s_ref):
        c = jnp.minimum(c, n_kv - 1)
        k = k_ref[pl.ds(pl.multiple_of(c * tk, tk), tk), :]
        for h in range(heads):
            s_ref[h] = _dot(k, qt_ref[h])

    def update(c, s_ref):
        vt = vt_ref[c]
        for h in range(heads):
            st = s_ref[h]
            m_old = m_ref[h]
            m_new = jnp.maximum(m_old, jnp.max(st, axis=0, keepdims=True))
            pt = jnp.exp2((st - m_new).astype(BF16))
            acc_ref[h] = jnp.exp2(m_old - m_new) * acc_ref[h] + _dot(vt, pt)
            m_ref[h] = m_new

    scores(0, s0_ref)

    def pairs(c0, n_pairs):
        for j in range(n_pairs):
            scores(c0 + 2 * j + 1, s1_ref)
            update(c0 + 2 * j, s0_ref)
            scores(c0 + 2 * j + 2, s0_ref)
            update(c0 + 2 * j + 1, s1_ref)

    n_pairs = n_kv // 2
    unroll = ATT_PAIR_UNROLL

    def body(i, carry):
        pairs(2 * unroll * i, unroll)
        return carry

    lax.fori_loop(0, n_pairs // unroll, body, 0)
    done = (n_pairs // unroll) * unroll
    piece = unroll // 2
    while piece >= 1:
        @pl.when(((n_pairs - done) & piece) != 0)
        def _(done=done, piece=piece):
            pairs(2 * done, piece)

        done = done + ((n_pairs - done) & piece)
        piece //= 2

    @pl.when(n_kv % 2 == 1)
    def _():
        update(n_kv - 1, s0_ref)

    for h in range(heads):
        acc = acc_ref[h]
        ot = acc[0:hd] / acc[hd:hd + 1]
        o_ref[:, h * hd:(h + 1) * hd] = ot.T.astype(o_ref.dtype)


def _attention(lay, qt, kk, vt):
    B, kvh, NS, vrows, blk = vt.shape
    _, heads, hd, _ = qt.shape
    group = heads // kvh
    return pl.pallas_call(
        functools.partial(_attn_kernel, n_ctx_blocks=lay.NC),
        out_shape=jax.ShapeDtypeStruct((lay.R, heads * hd), BF16),
        grid=(B, kvh, NS),
        in_specs=[
            pl.BlockSpec((None, group, hd, blk), lambda b, g, n: (lay.row_block(b, n), g, 0, 0)),
            pl.BlockSpec((None, NS * blk, hd), lambda b, g, n: (b, 0, g)),
            pl.BlockSpec((None, None, NS, vrows, blk), lambda b, g, n: (b, g, 0, 0, 0)),
        ],
        out_specs=pl.BlockSpec((blk, group * hd), lambda b, g, n: (lay.row_block(b, n), g)),
        scratch_shapes=[pltpu.VMEM((group, 1, blk), F32), pltpu.VMEM((group, vrows, blk), F32),
                        pltpu.VMEM((group, blk, blk), F32), pltpu.VMEM((group, blk, blk), F32)],
        compiler_params=_cparams(("arbitrary", "arbitrary", "arbitrary")),
        name="attention",
    )(qt, kk, vt)


def _gla_kernel(*refs, nb, rev, final, dk, dv):
    per_b = 5 if final else 4
    seqs = [refs[i * per_b:(i + 1) * per_b] for i in range(nb)]
    rest = refs[nb * per_b:]
    if final:
        wd_ref, bd_ref, ng_ref, prev_ref, o_ref, st_ref = rest
    else:
        wd_ref, bd_ref, o_ref, st_ref = rest
    blk = seqs[0][0].shape[0]
    heads = seqs[0][0].shape[1] // dk
    C = GLA_CHUNK

    @pl.when(pl.program_id(0) == 0)
    def _():
        st_ref[...] = jnp.zeros_like(st_ref)

    tri = _block_tri(blk, C, rev)
    bcums = []
    for bi in range(nb):
        z = _dot(seqs[bi][3][...].astype(BF16), wd_ref[...]) + bd_ref[...]
        la = (jnp.minimum(z, 0.0) - jnp.log1p(jnp.exp(-jnp.abs(z)))) * (1.0 / GLA_TAU)
        bcums.append(_dot_exact_lhs(tri, la))

    ri = lax.broadcasted_iota(jnp.int32, (C, C), 0)
    ci = lax.broadcasted_iota(jnp.int32, (C, C), 1)
    causal = (ci >= ri) if rev else (ci <= ri)

    n_chunks = blk // C
    for step in range(n_chunks):
        c = n_chunks - 1 - step if rev else step
        rs = slice(c * C, (c + 1) * C)
        last = c * C if rev else (c + 1) * C - 1
        units = [(h, bi) for h in range(heads) for bi in range(nb)]
        scores, inter = [], []
        for h, bi in units:
            ks = slice(h * dk, (h + 1) * dk)
            q_ref, k_ref, v_ref = seqs[bi][:3]
            b = bcums[bi][rs, ks]
            b_last = bcums[bi][last:last + 1, ks]
            q = q_ref[rs, ks].astype(F32) * (dk ** -0.5)
            k = k_ref[rs, ks].astype(F32)
            q_dec = (q * jnp.exp(b)).astype(BF16)
            k_dec = (k * jnp.exp(-b)).astype(BF16)
            k_st = (k * jnp.exp(b_last - b)).astype(BF16)
            st = st_ref[bi * heads + h]
            scores.append(_dot_nt(q_dec, k_dec))
            inter.append(_dot_nt(q_dec, st.astype(BF16)))
            st_ref[bi * heads + h] = st * jnp.exp(b_last) + _dot_tn(v_ref[rs, h * dv:(h + 1) * dv], k_st)
        for (h, bi), sc, it in zip(units, scores, inter):
            vs = slice(h * dv, (h + 1) * dv)
            att = jnp.where(causal, sc, 0.0).astype(BF16)
            o = _dot(att, seqs[bi][2][rs, vs]) + it
            if final:
                tot = o + prev_ref[bi, rs, vs]
                y = tot * lax.rsqrt(jnp.mean(tot * tot, axis=-1, keepdims=True) + EPS) * ng_ref[...]
                o_ref[bi, rs, vs] = (y * _silu(seqs[bi][4][rs, vs].astype(F32))).astype(o_ref.dtype)
            else:
                o_ref[bi, rs, vs] = o


def _gla_dir(lay, U, small, cols, w_dec, b_dec, rev, norm_g=None, prev=None):
    B, NS, blk = lay.B, lay.NS, SEQ_BLOCK
    qk_w = cols["gla_q"][1]
    v_w = cols["gla_v"][1]
    dk, dv = qk_w // GLA_HEADS, v_w // GLA_HEADS
    final = prev is not None
    const = lambda shape: pl.BlockSpec(shape, lambda n: (0,) * len(shape))
    seq = lambda n: lay.seq_order(n, rev)

    def ucol(name, b):
        off, w = cols[name]
        return pl.BlockSpec((blk, w), lambda n: (lay.row_block(b, seq(n)), off // w))

    in_specs, args = [], []
    for b in range(B):
        in_specs += [ucol("gla_q", b), ucol("gla_k", b), ucol("gla_v", b),
                     pl.BlockSpec((blk, SMALL_COLS), lambda n, b=b: (lay.row_block(b, seq(n)), 0))]
        args += [U, U, U, small]
        if final:
            in_specs.append(ucol("gla_gate", b))
            args.append(U)
    in_specs += [const((SMALL_COLS, qk_w)), const((1, qk_w))]
    args += [w_dec, b_dec]
    per_batch = pl.BlockSpec((B, blk, v_w), lambda n: (0, seq(n), 0))
    if final:
        in_specs += [const((1, dv)), per_batch]
        args += [norm_g, prev]
    return pl.pallas_call(
        functools.partial(_gla_kernel, nb=B, rev=rev, final=final, dk=dk, dv=dv),
        out_shape=jax.ShapeDtypeStruct((B, NS * blk, v_w), BF16 if final else F32),
        grid=(NS,),
        in_specs=in_specs,
        out_specs=per_batch,
        scratch_shapes=[pltpu.VMEM((B * GLA_HEADS, dv, dk), F32)],
        compiler_params=_cparams(("arbitrary",)),
        name="gla_bwd" if rev else "gla_fwd",
    )(*args)


def _conv_kernel(x_ref, xp_ref, xn_ref, bc_ref, bcp_ref, bcn_ref, wx_ref, bx_ref, wbc_ref, bbc_ref,
                 xo_ref, bco_ref, *, n_ctx_total, NC, NL):
    i = pl.program_id(0)
    in_ctx = i < n_ctx_total
    p = jnp.where(in_ctx, i % NC, jnp.maximum(i - n_ctx_total, 0) % NL)
    seg = jnp.where(in_ctx, NC, NL)
    has_left = (p > 0).astype(F32)
    has_right = (p < seg - 1).astype(F32)
    blk = x_ref.shape[0]
    pad = SSD_CONV // 2
    edge = 8
    assert pad <= edge <= HALO_ROWS

    r = lax.broadcasted_iota(jnp.int32, (blk, blk), 0)
    c = lax.broadcasted_iota(jnp.int32, (blk, blk), 1)
    re = lax.broadcasted_iota(jnp.int32, (edge, HALO_ROWS), 0)
    ce = lax.broadcasted_iota(jnp.int32, (edge, HALO_ROWS), 1)
    taps = [j - pad for j in range(SSD_CONV)]
    shift = {s: jnp.where(c == r + s, 1.0, 0.0).astype(BF16) for s in taps if s != 0}
    head = {s: (jnp.where(ce == HALO_ROWS + re + s, 1.0, 0.0) * has_left).astype(BF16) for s in taps if s < 0}
    tail = {s: (jnp.where(ce == re + s - edge, 1.0, 0.0) * has_right).astype(BF16) for s in taps if s > 0}

    def run(m_ref, p_ref, n_ref, w_ref, b_ref, o_ref):
        main = m_ref[...]
        acc = b_ref[...] + w_ref[pad:pad + 1, :] * main.astype(F32)
        for j, s in enumerate(taps):
            if s != 0:
                acc = acc + w_ref[j:j + 1, :] * _dot(shift[s], main)
        first, last = acc[0:edge], acc[blk - edge:blk]
        for j, s in enumerate(taps):
            if s < 0:
                first = first + w_ref[j:j + 1, :] * _dot(head[s], p_ref[...])
            if s > 0:
                last = last + w_ref[j:j + 1, :] * _dot(tail[s], n_ref[...])
        o_ref[0:edge, :] = _silu(first).astype(o_ref.dtype)
        o_ref[edge:blk - edge, :] = _silu(acc[edge:blk - edge]).astype(o_ref.dtype)
        o_ref[blk - edge:blk, :] = _silu(last).astype(o_ref.dtype)

    run(x_ref, xp_ref, xn_ref, wx_ref, bx_ref, xo_ref)
    run(bc_ref, bcp_ref, bcn_ref, wbc_ref, bbc_ref, bco_ref)


def _conv(lay, U, cols, wx, bx, wbc, bbc):
    blk = SEQ_BLOCK
    nb = lay.R // blk
    hb = blk // HALO_ROWS
    nh = lay.R // HALO_ROWS
    xw = cols["ssd_x"][1]
    bcw = cols["ssd_bc"][1]
    xc, bcc = cols["ssd_x"][0] // xw, cols["ssd_bc"][0] // bcw
    prev = lambda i: jnp.maximum(i * hb - 1, 0)
    nxt = lambda i: jnp.minimum((i + 1) * hb, nh - 1)
    const = lambda shape: pl.BlockSpec(shape, lambda i: (0,) * len(shape))
    return pl.pallas_call(
        functools.partial(_conv_kernel, n_ctx_total=lay.B * lay.NC, NC=lay.NC, NL=lay.NL),
        out_shape=(jax.ShapeDtypeStruct((lay.R, xw), BF16), jax.ShapeDtypeStruct((lay.R, bcw), BF16)),
        grid=(nb,),
        in_specs=[
            pl.BlockSpec((blk, xw), lambda i: (i, xc)),
            pl.BlockSpec((HALO_ROWS, xw), lambda i: (prev(i), xc)),
            pl.BlockSpec((HALO_ROWS, xw), lambda i: (nxt(i), xc)),
            pl.BlockSpec((blk, bcw), lambda i: (i, bcc)),
            pl.BlockSpec((HALO_ROWS, bcw), lambda i: (prev(i), bcc)),
            pl.BlockSpec((HALO_ROWS, bcw), lambda i: (nxt(i), bcc)),
            const((SSD_CONV, xw)), const((1, xw)), const((SSD_CONV, bcw)), const((1, bcw)),
        ],
        out_specs=(pl.BlockSpec((blk, xw), lambda i: (i, 0)), pl.BlockSpec((blk, bcw), lambda i: (i, 0))),
        compiler_params=_cparams(("arbitrary",)),
        name="ssd_conv",
    )(U, U, U, U, U, U, wx, bx, wbc, bbc)


def _ssd_kernel(*refs, nb, rev, final):
    per_b = 4 if final else 3
    seqs = [refs[i * per_b:(i + 1) * per_b] for i in range(nb)]
    rest = refs[nb * per_b:]
    if final:
        dtb_ref, apad_ref, e_ref, dsk_ref, ng_ref, prev_ref, o_ref, st_ref, y_ref = rest
    else:
        dtb_ref, apad_ref, e_ref, o_ref, st_ref = rest
        y_ref = o_ref
    blk, inner = seqs[0][0].shape
    N = SSD_STATE
    P = SSD_HEAD_DIM
    C = SSD_CHUNK
    gw = inner // SSD_GROUPS
    pair = 2 * P

    @pl.when(pl.program_id(0) == 0)
    def _():
        st_ref[...] = jnp.zeros_like(st_ref)

    tri = _block_tri(blk, C, rev)
    e = e_ref[...]
    dt_fulls, cs_fulls = [], []
    for bi in range(nb):
        dt16 = _softplus(seqs[bi][2][...] + dtb_ref[...])
        cs16 = _dot_exact_lhs(tri, dt16 * apad_ref[...])
        dt_fulls.append(_dot_spread(dt16, e))
        cs_fulls.append(_dot_spread(cs16, e))

    ri = lax.broadcasted_iota(jnp.int32, (C, pair), 0)
    ci = lax.broadcasted_iota(jnp.int32, (C, pair), 1)
    cm = jnp.where(ci >= P, ci - P, ci)
    eye2 = cm == ri
    causal2 = (cm >= ri) if rev else (cm <= ri)
    left = ci < P
    zero = jnp.zeros((C, pair), BF16)

    n_chunks = blk // C
    for step in range(n_chunks):
        c = n_chunks - 1 - step if rev else step
        rs = slice(c * C, (c + 1) * C)
        last = c * C if rev else (c + 1) * C - 1
        units = [(g, bi) for g in range(SSD_GROUPS) for bi in range(nb)]
        held = []
        for g, bi in units:
            gs = slice(g * gw, (g + 1) * gw)
            x_ref, bc_ref = seqs[bi][:2]
            bm = bc_ref[rs, g * N:(g + 1) * N]
            cmat = bc_ref[rs, SSD_GROUPS * N + g * N:SSD_GROUPS * N + (g + 1) * N]
            cb2 = _dot_nt(cmat, jnp.concatenate([bm, bm], axis=0))
            cs_g = cs_fulls[bi][rs, gs]
            cs_last = cs_fulls[bi][last:last + 1, gs]
            xdt = x_ref[rs, gs].astype(F32) * dt_fulls[bi][rs, gs]
            st = st_ref[bi * SSD_GROUPS + g]
            y_inter = _dot(cmat, st.astype(BF16)) * jnp.exp(cs_g)
            w = (xdt * jnp.exp(cs_last - cs_g)).astype(BF16)
            st_ref[bi * SSD_GROUPS + g] = st * jnp.exp(cs_last) + _dot_tn(bm, w)
            held.append((cb2, cs_g, y_inter, xdt.astype(BF16)))
        for (g, bi), (cb2, cs_g, y_inter, xdt_b) in zip(units, held):
            x_ref = seqs[bi][0]
            for pr in range(gw // pair):
                ps = slice(pr * pair, (pr + 1) * pair)
                col = cs_g[:, ps]
                rowv = jnp.sum(jnp.where(eye2, col, 0.0), axis=0, keepdims=True)
                lmat = jnp.exp(jnp.where(causal2, col - rowv, -jnp.inf))
                mm = (lmat * cb2).astype(BF16)
                xp = xdt_b[:, ps]
                rhs = jnp.concatenate([jnp.where(left, xp, zero), jnp.where(left, zero, xp)], axis=0)
                y = _dot(mm, rhs) + y_inter[:, ps]
                os_ = slice(g * gw + pr * pair, g * gw + (pr + 1) * pair)
                if final:
                    y_ref[bi, rs, os_] = (y + prev_ref[bi, rs, os_]
                                          + x_ref[rs, os_].astype(F32) * dsk_ref[:, os_])
                else:
                    y_ref[bi, rs, os_] = y

    if final:
        rc = 64
        for bi in range(nb):
            z_ref = seqs[bi][3]
            for r in range(blk // rc):
                rs = slice(r * rc, (r + 1) * rc)
                for g in range(SSD_GROUPS):
                    gs = slice(g * gw, (g + 1) * gw)
                    yg = y_ref[bi, rs, gs] * _silu(z_ref[rs, gs].astype(F32))
                    ms = jnp.mean(yg * yg, axis=-1, keepdims=True)
                    o_ref[bi, rs, gs] = (yg * lax.rsqrt(ms + EPS) * ng_ref[:, gs]).astype(o_ref.dtype)


def _ssd_dir(lay, xc, bcc, small, U, cols, dtb, apad, expand, rev, dsk=None, norm_g=None, prev=None):
    B, NS, blk = lay.B, lay.NS, SEQ_BLOCK
    inner = xc.shape[1]
    bcw = bcc.shape[1]
    final = prev is not None
    seq = lambda n: lay.seq_order(n, rev)
    const = lambda shape: pl.BlockSpec(shape, lambda n: (0,) * len(shape))

    def rows(w, b, col=0):
        return pl.BlockSpec((blk, w), lambda n: (lay.row_block(b, seq(n)), col))

    in_specs, args = [], []
    for b in range(B):
        in_specs += [rows(inner, b), rows(bcw, b), rows(SMALL_COLS, b)]
        args += [xc, bcc, small]
        if final:
            zc = cols["ssd_z"]
            in_specs.append(rows(inner, b, zc[0] // zc[1]))
            args.append(U)
    in_specs += [const((1, SMALL_COLS)), const((1, SMALL_COLS)), const((SMALL_COLS, inner))]
    args += [dtb, apad, expand]
    per_batch = pl.BlockSpec((B, blk, inner), lambda n: (0, seq(n), 0))
    if final:
        in_specs += [const((1, inner)), const((1, inner)), per_batch]
        args += [dsk, norm_g, prev]
    return pl.pallas_call(
        functools.partial(_ssd_kernel, nb=B, rev=rev, final=final),
        out_shape=jax.ShapeDtypeStruct((B, NS * blk, inner), BF16 if final else F32),
        grid=(NS,),
        in_specs=in_specs,
        out_specs=per_batch,
        scratch_shapes=[pltpu.VMEM((B * SSD_GROUPS, SSD_STATE, inner // SSD_GROUPS), F32)]
        + ([pltpu.VMEM((B, blk, inner), F32)] if final else []),
        compiler_params=_cparams(("arbitrary",)),
        name="ssd_bwd" if rev else "ssd_fwd",
    )(*args)


def _rope_tables(CTX, SEQ):
    hd = ATT_HEAD_DIM
    pairs = hd // 4
    rows = SEQ // GRID_W
    row = jnp.repeat(jnp.arange(rows, dtype=F32), GRID_W)
    col = jnp.tile(jnp.arange(GRID_W, dtype=F32), rows)
    inv = jnp.exp(-math.log(ROPE_THETA) * jnp.arange(pairs, dtype=F32) / pairs)
    ar = row[:, None] * inv
    ac = col[:, None] * inv
    ang = jnp.concatenate([ar, ar, ac, ac], axis=-1)
    cos, sin = jnp.cos(ang), jnp.sin(ang)
    first = (jnp.arange(hd) % (2 * pairs)) < pairs
    sa = jnp.where(first, -sin, 0.0)
    sb = jnp.where(first, 0.0, sin)
    ones = jnp.ones((CTX, hd), F32)
    zeros = jnp.zeros((CTX, hd), F32)
    return (jnp.concatenate([ones, cos]), jnp.concatenate([zeros, sa]), jnp.concatenate([zeros, sb]))


def _column_plan(D):
    mix = D // 2
    gla_v = mix
    gla_qk = mix // 2
    ssd_inner = mix
    ssd_bc = 2 * SSD_GROUPS * SSD_STATE
    order = [("gla_q", gla_qk), ("gla_k", gla_qk), ("gla_v", gla_v), ("gla_gate", gla_v),
             ("ssd_z", ssd_inner), ("ssd_x", ssd_inner), ("ssd_bc", ssd_bc)]
    cols, off = {}, 0
    for name, w in order:
        assert off % w == 0
        cols[name] = (off, w)
        off += w
    return cols, off


def _source_columns(D):
    mix = D // 2
    att_kv = ATT_KV_HEADS * ATT_HEAD_DIM
    ssd_bc = SSD_GROUPS * SSD_STATE
    heads = mix // SSD_HEAD_DIM
    widths = [("att_q", mix), ("att_k", att_kv), ("att_v", att_kv), ("gla_q", mix // 2), ("gla_k", mix // 2),
              ("gla_v", mix), ("gla_gate", mix), ("gla_lr", 2 * GLA_LOWRANK), ("ssd_z", mix),
              ("ssd_x", mix), ("ssd_bc", 2 * ssd_bc), ("ssd_dt", 2 * heads), ("gates", N_BRANCH * D)]
    src, off = {}, 0
    for name, w in widths:
        src[name] = (off, w)
        off += w
    return src, off


def kernel(x, c, ctx, c_ctx, norm1, norm2, w_ada, b_ada, w_in, att_q_norm, att_k_norm, gla_w_decay,
           gla_b_decay, gla_norm, ssd_conv_w, ssd_conv_b, ssd_dt_bias, ssd_a_log, ssd_d, ssd_norm,
           w_branch, w_out, w_ff1, w_ff2, final_norm):
    B, SEQ, D = x.shape
    CTX = ctx.shape[1]
    depth = w_in.shape[0]
    lay = _Layout(B, CTX, SEQ, D)
    cols, n_main = _column_plan(D)
    src, n_src = _source_columns(D)
    assert n_src == w_in.shape[2]
    mix = D // 2
    ssd_heads = mix // SSD_HEAD_DIM
    lr_w = 2 * GLA_LOWRANK
    assert lr_w + 2 * ssd_heads <= SMALL_COLS

    X = jnp.concatenate([ctx.reshape(B * CTX, D), x.reshape(B * SEQ, D)], axis=0)
    cvec = jnp.concatenate([c, c_ctx[None, :], jnp.zeros((MOD_ROWS - B - 1, D), F32)], axis=0)
    mods = _ada(cvec, w_ada, b_ada).reshape(depth, MOD_ROWS, 1, N_MOD * D)
    rope = _rope_tables(CTX, SEQ)

    expand = np.zeros((2, SMALL_COLS, mix), np.float32)
    for d in range(2):
        for h in range(ssd_heads):
            expand[d, lr_w + d * ssd_heads + h, h * SSD_HEAD_DIM:(h + 1) * SSD_HEAD_DIM] = 1.0
    expand = jnp.asarray(expand, BF16)

    lead = ["att_q", "att_k", "att_v", "gla_q", "gla_k", "gla_v", "gla_gate"]
    assert [src[n][0] for n in lead] == list(np.cumsum([0] + [src[n][1] for n in lead[:-1]]))
    assert [cols[n][0] for n in lead[3:]] == [src[n][0] - src["gla_q"][0] for n in lead[3:]]
    n_a = src["gla_gate"][0] + src["gla_gate"][1]
    w_a, w_rest = _cast_split(w_in, n_a)
    take = lambda name: w_rest[:, :, src[name][0] - n_a:src[name][0] - n_a + src[name][1]]
    w_b = jnp.concatenate([take("ssd_z"), take("ssd_x"), take("ssd_bc")], axis=2)
    assert cols["ssd_z"][0] == n_a - src["gla_q"][0] and cols["ssd_x"][0] == cols["ssd_z"][0] + mix
    w_small = jnp.concatenate([take("gla_lr"), take("ssd_dt"),
                               jnp.zeros((depth, D, SMALL_COLS - lr_w - 2 * ssd_heads), BF16)], axis=2)
    w_gate = take("gates")
    w_branch_b, w_out_b = w_branch.astype(BF16), w_out.astype(BF16)
    w_ff1_b, w_ff2_b = w_ff1.astype(BF16), w_ff2.astype(BF16)

    out = None
    for l in range(depth):
        mod = mods[l]
        g1 = norm1[l][None, :]

        U, small, qt, kk, vt = _inproj(lay, X, mod, g1, w_a, w_b, w_small, att_q_norm[l][None, :],
                                       att_k_norm[l][None, :], rope, l)

        o_att = _attention(lay, qt, kk, vt)

        gla_out = None
        for d in range(2):
            wd = jnp.zeros((SMALL_COLS, mix // 2), F32).at[d * GLA_LOWRANK:(d + 1) * GLA_LOWRANK].set(
                gla_w_decay[l, d]).astype(BF16)
            bd = gla_b_decay[l, d][None, :]
            if d == 0:
                gla_out = _gla_dir(lay, U, small, cols, wd, bd, rev=False)
            else:
                gla_out = _gla_dir(lay, U, small, cols, wd, bd, rev=True,
                                   norm_g=gla_norm[l][None, :], prev=gla_out)
        o_gla = gla_out

        cw, cbias = ssd_conv_w[l], ssd_conv_b[l]
        xc, bcc = _conv(lay, U, cols, cw[:, :mix], cbias[None, :mix], cw[:, mix:], cbias[None, mix:])
        ssd_out = None
        for d in range(2):
            lanes = slice(lr_w + d * ssd_heads, lr_w + (d + 1) * ssd_heads)
            dtb = jnp.zeros((1, SMALL_COLS), F32).at[0, lanes].set(ssd_dt_bias[l, d])
            apad = jnp.zeros((1, SMALL_COLS), F32).at[0, lanes].set(-jnp.exp(ssd_a_log[l, d]))
            if d == 0:
                ssd_out = _ssd_dir(lay, xc, bcc, small, U, cols, dtb, apad, expand[d], rev=False)
            else:
                ssd_out = _ssd_dir(lay, xc, bcc, small, U, cols, dtb, apad, expand[d], rev=True,
                                   dsk=jnp.repeat(ssd_d[l], SSD_HEAD_DIM)[None, :],
                                   norm_g=ssd_norm[l][None, :], prev=ssd_out)
        o_ssd = ssd_out

        last = l == depth - 1
        skip = B * CTX if last else 0
        m = _merge(lay, X, mod, g1, o_att, o_gla, o_ssd, w_gate, w_branch_b, l, skip)
        mlp_args = (lay, X, m, w_out_b, mod, norm2[l][None, :], w_ff1_b, w_ff2_b, l, skip)
        if last:
            out = _mlp(*mlp_args, final_gain=final_norm[None, :])
        else:
            X = _mlp(*mlp_args)
    return out.reshape(B, SEQ, D)
```

```python
import functools
import math

import jax
import jax.numpy as jnp
import numpy as np
from jax import lax
from jax.experimental import pallas as pl
from jax.experimental.pallas import tpu as pltpu

F32 = jnp.float32
BF16 = jnp.bfloat16

EPS = 1e-6
GRID_W = 64
N_BRANCH = 3
N_MOD = 6
ATT_HEAD_DIM = 128
ATT_KV_HEADS = 2
ROPE_THETA = 10000.0
GLA_HEADS = 4
GLA_LOWRANK = 16
GLA_TAU = 16.0
GLA_CHUNK = 64
SSD_HEAD_DIM = 64
SSD_GROUPS = 2
SSD_STATE = 128
SSD_CONV = 5
SSD_CHUNK = 64

SEQ_BLOCK = 256
RESIDENT_TM = 256
MOD_ROWS = 8
SMALL_COLS = 128
HALO_ROWS = 16
ATT_ONES_ROWS = 16
ATT_PAIR_UNROLL = 4
VMEM_LIMIT = 56 * 1024 * 1024


def _cparams(sem):
    return pltpu.CompilerParams(dimension_semantics=sem, vmem_limit_bytes=VMEM_LIMIT)


def _dot(a, b):
    return jnp.dot(a, b, preferred_element_type=F32)


def _dot_nt(a, b):
    return lax.dot_general(a, b, (((1,), (1,)), ((), ())), preferred_element_type=F32)


def _dot_tn(a, b):
    return lax.dot_general(a, b, (((0,), (0,)), ((), ())), preferred_element_type=F32)


def _split3(x):
    hi = x.astype(BF16)
    r = x - hi.astype(F32)
    mid = r.astype(BF16)
    lo = (r - mid.astype(F32)).astype(BF16)
    return hi, mid, lo


def _dot_exact_lhs(a, x):
    hi, mid, lo = _split3(x)
    return _dot(a, lo) + _dot(a, mid) + _dot(a, hi)


def _dot_spread(x, e):
    hi = x.astype(BF16)
    lo = (x - hi.astype(F32)).astype(BF16)
    return _dot(lo, e) + _dot(hi, e)


def _silu(x):
    return x * jax.nn.sigmoid(x)


def _softplus(x):
    return jnp.maximum(x, 0.0) + jnp.log1p(jnp.exp(-jnp.abs(x)))


def _block_tri(n, chunk, rev):
    r = lax.broadcasted_iota(jnp.int32, (n, n), 0)
    c = lax.broadcasted_iota(jnp.int32, (n, n), 1)
    shift = chunk.bit_length() - 1
    assert chunk == 1 << shift
    same = jnp.right_shift(r, shift) == jnp.right_shift(c, shift)
    tri = (c >= r) if rev else (c <= r)
    return jnp.where(same, jnp.where(tri, 1.0, 0.0), 0.0).astype(BF16)


class _Layout:
    def __init__(self, B, CTX, SEQ, D):
        assert CTX % SEQ_BLOCK == 0 and SEQ % SEQ_BLOCK == 0
        self.B, self.CTX, self.SEQ, self.D = B, CTX, SEQ, D
        self.R = B * (CTX + SEQ)
        self.NC = CTX // SEQ_BLOCK
        self.NL = SEQ // SEQ_BLOCK
        self.NS = self.NC + self.NL
        self.tm = next(t for t in (512, 256) if (B * CTX) % t == 0 and SEQ % t == 0)
        assert B + 1 <= MOD_ROWS

    def row_block(self, b, s):
        return jnp.where(s < self.NC, b * self.NC + s, self.B * self.NC + b * self.NL + (s - self.NC))

    def batch_seq(self, i):
        nctx = self.B * self.NC
        j = jnp.maximum(i - nctx, 0)
        return (jnp.where(i < nctx, i // self.NC, j // self.NL),
                jnp.where(i < nctx, i % self.NC, self.NC + j % self.NL))

    def seq_order(self, n, rev):
        if not rev:
            return n
        return jnp.where(n < self.NC, self.NC - 1 - n, self.NC + self.NL - 1 - (n - self.NC))

    def mod_row(self, i, tm):
        nctx = (self.B * self.CTX) // tm
        return jnp.where(i < nctx, self.B, (i - nctx) // (self.SEQ // tm))


def _ada_kernel(c_ref, w_ref, b_ref, o_ref):
    s = _silu(c_ref[...]).astype(BF16)
    o_ref[0] = _dot(s, w_ref[0].astype(BF16)) + b_ref[0]


def _ada(cvec, w_ada, b_ada):
    depth, D, n = w_ada.shape
    tn = 1024
    return pl.pallas_call(
        _ada_kernel,
        out_shape=jax.ShapeDtypeStruct((depth, MOD_ROWS, n), F32),
        grid=(depth, n // tn),
        in_specs=[
            pl.BlockSpec((MOD_ROWS, D), lambda l, j: (0, 0)),
            pl.BlockSpec((1, D, tn), lambda l, j: (l, 0, j)),
            pl.BlockSpec((1, 1, tn), lambda l, j: (l, 0, j)),
        ],
        out_specs=pl.BlockSpec((1, MOD_ROWS, tn), lambda l, j: (l, 0, j)),
        compiler_params=_cparams(("arbitrary", "arbitrary")),
        name="ada",
    )(cvec, w_ada, b_ada.reshape(depth, 1, n))


def _norm_mod(x_ref, g_ref, mod_ref, k, h_ref):
    D = x_ref.shape[-1]
    shift = mod_ref[:, k * D:(k + 1) * D]
    a = g_ref[...] * (1.0 + mod_ref[:, (k + 1) * D:(k + 2) * D])
    rc = 64

    def body(r, carry):
        sl = pl.ds(pl.multiple_of(r * rc, rc), rc)
        xf = x_ref[sl, :]
        ms = jnp.mean(xf * xf, axis=-1, keepdims=True)
        h_ref[sl, :] = (xf * lax.rsqrt(ms + EPS) * a + shift).astype(h_ref.dtype)
        return carry

    lax.fori_loop(0, x_ref.shape[0] // rc, body, 0)


def _resident(shape, layer=None):
    if layer is None:
        return pl.BlockSpec(shape, lambda *_: (0,) * len(shape), pipeline_mode=pl.Buffered(1))
    return pl.BlockSpec((None,) + tuple(shape), lambda *_: (layer,) + (0,) * len(shape),
                        pipeline_mode=pl.Buffered(1))


def _inproj_kernel(x_ref, mod_ref, g_ref, wa_ref, wb_ref, ws_ref, qg_ref, kg_ref, cos_ref, sa_ref, sb_ref,
                   u_ref, small_ref, qt_ref, ko_ref, vt_ref, h_ref, *, tn, att_q, att_kv):
    _norm_mod(x_ref, g_ref, mod_ref, 0, h_ref)
    h = h_ref[...]
    hd = ATT_HEAD_DIM
    cos, sa, sb = cos_ref[...], sa_ref[...], sb_ref[...]

    def prep(y, g):
        y = y * lax.rsqrt(jnp.mean(y * y, axis=-1, keepdims=True) + EPS) * g
        return y * cos + pltpu.roll(y, hd - hd // 4, 1) * sa + pltpu.roll(y, hd // 4, 1) * sb

    q_scale = (hd ** -0.5) * math.log2(math.e)
    for j in range(att_q // tn):
        q = _dot(h, wa_ref[:, j * tn:(j + 1) * tn])
        for hh in range(tn // hd):
            qh = prep(q[:, hh * hd:(hh + 1) * hd], qg_ref[...]) * q_scale
            qt_ref[j * (tn // hd) + hh] = qh.T.astype(qt_ref.dtype)
    kv = _dot(h, wa_ref[:, att_q:att_q + 2 * att_kv])
    for g in range(att_kv // hd):
        sl = slice(g * hd, (g + 1) * hd)
        ko_ref[:, sl] = prep(kv[:, sl], kg_ref[...]).astype(ko_ref.dtype)
        vt_ref[g, 0:hd, :] = kv[:, att_kv + g * hd:att_kv + (g + 1) * hd].T.astype(vt_ref.dtype)
        vt_ref[g, hd:, :] = jnp.ones((ATT_ONES_ROWS, x_ref.shape[0]), vt_ref.dtype)

    small_ref[...] = _dot(h, ws_ref[...])
    a0 = att_q + 2 * att_kv
    na = wa_ref.shape[1] - a0
    for j in range(na // tn):
        u_ref[:, j * tn:(j + 1) * tn] = _dot(h, wa_ref[:, a0 + j * tn:a0 + (j + 1) * tn]).astype(u_ref.dtype)
    for j in range(wb_ref.shape[1] // tn):
        u_ref[:, na + j * tn:na + (j + 1) * tn] = _dot(h, wb_ref[:, j * tn:(j + 1) * tn]).astype(u_ref.dtype)


def _inproj(lay, X, mod, g, w_a, n_a, w_b, w_small, q_gain, k_gain, rope, l):
    D = w_a.shape[1]
    n_b = w_b.shape[2]
    tm = RESIDENT_TM
    assert tm == SEQ_BLOCK
    hd = ATT_HEAD_DIM
    att_q = D // 2
    att_kv = ATT_KV_HEADS * hd
    heads = att_q // hd
    n_u = n_a - att_q - 2 * att_kv + n_b
    B, NS = lay.B, lay.NS
    bs = lay.batch_seq
    tab = pl.BlockSpec((tm, hd), lambda i: (bs(i)[1], 0))
    return pl.pallas_call(
        functools.partial(_inproj_kernel, tn=512, att_q=att_q, att_kv=att_kv),
        out_shape=(jax.ShapeDtypeStruct((lay.R, n_u), BF16),
                   jax.ShapeDtypeStruct((lay.R, SMALL_COLS), F32),
                   jax.ShapeDtypeStruct((lay.R // tm, heads, hd, tm), BF16),
                   jax.ShapeDtypeStruct((B, NS * tm, att_kv), BF16),
                   jax.ShapeDtypeStruct((B, ATT_KV_HEADS, NS, hd + ATT_ONES_ROWS, tm), BF16),
                   jax.ShapeDtypeStruct((lay.R, D), BF16)),
        grid=(lay.R // tm,),
        in_specs=[
            pl.BlockSpec((tm, D), lambda i: (i, 0)),
            pl.BlockSpec((None, 1, N_MOD * D), lambda i: (lay.mod_row(i, tm), 0, 0)),
            _resident((1, D)),
            _resident((D, n_a), l),
            _resident((D, n_b), l),
            _resident((D, SMALL_COLS), l),
            _resident((1, hd)), _resident((1, hd)),
            tab, tab, tab,
        ],
        out_specs=(pl.BlockSpec((tm, n_u), lambda i: (i, 0)),
                   pl.BlockSpec((tm, SMALL_COLS), lambda i: (i, 0)),
                   pl.BlockSpec((None, heads, hd, tm), lambda i: (i, 0, 0, 0)),
                   pl.BlockSpec((None, tm, att_kv), lambda i: (*bs(i), 0)),
                   pl.BlockSpec((None, ATT_KV_HEADS, None, hd + ATT_ONES_ROWS, tm),
                                lambda i: (bs(i)[0], 0, bs(i)[1], 0, 0)),
                   pl.BlockSpec((tm, D), lambda i: (i, 0))),
        compiler_params=_cparams(("arbitrary",)),
        name="in_proj",
    )(X, mod, g, w_a, w_b, w_small, q_gain, k_gain, *rope)


def _merge_kernel(h_ref, oa_ref, og_ref, os_ref, wg_ref, wb_ref, m_ref, *, tn):
    h = h_ref[...]
    D = m_ref.shape[1]
    for j in range(D // tn):
        acc = None
        for b, o_ref in enumerate((oa_ref, og_ref, os_ref)):
            gate = _dot(h, wg_ref[:, b * D + j * tn:b * D + (j + 1) * tn])
            t = jax.nn.sigmoid(gate) * _dot(o_ref[...], wb_ref[b, :, j * tn:(j + 1) * tn])
            acc = t if acc is None else acc + t
        m_ref[:, j * tn:(j + 1) * tn] = acc.astype(m_ref.dtype)


def _merge(lay, H, o_att, o_gla, o_ssd, w_gate, w_branch, l, skip_rows):
    D = lay.D
    W = o_att.shape[1]
    tm = RESIDENT_TM
    assert tm == SEQ_BLOCK
    row0 = skip_rows // tm
    rows = lambda i: (i + row0, 0)
    seq_rows = pl.BlockSpec((None, tm, W), lambda i: (*lay.batch_seq(i + row0), 0))
    return pl.pallas_call(
        functools.partial(_merge_kernel, tn=512),
        out_shape=jax.ShapeDtypeStruct((lay.R, D), BF16),
        grid=(lay.R // tm - row0,),
        in_specs=[
            pl.BlockSpec((tm, D), rows),
            pl.BlockSpec((tm, W), rows), seq_rows, seq_rows,
            _resident(w_gate.shape[1:], l), _resident(w_branch.shape[1:], l),
        ],
        out_specs=pl.BlockSpec((tm, D), rows),
        compiler_params=_cparams(("arbitrary",)),
        name="merge",
    )(H, o_att, o_gla, o_ssd, w_gate, w_branch)


def _mlp_kernel(x_ref, m_ref, wo_ref, mod_ref, g_ref, w1_ref, w2_ref, *rest, final):
    if final:
        fg_ref, o_ref, x1_ref, h_ref = rest
    else:
        o_ref, x1_ref, h_ref = rest
    k = pl.program_id(1)
    D = x_ref.shape[-1]

    @pl.when(k == 0)
    def _():
        tn = 512
        for j in range(D // tn):
            sl = slice(j * tn, (j + 1) * tn)
            x1_ref[:, sl] = x_ref[:, sl] + mod_ref[:, 2 * D + j * tn:2 * D + (j + 1) * tn] * _dot(
                m_ref[...], wo_ref[:, sl])
        _norm_mod(x1_ref, g_ref, mod_ref, 3, h_ref)
        o_ref[...] = jnp.zeros_like(o_ref)

    f = jnp.square(jnp.maximum(_dot(h_ref[...], w1_ref[...]), 0.0)).astype(BF16)
    o_ref[...] += _dot(f, w2_ref[...])

    @pl.when(k == pl.num_programs(1) - 1)
    def _():
        gate = mod_ref[:, 5 * D:6 * D]
        rc = 64

        def body(r, carry):
            sl = pl.ds(pl.multiple_of(r * rc, rc), rc)
            y = x1_ref[sl, :] + gate * o_ref[sl, :]
            if final:
                ms = jnp.mean(y * y, axis=-1, keepdims=True)
                y = y * lax.rsqrt(ms + EPS) * fg_ref[...]
            o_ref[sl, :] = y
            return carry

        lax.fori_loop(0, x_ref.shape[0] // rc, body, 0)


def _mlp(lay, X, m, w_out, mod, g, w1, w2, l, skip_rows, final_gain=None):
    _, D, FF = w1.shape
    tm, tf = lay.tm, 1024
    row0 = skip_rows // tm
    final = final_gain is not None
    out_rows = lay.R - skip_rows if final else lay.R
    out_off = 0 if final else row0
    rows = lambda i, k: (i + row0, 0)
    in_specs = [
        pl.BlockSpec((tm, D), rows),
        pl.BlockSpec((tm, D), rows),
        _resident((D, D), l),
        pl.BlockSpec((None, 1, N_MOD * D), lambda i, k: (lay.mod_row(i + row0, tm), 0, 0)),
        _resident((1, D)),
        pl.BlockSpec((None, D, tf), lambda i, k: (l, 0, k)),
        pl.BlockSpec((None, tf, D), lambda i, k: (l, k, 0)),
    ]
    args = [X, m, w_out, mod, g, w1, w2]
    if final:
        in_specs.append(_resident((1, D)))
        args.append(final_gain)
    return pl.pallas_call(
        functools.partial(_mlp_kernel, final=final),
        out_shape=jax.ShapeDtypeStruct((out_rows, D), F32),
        grid=(lay.R // tm - row0, FF // tf),
        in_specs=in_specs,
        out_specs=pl.BlockSpec((tm, D), lambda i, k: (i + out_off, 0)),
        scratch_shapes=[pltpu.VMEM((tm, D), F32), pltpu.VMEM((tm, D), BF16)],
        compiler_params=_cparams(("arbitrary", "arbitrary")),
        name="mlp",
    )(*args)


def _attn_kernel(qt_ref, k_ref, vt_ref, o_ref, m_ref, acc_ref, s0_ref, s1_ref, *, n_ctx_blocks):
    n = pl.program_id(2)
    heads, hd, tq = qt_ref.shape
    nblk, vrows, tk = vt_ref.shape
    n_kv = jnp.where(n < n_ctx_blocks, n_ctx_blocks, nblk)
    m_ref[...] = jnp.full(m_ref.shape, -1e30, F32)
    acc_ref[...] = jnp.zeros(acc_ref.shape, F32)

    def scores(c, s_ref):
        c = jnp.minimum(c, n_kv - 1)
        k = k_ref[pl.ds(pl.multiple_of(c * tk, tk), tk), :]
        for h in range(heads):
            s_ref[h] = _dot(k, qt_ref[h])

    def update(c, s_ref):
        vt = vt_ref[c]
        for h in range(heads):
            st = s_ref[h]
            m_old = m_ref[h]
            m_new = jnp.maximum(m_old, jnp.max(st, axis=0, keepdims=True))
            pt = jnp.exp2((st - m_new).astype(BF16))
            acc_ref[h] = jnp.exp2(m_old - m_new) * acc_ref[h] + _dot(vt, pt)
            m_ref[h] = m_new

    scores(0, s0_ref)

    def pairs(c0, n_pairs):
        for j in range(n_pairs):
            scores(c0 + 2 * j + 1, s1_ref)
            update(c0 + 2 * j, s0_ref)
            scores(c0 + 2 * j + 2, s0_ref)
            update(c0 + 2 * j + 1, s1_ref)

    n_pairs = n_kv // 2
    unroll = ATT_PAIR_UNROLL

    def body(i, carry):
        pairs(2 * unroll * i, unroll)
        return carry

    lax.fori_loop(0, n_pairs // unroll, body, 0)
    done = (n_pairs // unroll) * unroll
    piece = unroll // 2
    while piece >= 1:
        @pl.when(((n_pairs - done) & piece) != 0)
        def _(done=done, piece=piece):
            pairs(2 * done, piece)

        done = done + ((n_pairs - done) & piece)
        piece //= 2

    @pl.when(n_kv % 2 == 1)
    def _():
        update(n_kv - 1, s0_ref)

    for h in range(heads):
        acc = acc_ref[h]
        ot = acc[0:hd] / acc[hd:hd + 1]
        o_ref[:, h * hd:(h + 1) * hd] = ot.T.astype(o_ref.dtype)


def _attention(lay, qt, kk, vt):
    B, kvh, NS, vrows, blk = vt.shape
    _, heads, hd, _ = qt.shape
    group = heads // kvh
    return pl.pallas_call(
        functools.partial(_attn_kernel, n_ctx_blocks=lay.NC),
        out_shape=jax.ShapeDtypeStruct((lay.R, heads * hd), BF16),
        grid=(B, kvh, NS),
        in_specs=[
            pl.BlockSpec((None, group, hd, blk), lambda b, g, n: (lay.row_block(b, n), g, 0, 0)),
            pl.BlockSpec((None, NS * blk, hd), lambda b, g, n: (b, 0, g)),
            pl.BlockSpec((None, None, NS, vrows, blk), lambda b, g, n: (b, g, 0, 0, 0)),
        ],
        out_specs=pl.BlockSpec((blk, group * hd), lambda b, g, n: (lay.row_block(b, n), g)),
        scratch_shapes=[pltpu.VMEM((group, 1, blk), F32), pltpu.VMEM((group, vrows, blk), F32),
                        pltpu.VMEM((group, blk, blk), F32), pltpu.VMEM((group, blk, blk), F32)],
        compiler_params=_cparams(("arbitrary", "arbitrary", "arbitrary")),
        name="attention",
    )(qt, kk, vt)


def _gla_kernel(*refs, nb, rev, final, dk, dv):
    per_b = 5 if final else 4
    seqs = [refs[i * per_b:(i + 1) * per_b] for i in range(nb)]
    rest = refs[nb * per_b:]
    if final:
        wd_ref, bd_ref, ng_ref, prev_ref, o_ref, st_ref = rest
    else:
        wd_ref, bd_ref, o_ref, st_ref = rest
    blk = seqs[0][0].shape[0]
    heads = seqs[0][0].shape[1] // dk
    C = GLA_CHUNK

    @pl.when(pl.program_id(0) == 0)
    def _():
        st_ref[...] = jnp.zeros_like(st_ref)

    tri = _block_tri(blk, C, rev)
    bcums = []
    for bi in range(nb):
        z = _dot(seqs[bi][3][...].astype(BF16), wd_ref[...]) + bd_ref[...]
        la = (jnp.minimum(z, 0.0) - jnp.log1p(jnp.exp(-jnp.abs(z)))) * (1.0 / GLA_TAU)
        bcums.append(_dot_exact_lhs(tri, la))

    ri = lax.broadcasted_iota(jnp.int32, (C, C), 0)
    ci = lax.broadcasted_iota(jnp.int32, (C, C), 1)
    causal = (ci >= ri) if rev else (ci <= ri)

    n_chunks = blk // C
    for step in range(n_chunks):
        c = n_chunks - 1 - step if rev else step
        rs = slice(c * C, (c + 1) * C)
        last = c * C if rev else (c + 1) * C - 1
        units = [(h, bi) for h in range(heads) for bi in range(nb)]
        scores, inter = [], []
        for h, bi in units:
            ks = slice(h * dk, (h + 1) * dk)
            q_ref, k_ref, v_ref = seqs[bi][:3]
            b = bcums[bi][rs, ks]
            b_last = bcums[bi][last:last + 1, ks]
            q = q_ref[rs, ks].astype(F32) * (dk ** -0.5)
            k = k_ref[rs, ks].astype(F32)
            q_dec = (q * jnp.exp(b)).astype(BF16)
            k_dec = (k * jnp.exp(-b)).astype(BF16)
            k_st = (k * jnp.exp(b_last - b)).astype(BF16)
            st = st_ref[bi * heads + h]
            scores.append(_dot_nt(q_dec, k_dec))
            inter.append(_dot_nt(q_dec, st.astype(BF16)))
            st_ref[bi * heads + h] = st * jnp.exp(b_last) + _dot_tn(v_ref[rs, h * dv:(h + 1) * dv], k_st)
        for (h, bi), sc, it in zip(units, scores, inter):
            vs = slice(h * dv, (h + 1) * dv)
            att = jnp.where(causal, sc, 0.0).astype(BF16)
            o = _dot(att, seqs[bi][2][rs, vs]) + it
            if final:
                tot = o + prev_ref[bi, rs, vs]
                y = tot * lax.rsqrt(jnp.mean(tot * tot, axis=-1, keepdims=True) + EPS) * ng_ref[...]
                o_ref[bi, rs, vs] = (y * _silu(seqs[bi][4][rs, vs].astype(F32))).astype(o_ref.dtype)
            else:
                o_ref[bi, rs, vs] = o


def _gla_dir(lay, U, small, cols, w_dec, b_dec, rev, norm_g=None, prev=None):
    B, NS, blk = lay.B, lay.NS, SEQ_BLOCK
    qk_w = cols["gla_q"][1]
    v_w = cols["gla_v"][1]
    dk, dv = qk_w // GLA_HEADS, v_w // GLA_HEADS
    final = prev is not None
    const = lambda shape: pl.BlockSpec(shape, lambda n: (0,) * len(shape))
    seq = lambda n: lay.seq_order(n, rev)

    def ucol(name, b):
        off, w = cols[name]
        return pl.BlockSpec((blk, w), lambda n: (lay.row_block(b, seq(n)), off // w))

    in_specs, args = [], []
    for b in range(B):
        in_specs += [ucol("gla_q", b), ucol("gla_k", b), ucol("gla_v", b),
                     pl.BlockSpec((blk, SMALL_COLS), lambda n, b=b: (lay.row_block(b, seq(n)), 0))]
        args += [U, U, U, small]
        if final:
            in_specs.append(ucol("gla_gate", b))
            args.append(U)
    in_specs += [const((SMALL_COLS, qk_w)), const((1, qk_w))]
    args += [w_dec, b_dec]
    per_batch = pl.BlockSpec((B, blk, v_w), lambda n: (0, seq(n), 0))
    if final:
        in_specs += [const((1, dv)), per_batch]
        args += [norm_g, prev]
    return pl.pallas_call(
        functools.partial(_gla_kernel, nb=B, rev=rev, final=final, dk=dk, dv=dv),
        out_shape=jax.ShapeDtypeStruct((B, NS * blk, v_w), BF16 if final else F32),
        grid=(NS,),
        in_specs=in_specs,
        out_specs=per_batch,
        scratch_shapes=[pltpu.VMEM((B * GLA_HEADS, dv, dk), F32)],
        compiler_params=_cparams(("arbitrary",)),
        name="gla_bwd" if rev else "gla_fwd",
    )(*args)


def _conv_kernel(x_ref, xp_ref, xn_ref, bc_ref, bcp_ref, bcn_ref, wx_ref, bx_ref, wbc_ref, bbc_ref,
                 xo_ref, bco_ref, *, n_ctx_total, NC, NL):
    i = pl.program_id(0)
    in_ctx = i < n_ctx_total
    p = jnp.where(in_ctx, i % NC, jnp.maximum(i - n_ctx_total, 0) % NL)
    seg = jnp.where(in_ctx, NC, NL)
    has_left = (p > 0).astype(F32)
    has_right = (p < seg - 1).astype(F32)
    blk = x_ref.shape[0]
    pad = SSD_CONV // 2
    edge = 8
    assert pad <= edge <= HALO_ROWS

    r = lax.broadcasted_iota(jnp.int32, (blk, blk), 0)
    c = lax.broadcasted_iota(jnp.int32, (blk, blk), 1)
    re = lax.broadcasted_iota(jnp.int32, (edge, HALO_ROWS), 0)
    ce = lax.broadcasted_iota(jnp.int32, (edge, HALO_ROWS), 1)
    taps = [j - pad for j in range(SSD_CONV)]
    shift = {s: jnp.where(c == r + s, 1.0, 0.0).astype(BF16) for s in taps if s != 0}
    head = {s: (jnp.where(ce == HALO_ROWS + re + s, 1.0, 0.0) * has_left).astype(BF16) for s in taps if s < 0}
    tail = {s: (jnp.where(ce == re + s - edge, 1.0, 0.0) * has_right).astype(BF16) for s in taps if s > 0}

    def run(m_ref, p_ref, n_ref, w_ref, b_ref, o_ref):
        main = m_ref[...]
        acc = b_ref[...] + w_ref[pad:pad + 1, :] * main.astype(F32)
        for j, s in enumerate(taps):
            if s != 0:
                acc = acc + w_ref[j:j + 1, :] * _dot(shift[s], main)
        first, last = acc[0:edge], acc[blk - edge:blk]
        for j, s in enumerate(taps):
            if s < 0:
                first = first + w_ref[j:j + 1, :] * _dot(head[s], p_ref[...])
            if s > 0:
                last = last + w_ref[j:j + 1, :] * _dot(tail[s], n_ref[...])
        o_ref[0:edge, :] = _silu(first).astype(o_ref.dtype)
        o_ref[edge:blk - edge, :] = _silu(acc[edge:blk - edge]).astype(o_ref.dtype)
        o_ref[blk - edge:blk, :] = _silu(last).astype(o_ref.dtype)

    run(x_ref, xp_ref, xn_ref, wx_ref, bx_ref, xo_ref)
    run(bc_ref, bcp_ref, bcn_ref, wbc_ref, bbc_ref, bco_ref)


def _conv(lay, U, cols, wx, bx, wbc, bbc):
    blk = SEQ_BLOCK
    nb = lay.R // blk
    hb = blk // HALO_ROWS
    nh = lay.R // HALO_ROWS
    xw = cols["ssd_x"][1]
    bcw = cols["ssd_bc"][1]
    xc, bcc = cols["ssd_x"][0] // xw, cols["ssd_bc"][0] // bcw
    prev = lambda i: jnp.maximum(i * hb - 1, 0)
    nxt = lambda i: jnp.minimum((i + 1) * hb, nh - 1)
    const = lambda shape: pl.BlockSpec(shape, lambda i: (0,) * len(shape))
    return pl.pallas_call(
        functools.partial(_conv_kernel, n_ctx_total=lay.B * lay.NC, NC=lay.NC, NL=lay.NL),
        out_shape=(jax.ShapeDtypeStruct((lay.R, xw), BF16), jax.ShapeDtypeStruct((lay.R, bcw), BF16)),
        grid=(nb,),
        in_specs=[
            pl.BlockSpec((blk, xw), lambda i: (i, xc)),
            pl.BlockSpec((HALO_ROWS, xw), lambda i: (prev(i), xc)),
            pl.BlockSpec((HALO_ROWS, xw), lambda i: (nxt(i), xc)),
            pl.BlockSpec((blk, bcw), lambda i: (i, bcc)),
            pl.BlockSpec((HALO_ROWS, bcw), lambda i: (prev(i), bcc)),
            pl.BlockSpec((HALO_ROWS, bcw), lambda i: (nxt(i), bcc)),
            const((SSD_CONV, xw)), const((1, xw)), const((SSD_CONV, bcw)), const((1, bcw)),
        ],
        out_specs=(pl.BlockSpec((blk, xw), lambda i: (i, 0)), pl.BlockSpec((blk, bcw), lambda i: (i, 0))),
        compiler_params=_cparams(("arbitrary",)),
        name="ssd_conv",
    )(U, U, U, U, U, U, wx, bx, wbc, bbc)


def _ssd_kernel(*refs, nb, rev, final):
    per_b = 4 if final else 3
    seqs = [refs[i * per_b:(i + 1) * per_b] for i in range(nb)]
    rest = refs[nb * per_b:]
    if final:
        dtb_ref, apad_ref, e_ref, dsk_ref, ng_ref, prev_ref, o_ref, st_ref, y_ref = rest
    else:
        dtb_ref, apad_ref, e_ref, o_ref, st_ref = rest
        y_ref = o_ref
    blk, inner = seqs[0][0].shape
    N = SSD_STATE
    P = SSD_HEAD_DIM
    C = SSD_CHUNK
    gw = inner // SSD_GROUPS
    pair = 2 * P

    @pl.when(pl.program_id(0) == 0)
    def _():
        st_ref[...] = jnp.zeros_like(st_ref)

    tri = _block_tri(blk, C, rev)
    e = e_ref[...]
    dt_fulls, cs_fulls = [], []
    for bi in range(nb):
        dt16 = _softplus(seqs[bi][2][...] + dtb_ref[...])
        cs16 = _dot_exact_lhs(tri, dt16 * apad_ref[...])
        dt_fulls.append(_dot_spread(dt16, e))
        cs_fulls.append(_dot_spread(cs16, e))

    ri = lax.broadcasted_iota(jnp.int32, (C, pair), 0)
    ci = lax.broadcasted_iota(jnp.int32, (C, pair), 1)
    cm = jnp.where(ci >= P, ci - P, ci)
    eye2 = cm == ri
    causal2 = (cm >= ri) if rev else (cm <= ri)
    left = ci < P
    zero = jnp.zeros((C, pair), BF16)

    n_chunks = blk // C
    for step in range(n_chunks):
        c = n_chunks - 1 - step if rev else step
        rs = slice(c * C, (c + 1) * C)
        last = c * C if rev else (c + 1) * C - 1
        units = [(g, bi) for g in range(SSD_GROUPS) for bi in range(nb)]
        held = []
        for g, bi in units:
            gs = slice(g * gw, (g + 1) * gw)
            x_ref, bc_ref = seqs[bi][:2]
            bm = bc_ref[rs, g * N:(g + 1) * N]
            cmat = bc_ref[rs, SSD_GROUPS * N + g * N:SSD_GROUPS * N + (g + 1) * N]
            cb2 = _dot_nt(cmat, jnp.concatenate([bm, bm], axis=0))
            cs_g = cs_fulls[bi][rs, gs]
            cs_last = cs_fulls[bi][last:last + 1, gs]
            xdt = x_ref[rs, gs].astype(F32) * dt_fulls[bi][rs, gs]
            st = st_ref[bi * SSD_GROUPS + g]
            y_inter = _dot(cmat, st.astype(BF16)) * jnp.exp(cs_g)
            w = (xdt * jnp.exp(cs_last - cs_g)).astype(BF16)
            st_ref[bi * SSD_GROUPS + g] = st * jnp.exp(cs_last) + _dot_tn(bm, w)
            held.append((cb2, cs_g, y_inter, xdt.astype(BF16)))
        for (g, bi), (cb2, cs_g, y_inter, xdt_b) in zip(units, held):
            x_ref = seqs[bi][0]
            for pr in range(gw // pair):
                ps = slice(pr * pair, (pr + 1) * pair)
                col = cs_g[:, ps]
                rowv = jnp.sum(jnp.where(eye2, col, 0.0), axis=0, keepdims=True)
                lmat = jnp.exp(jnp.where(causal2, col - rowv, -jnp.inf))
                mm = (lmat * cb2).astype(BF16)
                xp = xdt_b[:, ps]
                rhs = jnp.concatenate([jnp.where(left, xp, zero), jnp.where(left, zero, xp)], axis=0)
                y = _dot(mm, rhs) + y_inter[:, ps]
                os_ = slice(g * gw + pr * pair, g * gw + (pr + 1) * pair)
                if final:
                    y_ref[bi, rs, os_] = (y + prev_ref[bi, rs, os_]
                                          + x_ref[rs, os_].astype(F32) * dsk_ref[:, os_])
                else:
                    y_ref[bi, rs, os_] = y

    if final:
        rc = 64
        for bi in range(nb):
            z_ref = seqs[bi][3]
            for r in range(blk // rc):
                rs = slice(r * rc, (r + 1) * rc)
                for g in range(SSD_GROUPS):
                    gs = slice(g * gw, (g + 1) * gw)
                    yg = y_ref[bi, rs, gs] * _silu(z_ref[rs, gs].astype(F32))
                    ms = jnp.mean(yg * yg, axis=-1, keepdims=True)
                    o_ref[bi, rs, gs] = (yg * lax.rsqrt(ms + EPS) * ng_ref[:, gs]).astype(o_ref.dtype)


def _ssd_dir(lay, xc, bcc, small, U, cols, dtb, apad, expand, rev, dsk=None, norm_g=None, prev=None):
    B, NS, blk = lay.B, lay.NS, SEQ_BLOCK
    inner = xc.shape[1]
    bcw = bcc.shape[1]
    final = prev is not None
    seq = lambda n: lay.seq_order(n, rev)
    const = lambda shape: pl.BlockSpec(shape, lambda n: (0,) * len(shape))

    def rows(w, b, col=0):
        return pl.BlockSpec((blk, w), lambda n: (lay.row_block(b, seq(n)), col))

    in_specs, args = [], []
    for b in range(B):
        in_specs += [rows(inner, b), rows(bcw, b), rows(SMALL_COLS, b)]
        args += [xc, bcc, small]
        if final:
            zc = cols["ssd_z"]
            in_specs.append(rows(inner, b, zc[0] // zc[1]))
            args.append(U)
    in_specs += [const((1, SMALL_COLS)), const((1, SMALL_COLS)), const((SMALL_COLS, inner))]
    args += [dtb, apad, expand]
    per_batch = pl.BlockSpec((B, blk, inner), lambda n: (0, seq(n), 0))
    if final:
        in_specs += [const((1, inner)), const((1, inner)), per_batch]
        args += [dsk, norm_g, prev]
    return pl.pallas_call(
        functools.partial(_ssd_kernel, nb=B, rev=rev, final=final),
        out_shape=jax.ShapeDtypeStruct((B, NS * blk, inner), BF16 if final else F32),
        grid=(NS,),
        in_specs=in_specs,
        out_specs=per_batch,
        scratch_shapes=[pltpu.VMEM((B * SSD_GROUPS, SSD_STATE, inner // SSD_GROUPS), F32)]
        + ([pltpu.VMEM((B, blk, inner), F32)] if final else []),
        compiler_params=_cparams(("arbitrary",)),
        name="ssd_bwd" if rev else "ssd_fwd",
    )(*args)


def _rope_tables(CTX, SEQ):
    hd = ATT_HEAD_DIM
    pairs = hd // 4
    rows = SEQ // GRID_W
    row = jnp.repeat(jnp.arange(rows, dtype=F32), GRID_W)
    col = jnp.tile(jnp.arange(GRID_W, dtype=F32), rows)
    inv = jnp.exp(-math.log(ROPE_THETA) * jnp.arange(pairs, dtype=F32) / pairs)
    ar = row[:, None] * inv
    ac = col[:, None] * inv
    ang = jnp.concatenate([ar, ar, ac, ac], axis=-1)
    cos, sin = jnp.cos(ang), jnp.sin(ang)
    first = (jnp.arange(hd) % (2 * pairs)) < pairs
    sa = jnp.where(first, -sin, 0.0)
    sb = jnp.where(first, 0.0, sin)
    ones = jnp.ones((CTX, hd), F32)
    zeros = jnp.zeros((CTX, hd), F32)
    return (jnp.concatenate([ones, cos]), jnp.concatenate([zeros, sa]), jnp.concatenate([zeros, sb]))


def _column_plan(D):
    mix = D // 2
    gla_v = mix
    gla_qk = mix // 2
    ssd_inner = mix
    ssd_bc = 2 * SSD_GROUPS * SSD_STATE
    order = [("gla_q", gla_qk), ("gla_k", gla_qk), ("gla_v", gla_v), ("gla_gate", gla_v),
             ("ssd_z", ssd_inner), ("ssd_x", ssd_inner), ("ssd_bc", ssd_bc)]
    cols, off = {}, 0
    for name, w in order:
        assert off % w == 0
        cols[name] = (off, w)
        off += w
    return cols, off


def _source_columns(D):
    mix = D // 2
    att_kv = ATT_KV_HEADS * ATT_HEAD_DIM
    ssd_bc = SSD_GROUPS * SSD_STATE
    heads = mix // SSD_HEAD_DIM
    widths = [("att_q", mix), ("att_k", att_kv), ("att_v", att_kv), ("gla_q", mix // 2), ("gla_k", mix // 2),
              ("gla_v", mix), ("gla_gate", mix), ("gla_lr", 2 * GLA_LOWRANK), ("ssd_z", mix),
              ("ssd_x", mix), ("ssd_bc", 2 * ssd_bc), ("ssd_dt", 2 * heads), ("gates", N_BRANCH * D)]
    src, off = {}, 0
    for name, w in widths:
        src[name] = (off, w)
        off += w
    return src, off


def kernel(x, c, ctx, c_ctx, norm1, norm2, w_ada, b_ada, w_in, att_q_norm, att_k_norm, gla_w_decay,
           gla_b_decay, gla_norm, ssd_conv_w, ssd_conv_b, ssd_dt_bias, ssd_a_log, ssd_d, ssd_norm,
           w_branch, w_out, w_ff1, w_ff2, final_norm):
    B, SEQ, D = x.shape
    CTX = ctx.shape[1]
    depth = w_in.shape[0]
    lay = _Layout(B, CTX, SEQ, D)
    cols, n_main = _column_plan(D)
    src, n_src = _source_columns(D)
    assert n_src == w_in.shape[2]
    mix = D // 2
    ssd_heads = mix // SSD_HEAD_DIM
    lr_w = 2 * GLA_LOWRANK
    assert lr_w + 2 * ssd_heads <= SMALL_COLS

    X = jnp.concatenate([ctx.reshape(B * CTX, D), x.reshape(B * SEQ, D)], axis=0)
    cvec = jnp.concatenate([c, c_ctx[None, :], jnp.zeros((MOD_ROWS - B - 1, D), F32)], axis=0)
    mods = _ada(cvec, w_ada, b_ada).reshape(depth, MOD_ROWS, 1, N_MOD * D)
    rope = _rope_tables(CTX, SEQ)

    expand = np.zeros((2, SMALL_COLS, mix), np.float32)
    for d in range(2):
        for h in range(ssd_heads):
            expand[d, lr_w + d * ssd_heads + h, h * SSD_HEAD_DIM:(h + 1) * SSD_HEAD_DIM] = 1.0
    expand = jnp.asarray(expand, BF16)

    lead = ["att_q", "att_k", "att_v", "gla_q", "gla_k", "gla_v", "gla_gate"]
    assert [src[n][0] for n in lead] == list(np.cumsum([0] + [src[n][1] for n in lead[:-1]]))
    assert [cols[n][0] for n in lead[3:]] == [src[n][0] - src["gla_q"][0] for n in lead[3:]]
    n_a = src["gla_gate"][0] + src["gla_gate"][1]
    w_in_b = w_in.astype(BF16)
    take = lambda name: w_in_b[:, :, src[name][0]:src[name][0] + src[name][1]]
    w_b = jnp.concatenate([take("ssd_z"), take("ssd_x"), take("ssd_bc")], axis=2)
    assert cols["ssd_z"][0] == n_a - src["gla_q"][0] and cols["ssd_x"][0] == cols["ssd_z"][0] + mix
    w_small = jnp.concatenate([take("gla_lr"), take("ssd_dt"),
                               jnp.zeros((depth, D, SMALL_COLS - lr_w - 2 * ssd_heads), BF16)], axis=2)
    w_gate = take("gates")
    w_branch_b, w_out_b = w_branch.astype(BF16), w_out.astype(BF16)
    w_ff1_b, w_ff2_b = w_ff1.astype(BF16), w_ff2.astype(BF16)

    out = None
    for l in range(depth):
        mod = mods[l]
        g1 = norm1[l][None, :]

        U, small, qt, kk, vt, H = _inproj(lay, X, mod, g1, w_in_b, n_a, w_b, w_small, att_q_norm[l][None, :],
                                          att_k_norm[l][None, :], rope, l)

        o_att = _attention(lay, qt, kk, vt)

        gla_out = None
        for d in range(2):
            wd = jnp.zeros((SMALL_COLS, mix // 2), F32).at[d * GLA_LOWRANK:(d + 1) * GLA_LOWRANK].set(
                gla_w_decay[l, d]).astype(BF16)
            bd = gla_b_decay[l, d][None, :]
            if d == 0:
                gla_out = _gla_dir(lay, U, small, cols, wd, bd, rev=False)
            else:
                gla_out = _gla_dir(lay, U, small, cols, wd, bd, rev=True,
                                   norm_g=gla_norm[l][None, :], prev=gla_out)
        o_gla = gla_out

        cw, cbias = ssd_conv_w[l], ssd_conv_b[l]
        xc, bcc = _conv(lay, U, cols, cw[:, :mix], cbias[None, :mix], cw[:, mix:], cbias[None, mix:])
        ssd_out = None
        for d in range(2):
            lanes = slice(lr_w + d * ssd_heads, lr_w + (d + 1) * ssd_heads)
            dtb = jnp.zeros((1, SMALL_COLS), F32).at[0, lanes].set(ssd_dt_bias[l, d])
            apad = jnp.zeros((1, SMALL_COLS), F32).at[0, lanes].set(-jnp.exp(ssd_a_log[l, d]))
            if d == 0:
                ssd_out = _ssd_dir(lay, xc, bcc, small, U, cols, dtb, apad, expand[d], rev=False)
            else:
                ssd_out = _ssd_dir(lay, xc, bcc, small, U, cols, dtb, apad, expand[d], rev=True,
                                   dsk=jnp.repeat(ssd_d[l], SSD_HEAD_DIM)[None, :],
                                   norm_g=ssd_norm[l][None, :], prev=ssd_out)
        o_ssd = ssd_out

        last = l == depth - 1
        skip = B * CTX if last else 0
        m = _merge(lay, H, o_att, o_gla, o_ssd, w_gate, w_branch_b, l, skip)
        mlp_args = (lay, X, m, w_out_b, mod, norm2[l][None, :], w_ff1_b, w_ff2_b, l, skip)
        if last:
            out = _mlp(*mlp_args, final_gain=final_norm[None, :])
        else:
            X = _mlp(*mlp_args)
    return out.reshape(B, SEQ, D)
```

```python
import functools
import math

import jax
import jax.numpy as jnp
import numpy as np
from jax import lax
from jax.experimental import pallas as pl
from jax.experimental.pallas import tpu as pltpu

F32 = jnp.float32
BF16 = jnp.bfloat16

EPS = 1e-6
GRID_W = 64
N_BRANCH = 3
N_MOD = 6
ATT_HEAD_DIM = 128
ATT_KV_HEADS = 2
ROPE_THETA = 10000.0
GLA_HEADS = 4
GLA_LOWRANK = 16
GLA_TAU = 16.0
GLA_CHUNK = 64
SSD_HEAD_DIM = 64
SSD_GROUPS = 2
SSD_STATE = 128
SSD_CONV = 5
SSD_CHUNK = 64

SEQ_BLOCK = 256
RESIDENT_TM = 256
MOD_ROWS = 8
SMALL_COLS = 128
HALO_ROWS = 16
ATT_ONES_ROWS = 16
ATT_PAIR_UNROLL = 4
VMEM_LIMIT = 56 * 1024 * 1024


def _cparams(sem):
    return pltpu.CompilerParams(dimension_semantics=sem, vmem_limit_bytes=VMEM_LIMIT)


def _dot(a, b):
    return jnp.dot(a, b, preferred_element_type=F32)


def _dot_nt(a, b):
    return lax.dot_general(a, b, (((1,), (1,)), ((), ())), preferred_element_type=F32)


def _dot_tn(a, b):
    return lax.dot_general(a, b, (((0,), (0,)), ((), ())), preferred_element_type=F32)


def _split3(x):
    hi = x.astype(BF16)
    r = x - hi.astype(F32)
    mid = r.astype(BF16)
    lo = (r - mid.astype(F32)).astype(BF16)
    return hi, mid, lo


def _dot_exact_lhs(a, x):
    hi, mid, lo = _split3(x)
    return _dot(a, lo) + _dot(a, mid) + _dot(a, hi)


def _dot_spread(x, e):
    hi = x.astype(BF16)
    lo = (x - hi.astype(F32)).astype(BF16)
    return _dot(lo, e) + _dot(hi, e)


def _silu(x):
    return x * jax.nn.sigmoid(x)


def _softplus(x):
    return jnp.maximum(x, 0.0) + jnp.log1p(jnp.exp(-jnp.abs(x)))


def _block_tri(n, chunk, rev):
    r = lax.broadcasted_iota(jnp.int32, (n, n), 0)
    c = lax.broadcasted_iota(jnp.int32, (n, n), 1)
    shift = chunk.bit_length() - 1
    assert chunk == 1 << shift
    same = jnp.right_shift(r, shift) == jnp.right_shift(c, shift)
    tri = (c >= r) if rev else (c <= r)
    return jnp.where(same, jnp.where(tri, 1.0, 0.0), 0.0).astype(BF16)


class _Layout:
    def __init__(self, B, CTX, SEQ, D):
        assert CTX % SEQ_BLOCK == 0 and SEQ % SEQ_BLOCK == 0
        self.B, self.CTX, self.SEQ, self.D = B, CTX, SEQ, D
        self.R = B * (CTX + SEQ)
        self.NC = CTX // SEQ_BLOCK
        self.NL = SEQ // SEQ_BLOCK
        self.NS = self.NC + self.NL
        self.tm = next(t for t in (512, 256) if (B * CTX) % t == 0 and SEQ % t == 0)
        assert B + 1 <= MOD_ROWS

    def row_block(self, b, s):
        return jnp.where(s < self.NC, b * self.NC + s, self.B * self.NC + b * self.NL + (s - self.NC))

    def batch_seq(self, i):
        nctx = self.B * self.NC
        j = jnp.maximum(i - nctx, 0)
        return (jnp.where(i < nctx, i // self.NC, j // self.NL),
                jnp.where(i < nctx, i % self.NC, self.NC + j % self.NL))

    def seq_order(self, n, rev):
        if not rev:
            return n
        return jnp.where(n < self.NC, self.NC - 1 - n, self.NC + self.NL - 1 - (n - self.NC))

    def mod_row(self, i, tm):
        nctx = (self.B * self.CTX) // tm
        return jnp.where(i < nctx, self.B, (i - nctx) // (self.SEQ // tm))


def _ada_kernel(c_ref, w_ref, b_ref, o_ref):
    s = _silu(c_ref[...]).astype(BF16)
    o_ref[0] = _dot(s, w_ref[0].astype(BF16)) + b_ref[0]


def _ada(cvec, w_ada, b_ada):
    depth, D, n = w_ada.shape
    tn = 1024
    return pl.pallas_call(
        _ada_kernel,
        out_shape=jax.ShapeDtypeStruct((depth, MOD_ROWS, n), F32),
        grid=(depth, n // tn),
        in_specs=[
            pl.BlockSpec((MOD_ROWS, D), lambda l, j: (0, 0)),
            pl.BlockSpec((1, D, tn), lambda l, j: (l, 0, j)),
            pl.BlockSpec((1, 1, tn), lambda l, j: (l, 0, j)),
        ],
        out_specs=pl.BlockSpec((1, MOD_ROWS, tn), lambda l, j: (l, 0, j)),
        compiler_params=_cparams(("arbitrary", "arbitrary")),
        name="ada",
    )(cvec, w_ada, b_ada.reshape(depth, 1, n))


def _norm_mod(x_ref, g_ref, mod_ref, k, h_ref):
    D = x_ref.shape[-1]
    shift = mod_ref[:, k * D:(k + 1) * D]
    a = g_ref[...] * (1.0 + mod_ref[:, (k + 1) * D:(k + 2) * D])
    rc = 64

    def body(r, carry):
        sl = pl.ds(pl.multiple_of(r * rc, rc), rc)
        xf = x_ref[sl, :]
        ms = jnp.mean(xf * xf, axis=-1, keepdims=True)
        h_ref[sl, :] = (xf * lax.rsqrt(ms + EPS) * a + shift).astype(h_ref.dtype)
        return carry

    lax.fori_loop(0, x_ref.shape[0] // rc, body, 0)


def _resident(shape, layer=None):
    if layer is None:
        return pl.BlockSpec(shape, lambda *_: (0,) * len(shape), pipeline_mode=pl.Buffered(1))
    return pl.BlockSpec((None,) + tuple(shape), lambda *_: (layer,) + (0,) * len(shape),
                        pipeline_mode=pl.Buffered(1))


def _inproj_kernel(x_ref, mod_ref, g_ref, wa_ref, wb_ref, ws_ref, qg_ref, kg_ref, cos_ref, sa_ref, sb_ref,
                   u_ref, small_ref, qt_ref, ko_ref, vt_ref, h_ref, *, tn, att_q, att_kv):
    _norm_mod(x_ref, g_ref, mod_ref, 0, h_ref)
    h = h_ref[...]
    hd = ATT_HEAD_DIM
    cos, sa, sb = cos_ref[...], sa_ref[...], sb_ref[...]

    def prep(y, g):
        y = y * lax.rsqrt(jnp.mean(y * y, axis=-1, keepdims=True) + EPS) * g
        return y * cos + pltpu.roll(y, hd - hd // 4, 1) * sa + pltpu.roll(y, hd // 4, 1) * sb

    q_scale = (hd ** -0.5) * math.log2(math.e)
    for j in range(att_q // tn):
        q = _dot(h, wa_ref[:, j * tn:(j + 1) * tn])
        for hh in range(tn // hd):
            qh = prep(q[:, hh * hd:(hh + 1) * hd], qg_ref[...]) * q_scale
            qt_ref[j * (tn // hd) + hh] = qh.T.astype(qt_ref.dtype)
    kv = _dot(h, wa_ref[:, att_q:att_q + 2 * att_kv])
    for g in range(att_kv // hd):
        sl = slice(g * hd, (g + 1) * hd)
        ko_ref[:, sl] = prep(kv[:, sl], kg_ref[...]).astype(ko_ref.dtype)
        vt_ref[g, 0:hd, :] = kv[:, att_kv + g * hd:att_kv + (g + 1) * hd].T.astype(vt_ref.dtype)
        vt_ref[g, hd:, :] = jnp.ones((ATT_ONES_ROWS, x_ref.shape[0]), vt_ref.dtype)

    small_ref[...] = _dot(h, ws_ref[...])
    a0 = att_q + 2 * att_kv
    na = wa_ref.shape[1] - a0
    for j in range(na // tn):
        u_ref[:, j * tn:(j + 1) * tn] = _dot(h, wa_ref[:, a0 + j * tn:a0 + (j + 1) * tn]).astype(u_ref.dtype)
    for j in range(wb_ref.shape[1] // tn):
        u_ref[:, na + j * tn:na + (j + 1) * tn] = _dot(h, wb_ref[:, j * tn:(j + 1) * tn]).astype(u_ref.dtype)


def _inproj(lay, X, mod, g, w_a, n_a, w_b, w_small, q_gain, k_gain, rope, l):
    D = w_a.shape[1]
    n_b = w_b.shape[2]
    tm = RESIDENT_TM
    assert tm == SEQ_BLOCK
    hd = ATT_HEAD_DIM
    att_q = D // 2
    att_kv = ATT_KV_HEADS * hd
    heads = att_q // hd
    n_u = n_a - att_q - 2 * att_kv + n_b
    B, NS = lay.B, lay.NS
    bs = lay.batch_seq
    tab = pl.BlockSpec((tm, hd), lambda i: (bs(i)[1], 0))
    return pl.pallas_call(
        functools.partial(_inproj_kernel, tn=512, att_q=att_q, att_kv=att_kv),
        out_shape=(jax.ShapeDtypeStruct((lay.R, n_u), BF16),
                   jax.ShapeDtypeStruct((lay.R, SMALL_COLS), F32),
                   jax.ShapeDtypeStruct((lay.R // tm, heads, hd, tm), BF16),
                   jax.ShapeDtypeStruct((B, NS * tm, att_kv), BF16),
                   jax.ShapeDtypeStruct((B, ATT_KV_HEADS, NS, hd + ATT_ONES_ROWS, tm), BF16),
                   jax.ShapeDtypeStruct((lay.R, D), BF16)),
        grid=(lay.R // tm,),
        in_specs=[
            pl.BlockSpec((tm, D), lambda i: (i, 0)),
            pl.BlockSpec((None, 1, N_MOD * D), lambda i: (lay.mod_row(i, tm), 0, 0)),
            _resident((1, D)),
            _resident((D, n_a), l),
            _resident((D, n_b), l),
            _resident((D, SMALL_COLS), l),
            _resident((1, hd)), _resident((1, hd)),
            tab, tab, tab,
        ],
        out_specs=(pl.BlockSpec((tm, n_u), lambda i: (i, 0)),
                   pl.BlockSpec((tm, SMALL_COLS), lambda i: (i, 0)),
                   pl.BlockSpec((None, heads, hd, tm), lambda i: (i, 0, 0, 0)),
                   pl.BlockSpec((None, tm, att_kv), lambda i: (*bs(i), 0)),
                   pl.BlockSpec((None, ATT_KV_HEADS, None, hd + ATT_ONES_ROWS, tm),
                                lambda i: (bs(i)[0], 0, bs(i)[1], 0, 0)),
                   pl.BlockSpec((tm, D), lambda i: (i, 0))),
        compiler_params=_cparams(("arbitrary",)),
        name="in_proj",
    )(X, mod, g, w_a, w_b, w_small, q_gain, k_gain, *rope)


def _slab_count(n_steps, *row_counts):
    n = 1
    while 2 * n <= n_steps and all(r % (2 * n * 16) == 0 for r in row_counts):
        n *= 2
    return n


def _merge_kernel(h_ref, oa_ref, og_ref, os_ref, wg_ref, wb_ref, *rest, tn, cast_next):
    if cast_next:
        c1_ref, c2_ref, m_ref, n1_ref, n2_ref = rest
        n1_ref[...] = c1_ref[...].astype(n1_ref.dtype)
        n2_ref[...] = c2_ref[...].astype(n2_ref.dtype)
    else:
        (m_ref,) = rest
    h = h_ref[...]
    D = m_ref.shape[1]
    for j in range(D // tn):
        acc = None
        for b, o_ref in enumerate((oa_ref, og_ref, os_ref)):
            gate = _dot(h, wg_ref[:, b * D + j * tn:b * D + (j + 1) * tn])
            t = jax.nn.sigmoid(gate) * _dot(o_ref[...], wb_ref[b, :, j * tn:(j + 1) * tn])
            acc = t if acc is None else acc + t
        m_ref[:, j * tn:(j + 1) * tn] = acc.astype(m_ref.dtype)


def _merge(lay, H, o_att, o_gla, o_ssd, w_gate, w_branch, l, skip_rows, cast_next=None):
    D = lay.D
    W = o_att.shape[1]
    tm = RESIDENT_TM
    assert tm == SEQ_BLOCK
    row0 = skip_rows // tm
    rows = lambda i: (i + row0, 0)
    seq_rows = pl.BlockSpec((None, tm, W), lambda i: (*lay.batch_seq(i + row0), 0))
    n_i = lay.R // tm - row0
    in_specs = [
        pl.BlockSpec((tm, D), rows),
        pl.BlockSpec((tm, W), rows), seq_rows, seq_rows,
        _resident(w_gate.shape[1:], l), _resident(w_branch.shape),
    ]
    args = [H, o_att, o_gla, o_ssd, w_gate, w_branch]
    out_shape = [jax.ShapeDtypeStruct((lay.R, D), BF16)]
    out_specs = [pl.BlockSpec((tm, D), rows)]
    if cast_next is not None:
        n_slabs = _slab_count(n_i, N_BRANCH * W, D)
        r1, r2 = N_BRANCH * W // n_slabs, D // n_slabs
        slab = lambda i: jnp.minimum(i, n_slabs - 1)
        in_specs += [pl.BlockSpec((None, r1, D), lambda i: (l + 1, slab(i), 0)),
                     pl.BlockSpec((None, r2, D), lambda i: (l + 1, slab(i), 0))]
        args += list(cast_next)
        out_shape += [jax.ShapeDtypeStruct((N_BRANCH * W, D), BF16), jax.ShapeDtypeStruct((D, D), BF16)]
        out_specs += [pl.BlockSpec((r1, D), lambda i: (slab(i), 0)), pl.BlockSpec((r2, D), lambda i: (slab(i), 0))]
    res = pl.pallas_call(
        functools.partial(_merge_kernel, tn=512, cast_next=cast_next is not None),
        out_shape=tuple(out_shape),
        grid=(n_i,),
        in_specs=in_specs,
        out_specs=tuple(out_specs),
        compiler_params=_cparams(("arbitrary",)),
        name="merge",
    )(*args)
    if cast_next is None:
        return res[0]
    return res[0], res[1].reshape(N_BRANCH, W, D), res[2]


def _mlp_kernel(x_ref, m_ref, wo_ref, mod_ref, g_ref, w1_ref, w2_ref, *rest, final, cast_next):
    rest = list(rest)
    fg_ref = rest.pop(0) if final else None
    if cast_next:
        c1_ref, c2_ref = rest.pop(0), rest.pop(0)
        o_ref, n1_ref, n2_ref, x1_ref, h_ref = rest
        n1_ref[...] = c1_ref[...].astype(n1_ref.dtype)
        n2_ref[...] = c2_ref[...].astype(n2_ref.dtype)
    else:
        o_ref, x1_ref, h_ref = rest
    k = pl.program_id(1)
    D = x_ref.shape[-1]

    @pl.when(k == 0)
    def _():
        tn = 512
        for j in range(D // tn):
            sl = slice(j * tn, (j + 1) * tn)
            x1_ref[:, sl] = x_ref[:, sl] + mod_ref[:, 2 * D + j * tn:2 * D + (j + 1) * tn] * _dot(
                m_ref[...], wo_ref[:, sl])
        _norm_mod(x1_ref, g_ref, mod_ref, 3, h_ref)
        o_ref[...] = jnp.zeros_like(o_ref)

    f = jnp.square(jnp.maximum(_dot(h_ref[...], w1_ref[...]), 0.0)).astype(BF16)
    o_ref[...] += _dot(f, w2_ref[...])

    @pl.when(k == pl.num_programs(1) - 1)
    def _():
        gate = mod_ref[:, 5 * D:6 * D]
        rc = 64

        def body(r, carry):
            sl = pl.ds(pl.multiple_of(r * rc, rc), rc)
            y = x1_ref[sl, :] + gate * o_ref[sl, :]
            if final:
                ms = jnp.mean(y * y, axis=-1, keepdims=True)
                y = y * lax.rsqrt(ms + EPS) * fg_ref[...]
            o_ref[sl, :] = y
            return carry

        lax.fori_loop(0, x_ref.shape[0] // rc, body, 0)


def _mlp(lay, X, m, w_out, mod, g, w1, w2, l, skip_rows, final_gain=None, cast_next=None):
    D, FF = w1.shape
    tm, tf = lay.tm, 1024
    row0 = skip_rows // tm
    final = final_gain is not None
    out_rows = lay.R - skip_rows if final else lay.R
    out_off = 0 if final else row0
    rows = lambda i, k: (i + row0, 0)
    n_i, n_k = lay.R // tm - row0, FF // tf
    in_specs = [
        pl.BlockSpec((tm, D), rows),
        pl.BlockSpec((tm, D), rows),
        _resident((D, D)),
        pl.BlockSpec((None, 1, N_MOD * D), lambda i, k: (lay.mod_row(i + row0, tm), 0, 0)),
        _resident((1, D)),
        pl.BlockSpec((D, tf), lambda i, k: (0, k)),
        pl.BlockSpec((tf, D), lambda i, k: (k, 0)),
    ]
    args = [X, m, w_out, mod, g, w1, w2]
    out_shape = [jax.ShapeDtypeStruct((out_rows, D), F32)]
    out_specs = [pl.BlockSpec((tm, D), lambda i, k: (i + out_off, 0))]
    if final:
        in_specs.append(_resident((1, D)))
        args.append(final_gain)
    if cast_next is not None:
        n_slabs = _slab_count(n_i * n_k, D, FF)
        r1, r2 = D // n_slabs, FF // n_slabs
        slab = lambda i, k: jnp.minimum(i * n_k + k, n_slabs - 1)
        in_specs += [pl.BlockSpec((None, r1, FF), lambda i, k: (l + 1, slab(i, k), 0)),
                     pl.BlockSpec((None, r2, D), lambda i, k: (l + 1, slab(i, k), 0))]
        args += list(cast_next)
        out_shape += [jax.ShapeDtypeStruct((D, FF), BF16), jax.ShapeDtypeStruct((FF, D), BF16)]
        out_specs += [pl.BlockSpec((r1, FF), lambda i, k: (slab(i, k), 0)),
                      pl.BlockSpec((r2, D), lambda i, k: (slab(i, k), 0))]
    res = pl.pallas_call(
        functools.partial(_mlp_kernel, final=final, cast_next=cast_next is not None),
        out_shape=tuple(out_shape),
        grid=(n_i, n_k),
        in_specs=in_specs,
        out_specs=tuple(out_specs),
        scratch_shapes=[pltpu.VMEM((tm, D), F32), pltpu.VMEM((tm, D), BF16)],
        compiler_params=_cparams(("arbitrary", "arbitrary")),
        name="mlp",
    )(*args)
    return res if cast_next is not None else res[0]


def _attn_kernel(qt_ref, k_ref, vt_ref, o_ref, m_ref, acc_ref, s0_ref, s1_ref, *, n_ctx_blocks):
    n = pl.program_id(2)
    heads, hd, tq = qt_ref.shape
    nblk, vrows, tk = vt_ref.shape
    n_kv = jnp.where(n < n_ctx_blocks, n_ctx_blocks, nblk)
    m_ref[...] = jnp.full(m_ref.shape, -1e30, F32)
    acc_ref[...] = jnp.zeros(acc_ref.shape, F32)

    def scores(c, s_ref):
        c = jnp.minimum(c, n_kv - 1)
        k = k_ref[pl.ds(pl.multiple_of(c * tk, tk), tk), :]
        for h in range(heads):
            s_ref[h] = _dot(k, qt_ref[h])

    def update(c, s_ref):
        vt = vt_ref[c]
        for h in range(heads):
            st = s_ref[h]
            m_old = m_ref[h]
            m_new = jnp.maximum(m_old, jnp.max(st, axis=0, keepdims=True))
            pt = jnp.exp2((st - m_new).astype(BF16))
            acc_ref[h] = jnp.exp2(m_old - m_new) * acc_ref[h] + _dot(vt, pt)
            m_ref[h] = m_new

    scores(0, s0_ref)

    def pairs(c0, n_pairs):
        for j in range(n_pairs):
            scores(c0 + 2 * j + 1, s1_ref)
            update(c0 + 2 * j, s0_ref)
            scores(c0 + 2 * j + 2, s0_ref)
            update(c0 + 2 * j + 1, s1_ref)

    n_pairs = n_kv // 2
    unroll = ATT_PAIR_UNROLL

    def body(i, carry):
        pairs(2 * unroll * i, unroll)
        return carry

    lax.fori_loop(0, n_pairs // unroll, body, 0)
    done = (n_pairs // unroll) * unroll
    piece = unroll // 2
    while piece >= 1:
        @pl.when(((n_pairs - done) & piece) != 0)
        def _(done=done, piece=piece):
            pairs(2 * done, piece)

        done = done + ((n_pairs - done) & piece)
        piece //= 2

    @pl.when(n_kv % 2 == 1)
    def _():
        update(n_kv - 1, s0_ref)

    for h in range(heads):
        acc = acc_ref[h]
        ot = acc[0:hd] / acc[hd:hd + 1]
        o_ref[:, h * hd:(h + 1) * hd] = ot.T.astype(o_ref.dtype)


def _attention(lay, qt, kk, vt):
    B, kvh, NS, vrows, blk = vt.shape
    _, heads, hd, _ = qt.shape
    group = heads // kvh
    return pl.pallas_call(
        functools.partial(_attn_kernel, n_ctx_blocks=lay.NC),
        out_shape=jax.ShapeDtypeStruct((lay.R, heads * hd), BF16),
        grid=(B, kvh, NS),
        in_specs=[
            pl.BlockSpec((None, group, hd, blk), lambda b, g, n: (lay.row_block(b, n), g, 0, 0)),
            pl.BlockSpec((None, NS * blk, hd), lambda b, g, n: (b, 0, g)),
            pl.BlockSpec((None, None, NS, vrows, blk), lambda b, g, n: (b, g, 0, 0, 0)),
        ],
        out_specs=pl.BlockSpec((blk, group * hd), lambda b, g, n: (lay.row_block(b, n), g)),
        scratch_shapes=[pltpu.VMEM((group, 1, blk), F32), pltpu.VMEM((group, vrows, blk), F32),
                        pltpu.VMEM((group, blk, blk), F32), pltpu.VMEM((group, blk, blk), F32)],
        compiler_params=_cparams(("arbitrary", "arbitrary", "arbitrary")),
        name="attention",
    )(qt, kk, vt)


def _gla_kernel(*refs, nb, rev, final, dk, dv):
    per_b = 5 if final else 4
    seqs = [refs[i * per_b:(i + 1) * per_b] for i in range(nb)]
    rest = refs[nb * per_b:]
    if final:
        wd_ref, bd_ref, ng_ref, prev_ref, o_ref, st_ref = rest
    else:
        wd_ref, bd_ref, o_ref, st_ref = rest
    blk = seqs[0][0].shape[0]
    heads = seqs[0][0].shape[1] // dk
    C = GLA_CHUNK

    @pl.when(pl.program_id(0) == 0)
    def _():
        st_ref[...] = jnp.zeros_like(st_ref)

    tri = _block_tri(blk, C, rev)
    bcums = []
    for bi in range(nb):
        z = _dot(seqs[bi][3][...].astype(BF16), wd_ref[...]) + bd_ref[...]
        la = (jnp.minimum(z, 0.0) - jnp.log1p(jnp.exp(-jnp.abs(z)))) * (1.0 / GLA_TAU)
        bcums.append(_dot_exact_lhs(tri, la))

    ri = lax.broadcasted_iota(jnp.int32, (C, C), 0)
    ci = lax.broadcasted_iota(jnp.int32, (C, C), 1)
    causal = (ci >= ri) if rev else (ci <= ri)

    n_chunks = blk // C
    for step in range(n_chunks):
        c = n_chunks - 1 - step if rev else step
        rs = slice(c * C, (c + 1) * C)
        last = c * C if rev else (c + 1) * C - 1
        units = [(h, bi) for h in range(heads) for bi in range(nb)]
        scores, inter = [], []
        for h, bi in units:
            ks = slice(h * dk, (h + 1) * dk)
            q_ref, k_ref, v_ref = seqs[bi][:3]
            b = bcums[bi][rs, ks]
            b_last = bcums[bi][last:last + 1, ks]
            q = q_ref[rs, ks].astype(F32) * (dk ** -0.5)
            k = k_ref[rs, ks].astype(F32)
            q_dec = (q * jnp.exp(b)).astype(BF16)
            k_dec = (k * jnp.exp(-b)).astype(BF16)
            k_st = (k * jnp.exp(b_last - b)).astype(BF16)
            st = st_ref[bi * heads + h]
            scores.append(_dot_nt(q_dec, k_dec))
            inter.append(_dot_nt(q_dec, st.astype(BF16)))
            st_ref[bi * heads + h] = st * jnp.exp(b_last) + _dot_tn(v_ref[rs, h * dv:(h + 1) * dv], k_st)
        for (h, bi), sc, it in zip(units, scores, inter):
            vs = slice(h * dv, (h + 1) * dv)
            att = jnp.where(causal, sc, 0.0).astype(BF16)
            o = _dot(att, seqs[bi][2][rs, vs]) + it
            if final:
                tot = o + prev_ref[bi, rs, vs]
                y = tot * lax.rsqrt(jnp.mean(tot * tot, axis=-1, keepdims=True) + EPS) * ng_ref[...]
                o_ref[bi, rs, vs] = (y * _silu(seqs[bi][4][rs, vs].astype(F32))).astype(o_ref.dtype)
            else:
                o_ref[bi, rs, vs] = o


def _gla_dir(lay, U, small, cols, w_dec, b_dec, rev, norm_g=None, prev=None):
    B, NS, blk = lay.B, lay.NS, SEQ_BLOCK
    qk_w = cols["gla_q"][1]
    v_w = cols["gla_v"][1]
    dk, dv = qk_w // GLA_HEADS, v_w // GLA_HEADS
    final = prev is not None
    const = lambda shape: pl.BlockSpec(shape, lambda n: (0,) * len(shape))
    seq = lambda n: lay.seq_order(n, rev)

    def ucol(name, b):
        off, w = cols[name]
        return pl.BlockSpec((blk, w), lambda n: (lay.row_block(b, seq(n)), off // w))

    in_specs, args = [], []
    for b in range(B):
        in_specs += [ucol("gla_q", b), ucol("gla_k", b), ucol("gla_v", b),
                     pl.BlockSpec((blk, SMALL_COLS), lambda n, b=b: (lay.row_block(b, seq(n)), 0))]
        args += [U, U, U, small]
        if final:
            in_specs.append(ucol("gla_gate", b))
            args.append(U)
    in_specs += [const((SMALL_COLS, qk_w)), const((1, qk_w))]
    args += [w_dec, b_dec]
    per_batch = pl.BlockSpec((B, blk, v_w), lambda n: (0, seq(n), 0))
    if final:
        in_specs += [const((1, dv)), per_batch]
        args += [norm_g, prev]
    return pl.pallas_call(
        functools.partial(_gla_kernel, nb=B, rev=rev, final=final, dk=dk, dv=dv),
        out_shape=jax.ShapeDtypeStruct((B, NS * blk, v_w), BF16 if final else F32),
        grid=(NS,),
        in_specs=in_specs,
        out_specs=per_batch,
        scratch_shapes=[pltpu.VMEM((B * GLA_HEADS, dv, dk), F32)],
        compiler_params=_cparams(("arbitrary",)),
        name="gla_bwd" if rev else "gla_fwd",
    )(*args)


def _conv_kernel(x_ref, xp_ref, xn_ref, bc_ref, bcp_ref, bcn_ref, wx_ref, bx_ref, wbc_ref, bbc_ref,
                 xo_ref, bco_ref, *, n_ctx_total, NC, NL):
    i = pl.program_id(0)
    in_ctx = i < n_ctx_total
    p = jnp.where(in_ctx, i % NC, jnp.maximum(i - n_ctx_total, 0) % NL)
    seg = jnp.where(in_ctx, NC, NL)
    has_left = (p > 0).astype(F32)
    has_right = (p < seg - 1).astype(F32)
    blk = x_ref.shape[0]
    pad = SSD_CONV // 2
    edge = 8
    assert pad <= edge <= HALO_ROWS

    r = lax.broadcasted_iota(jnp.int32, (blk, blk), 0)
    c = lax.broadcasted_iota(jnp.int32, (blk, blk), 1)
    re = lax.broadcasted_iota(jnp.int32, (edge, HALO_ROWS), 0)
    ce = lax.broadcasted_iota(jnp.int32, (edge, HALO_ROWS), 1)
    taps = [j - pad for j in range(SSD_CONV)]
    shift = {s: jnp.where(c == r + s, 1.0, 0.0).astype(BF16) for s in taps if s != 0}
    head = {s: (jnp.where(ce == HALO_ROWS + re + s, 1.0, 0.0) * has_left).astype(BF16) for s in taps if s < 0}
    tail = {s: (jnp.where(ce == re + s - edge, 1.0, 0.0) * has_right).astype(BF16) for s in taps if s > 0}

    def run(m_ref, p_ref, n_ref, w_ref, b_ref, o_ref):
        main = m_ref[...]
        acc = b_ref[...] + w_ref[pad:pad + 1, :] * main.astype(F32)
        for j, s in enumerate(taps):
            if s != 0:
                acc = acc + w_ref[j:j + 1, :] * _dot(shift[s], main)
        first, last = acc[0:edge], acc[blk - edge:blk]
        for j, s in enumerate(taps):
            if s < 0:
                first = first + w_ref[j:j + 1, :] * _dot(head[s], p_ref[...])
            if s > 0:
                last = last + w_ref[j:j + 1, :] * _dot(tail[s], n_ref[...])
        o_ref[0:edge, :] = _silu(first).astype(o_ref.dtype)
        o_ref[edge:blk - edge, :] = _silu(acc[edge:blk - edge]).astype(o_ref.dtype)
        o_ref[blk - edge:blk, :] = _silu(last).astype(o_ref.dtype)

    run(x_ref, xp_ref, xn_ref, wx_ref, bx_ref, xo_ref)
    run(bc_ref, bcp_ref, bcn_ref, wbc_ref, bbc_ref, bco_ref)


def _conv(lay, U, cols, wx, bx, wbc, bbc):
    blk = SEQ_BLOCK
    nb = lay.R // blk
    hb = blk // HALO_ROWS
    nh = lay.R // HALO_ROWS
    xw = cols["ssd_x"][1]
    bcw = cols["ssd_bc"][1]
    xc, bcc = cols["ssd_x"][0] // xw, cols["ssd_bc"][0] // bcw
    prev = lambda i: jnp.maximum(i * hb - 1, 0)
    nxt = lambda i: jnp.minimum((i + 1) * hb, nh - 1)
    const = lambda shape: pl.BlockSpec(shape, lambda i: (0,) * len(shape))
    return pl.pallas_call(
        functools.partial(_conv_kernel, n_ctx_total=lay.B * lay.NC, NC=lay.NC, NL=lay.NL),
        out_shape=(jax.ShapeDtypeStruct((lay.R, xw), BF16), jax.ShapeDtypeStruct((lay.R, bcw), BF16)),
        grid=(nb,),
        in_specs=[
            pl.BlockSpec((blk, xw), lambda i: (i, xc)),
            pl.BlockSpec((HALO_ROWS, xw), lambda i: (prev(i), xc)),
            pl.BlockSpec((HALO_ROWS, xw), lambda i: (nxt(i), xc)),
            pl.BlockSpec((blk, bcw), lambda i: (i, bcc)),
            pl.BlockSpec((HALO_ROWS, bcw), lambda i: (prev(i), bcc)),
            pl.BlockSpec((HALO_ROWS, bcw), lambda i: (nxt(i), bcc)),
            const((SSD_CONV, xw)), const((1, xw)), const((SSD_CONV, bcw)), const((1, bcw)),
        ],
        out_specs=(pl.BlockSpec((blk, xw), lambda i: (i, 0)), pl.BlockSpec((blk, bcw), lambda i: (i, 0))),
        compiler_params=_cparams(("arbitrary",)),
        name="ssd_conv",
    )(U, U, U, U, U, U, wx, bx, wbc, bbc)


def _ssd_kernel(*refs, nb, rev, final):
    per_b = 4 if final else 3
    seqs = [refs[i * per_b:(i + 1) * per_b] for i in range(nb)]
    rest = refs[nb * per_b:]
    if final:
        dtb_ref, apad_ref, e_ref, dsk_ref, ng_ref, prev_ref, o_ref, st_ref, y_ref = rest
    else:
        dtb_ref, apad_ref, e_ref, o_ref, st_ref = rest
        y_ref = o_ref
    blk, inner = seqs[0][0].shape
    N = SSD_STATE
    P = SSD_HEAD_DIM
    C = SSD_CHUNK
    gw = inner // SSD_GROUPS
    pair = 2 * P

    @pl.when(pl.program_id(0) == 0)
    def _():
        st_ref[...] = jnp.zeros_like(st_ref)

    tri = _block_tri(blk, C, rev)
    e = e_ref[...]
    dt_fulls, cs_fulls = [], []
    for bi in range(nb):
        dt16 = _softplus(seqs[bi][2][...] + dtb_ref[...])
        cs16 = _dot_exact_lhs(tri, dt16 * apad_ref[...])
        dt_fulls.append(_dot_spread(dt16, e))
        cs_fulls.append(_dot_spread(cs16, e))

    ri = lax.broadcasted_iota(jnp.int32, (C, pair), 0)
    ci = lax.broadcasted_iota(jnp.int32, (C, pair), 1)
    cm = jnp.where(ci >= P, ci - P, ci)
    eye2 = cm == ri
    causal2 = (cm >= ri) if rev else (cm <= ri)
    left = ci < P
    zero = jnp.zeros((C, pair), BF16)

    n_chunks = blk // C
    for step in range(n_chunks):
        c = n_chunks - 1 - step if rev else step
        rs = slice(c * C, (c + 1) * C)
        last = c * C if rev else (c + 1) * C - 1
        units = [(g, bi) for g in range(SSD_GROUPS) for bi in range(nb)]
        held = []
        for g, bi in units:
            gs = slice(g * gw, (g + 1) * gw)
            x_ref, bc_ref = seqs[bi][:2]
            bm = bc_ref[rs, g * N:(g + 1) * N]
            cmat = bc_ref[rs, SSD_GROUPS * N + g * N:SSD_GROUPS * N + (g + 1) * N]
            cb2 = _dot_nt(cmat, jnp.concatenate([bm, bm], axis=0))
            cs_g = cs_fulls[bi][rs, gs]
            cs_last = cs_fulls[bi][last:last + 1, gs]
            xdt = x_ref[rs, gs].astype(F32) * dt_fulls[bi][rs, gs]
            st = st_ref[bi * SSD_GROUPS + g]
            y_inter = _dot(cmat, st.astype(BF16)) * jnp.exp(cs_g)
            w = (xdt * jnp.exp(cs_last - cs_g)).astype(BF16)
            st_ref[bi * SSD_GROUPS + g] = st * jnp.exp(cs_last) + _dot_tn(bm, w)
            held.append((cb2, cs_g, y_inter, xdt.astype(BF16)))
        for (g, bi), (cb2, cs_g, y_inter, xdt_b) in zip(units, held):
            x_ref = seqs[bi][0]
            for pr in range(gw // pair):
                ps = slice(pr * pair, (pr + 1) * pair)
                col = cs_g[:, ps]
                rowv = jnp.sum(jnp.where(eye2, col, 0.0), axis=0, keepdims=True)
                lmat = jnp.exp(jnp.where(causal2, col - rowv, -jnp.inf))
                mm = (lmat * cb2).astype(BF16)
                xp = xdt_b[:, ps]
                rhs = jnp.concatenate([jnp.where(left, xp, zero), jnp.where(left, zero, xp)], axis=0)
                y = _dot(mm, rhs) + y_inter[:, ps]
                os_ = slice(g * gw + pr * pair, g * gw + (pr + 1) * pair)
                if final:
                    y_ref[bi, rs, os_] = (y + prev_ref[bi, rs, os_]
                                          + x_ref[rs, os_].astype(F32) * dsk_ref[:, os_])
                else:
                    y_ref[bi, rs, os_] = y

    if final:
        rc = 64
        for bi in range(nb):
            z_ref = seqs[bi][3]
            for r in range(blk // rc):
                rs = slice(r * rc, (r + 1) * rc)
                for g in range(SSD_GROUPS):
                    gs = slice(g * gw, (g + 1) * gw)
                    yg = y_ref[bi, rs, gs] * _silu(z_ref[rs, gs].astype(F32))
                    ms = jnp.mean(yg * yg, axis=-1, keepdims=True)
                    o_ref[bi, rs, gs] = (yg * lax.rsqrt(ms + EPS) * ng_ref[:, gs]).astype(o_ref.dtype)


def _ssd_dir(lay, xc, bcc, small, U, cols, dtb, apad, expand, rev, dsk=None, norm_g=None, prev=None):
    B, NS, blk = lay.B, lay.NS, SEQ_BLOCK
    inner = xc.shape[1]
    bcw = bcc.shape[1]
    final = prev is not None
    seq = lambda n: lay.seq_order(n, rev)
    const = lambda shape: pl.BlockSpec(shape, lambda n: (0,) * len(shape))

    def rows(w, b, col=0):
        return pl.BlockSpec((blk, w), lambda n: (lay.row_block(b, seq(n)), col))

    in_specs, args = [], []
    for b in range(B):
        in_specs += [rows(inner, b), rows(bcw, b), rows(SMALL_COLS, b)]
        args += [xc, bcc, small]
        if final:
            zc = cols["ssd_z"]
            in_specs.append(rows(inner, b, zc[0] // zc[1]))
            args.append(U)
    in_specs += [const((1, SMALL_COLS)), const((1, SMALL_COLS)), const((SMALL_COLS, inner))]
    args += [dtb, apad, expand]
    per_batch = pl.BlockSpec((B, blk, inner), lambda n: (0, seq(n), 0))
    if final:
        in_specs += [const((1, inner)), const((1, inner)), per_batch]
        args += [dsk, norm_g, prev]
    return pl.pallas_call(
        functools.partial(_ssd_kernel, nb=B, rev=rev, final=final),
        out_shape=jax.ShapeDtypeStruct((B, NS * blk, inner), BF16 if final else F32),
        grid=(NS,),
        in_specs=in_specs,
        out_specs=per_batch,
        scratch_shapes=[pltpu.VMEM((B * SSD_GROUPS, SSD_STATE, inner // SSD_GROUPS), F32)]
        + ([pltpu.VMEM((B, blk, inner), F32)] if final else []),
        compiler_params=_cparams(("arbitrary",)),
        name="ssd_bwd" if rev else "ssd_fwd",
    )(*args)


def _rope_tables(CTX, SEQ):
    hd = ATT_HEAD_DIM
    pairs = hd // 4
    rows = SEQ // GRID_W
    row = jnp.repeat(jnp.arange(rows, dtype=F32), GRID_W)
    col = jnp.tile(jnp.arange(GRID_W, dtype=F32), rows)
    inv = jnp.exp(-math.log(ROPE_THETA) * jnp.arange(pairs, dtype=F32) / pairs)
    ar = row[:, None] * inv
    ac = col[:, None] * inv
    ang = jnp.concatenate([ar, ar, ac, ac], axis=-1)
    cos, sin = jnp.cos(ang), jnp.sin(ang)
    first = (jnp.arange(hd) % (2 * pairs)) < pairs
    sa = jnp.where(first, -sin, 0.0)
    sb = jnp.where(first, 0.0, sin)
    ones = jnp.ones((CTX, hd), F32)
    zeros = jnp.zeros((CTX, hd), F32)
    return (jnp.concatenate([ones, cos]), jnp.concatenate([zeros, sa]), jnp.concatenate([zeros, sb]))


def _column_plan(D):
    mix = D // 2
    gla_v = mix
    gla_qk = mix // 2
    ssd_inner = mix
    ssd_bc = 2 * SSD_GROUPS * SSD_STATE
    order = [("gla_q", gla_qk), ("gla_k", gla_qk), ("gla_v", gla_v), ("gla_gate", gla_v),
             ("ssd_z", ssd_inner), ("ssd_x", ssd_inner), ("ssd_bc", ssd_bc)]
    cols, off = {}, 0
    for name, w in order:
        assert off % w == 0
        cols[name] = (off, w)
        off += w
    return cols, off


def _source_columns(D):
    mix = D // 2
    att_kv = ATT_KV_HEADS * ATT_HEAD_DIM
    ssd_bc = SSD_GROUPS * SSD_STATE
    heads = mix // SSD_HEAD_DIM
    widths = [("att_q", mix), ("att_k", att_kv), ("att_v", att_kv), ("gla_q", mix // 2), ("gla_k", mix // 2),
              ("gla_v", mix), ("gla_gate", mix), ("gla_lr", 2 * GLA_LOWRANK), ("ssd_z", mix),
              ("ssd_x", mix), ("ssd_bc", 2 * ssd_bc), ("ssd_dt", 2 * heads), ("gates", N_BRANCH * D)]
    src, off = {}, 0
    for name, w in widths:
        src[name] = (off, w)
        off += w
    return src, off


def kernel(x, c, ctx, c_ctx, norm1, norm2, w_ada, b_ada, w_in, att_q_norm, att_k_norm, gla_w_decay,
           gla_b_decay, gla_norm, ssd_conv_w, ssd_conv_b, ssd_dt_bias, ssd_a_log, ssd_d, ssd_norm,
           w_branch, w_out, w_ff1, w_ff2, final_norm):
    B, SEQ, D = x.shape
    CTX = ctx.shape[1]
    depth = w_in.shape[0]
    lay = _Layout(B, CTX, SEQ, D)
    cols, n_main = _column_plan(D)
    src, n_src = _source_columns(D)
    assert n_src == w_in.shape[2]
    mix = D // 2
    ssd_heads = mix // SSD_HEAD_DIM
    lr_w = 2 * GLA_LOWRANK
    assert lr_w + 2 * ssd_heads <= SMALL_COLS

    X = jnp.concatenate([ctx.reshape(B * CTX, D), x.reshape(B * SEQ, D)], axis=0)
    cvec = jnp.concatenate([c, c_ctx[None, :], jnp.zeros((MOD_ROWS - B - 1, D), F32)], axis=0)
    mods = _ada(cvec, w_ada, b_ada).reshape(depth, MOD_ROWS, 1, N_MOD * D)
    rope = _rope_tables(CTX, SEQ)

    expand = np.zeros((2, SMALL_COLS, mix), np.float32)
    for d in range(2):
        for h in range(ssd_heads):
            expand[d, lr_w + d * ssd_heads + h, h * SSD_HEAD_DIM:(h + 1) * SSD_HEAD_DIM] = 1.0
    expand = jnp.asarray(expand, BF16)

    lead = ["att_q", "att_k", "att_v", "gla_q", "gla_k", "gla_v", "gla_gate"]
    assert [src[n][0] for n in lead] == list(np.cumsum([0] + [src[n][1] for n in lead[:-1]]))
    assert [cols[n][0] for n in lead[3:]] == [src[n][0] - src["gla_q"][0] for n in lead[3:]]
    n_a = src["gla_gate"][0] + src["gla_gate"][1]
    w_in_b = w_in.astype(BF16)
    take = lambda name: w_in_b[:, :, src[name][0]:src[name][0] + src[name][1]]
    w_b = jnp.concatenate([take("ssd_z"), take("ssd_x"), take("ssd_bc")], axis=2)
    assert cols["ssd_z"][0] == n_a - src["gla_q"][0] and cols["ssd_x"][0] == cols["ssd_z"][0] + mix
    w_small = jnp.concatenate([take("gla_lr"), take("ssd_dt"),
                               jnp.zeros((depth, D, SMALL_COLS - lr_w - 2 * ssd_heads), BF16)], axis=2)
    w_gate = take("gates")
    wbr_b, wo_b = w_branch[0].astype(BF16), w_out[0].astype(BF16)
    w1_b, w2_b = w_ff1[0].astype(BF16), w_ff2[0].astype(BF16)
    w_branch_rows = w_branch.reshape(depth, N_BRANCH * mix, D)

    out = None
    for l in range(depth):
        mod = mods[l]
        g1 = norm1[l][None, :]

        U, small, qt, kk, vt, H = _inproj(lay, X, mod, g1, w_in_b, n_a, w_b, w_small, att_q_norm[l][None, :],
                                          att_k_norm[l][None, :], rope, l)

        o_att = _attention(lay, qt, kk, vt)

        gla_out = None
        for d in range(2):
            wd = jnp.zeros((SMALL_COLS, mix // 2), F32).at[d * GLA_LOWRANK:(d + 1) * GLA_LOWRANK].set(
                gla_w_decay[l, d]).astype(BF16)
            bd = gla_b_decay[l, d][None, :]
            if d == 0:
                gla_out = _gla_dir(lay, U, small, cols, wd, bd, rev=False)
            else:
                gla_out = _gla_dir(lay, U, small, cols, wd, bd, rev=True,
                                   norm_g=gla_norm[l][None, :], prev=gla_out)
        o_gla = gla_out

        cw, cbias = ssd_conv_w[l], ssd_conv_b[l]
        xc, bcc = _conv(lay, U, cols, cw[:, :mix], cbias[None, :mix], cw[:, mix:], cbias[None, mix:])
        ssd_out = None
        for d in range(2):
            lanes = slice(lr_w + d * ssd_heads, lr_w + (d + 1) * ssd_heads)
            dtb = jnp.zeros((1, SMALL_COLS), F32).at[0, lanes].set(ssd_dt_bias[l, d])
            apad = jnp.zeros((1, SMALL_COLS), F32).at[0, lanes].set(-jnp.exp(ssd_a_log[l, d]))
            if d == 0:
                ssd_out = _ssd_dir(lay, xc, bcc, small, U, cols, dtb, apad, expand[d], rev=False)
            else:
                ssd_out = _ssd_dir(lay, xc, bcc, small, U, cols, dtb, apad, expand[d], rev=True,
                                   dsk=jnp.repeat(ssd_d[l], SSD_HEAD_DIM)[None, :],
                                   norm_g=ssd_norm[l][None, :], prev=ssd_out)
        o_ssd = ssd_out

        last = l == depth - 1
        skip = B * CTX if last else 0
        if last:
            m = _merge(lay, H, o_att, o_gla, o_ssd, w_gate, wbr_b, l, skip)
            out = _mlp(lay, X, m, wo_b, mod, norm2[l][None, :], w1_b, w2_b, l, skip,
                       final_gain=final_norm[None, :])
        else:
            m, wbr_next, wo_next = _merge(lay, H, o_att, o_gla, o_ssd, w_gate, wbr_b, l, skip,
                                          cast_next=(w_branch_rows, w_out))
            X, w1_b, w2_b = _mlp(lay, X, m, wo_b, mod, norm2[l][None, :], w1_b, w2_b, l, skip,
                                 cast_next=(w_ff1, w_ff2))
            wbr_b, wo_b = wbr_next, wo_next
    return out.reshape(B, SEQ, D)
```

```python
import functools
import math

import jax
import jax.numpy as jnp
import numpy as np
from jax import lax
from jax.experimental import pallas as pl
from jax.experimental.pallas import tpu as pltpu

F32 = jnp.float32
BF16 = jnp.bfloat16

EPS = 1e-6
GRID_W = 64
N_BRANCH = 3
N_MOD = 6
ATT_HEAD_DIM = 128
ATT_KV_HEADS = 2
ROPE_THETA = 10000.0
GLA_HEADS = 4
GLA_LOWRANK = 16
GLA_TAU = 16.0
GLA_CHUNK = 64
SSD_HEAD_DIM = 64
SSD_GROUPS = 2
SSD_STATE = 128
SSD_CONV = 5
SSD_CHUNK = 64

SEQ_BLOCK = 256
RESIDENT_TM = 256
MOD_ROWS = 8
SMALL_COLS = 128
HALO_ROWS = 16
ATT_ONES_ROWS = 16
ATT_PAIR_UNROLL = 4
VMEM_LIMIT = 56 * 1024 * 1024


def _cparams(sem):
    return pltpu.CompilerParams(dimension_semantics=sem, vmem_limit_bytes=VMEM_LIMIT)


def _dot(a, b):
    return jnp.dot(a, b, preferred_element_type=F32)


def _dot_nt(a, b):
    return lax.dot_general(a, b, (((1,), (1,)), ((), ())), preferred_element_type=F32)


def _dot_tn(a, b):
    return lax.dot_general(a, b, (((0,), (0,)), ((), ())), preferred_element_type=F32)


def _split2(x):
    hi = x.astype(BF16)
    lo = (x - hi.astype(F32)).astype(BF16)
    return hi, lo


def _dot_sum(a, x):
    hi, lo = _split2(x)
    return _dot(a, lo) + _dot(a, hi)


def _dot_spread(x, e):
    hi, lo = _split2(x)
    return _dot(lo, e) + _dot(hi, e)


def _silu(x):
    return x * jax.nn.sigmoid(x)


def _log1p_exp_neg_abs(x):
    return jnp.log(1.0 + jnp.exp(-jnp.abs(x)))


def _softplus(x):
    return jnp.maximum(x, 0.0) + _log1p_exp_neg_abs(x)


def _block_tri(n, chunk, rev):
    r = lax.broadcasted_iota(jnp.int32, (n, n), 0)
    c = lax.broadcasted_iota(jnp.int32, (n, n), 1)
    shift = chunk.bit_length() - 1
    assert chunk == 1 << shift
    same = jnp.right_shift(r, shift) == jnp.right_shift(c, shift)
    tri = (c >= r) if rev else (c <= r)
    return jnp.where(same, jnp.where(tri, 1.0, 0.0), 0.0).astype(BF16)


class _Layout:
    def __init__(self, B, CTX, SEQ, D):
        assert CTX % SEQ_BLOCK == 0 and SEQ % SEQ_BLOCK == 0
        self.B, self.CTX, self.SEQ, self.D = B, CTX, SEQ, D
        self.R = B * (CTX + SEQ)
        self.NC = CTX // SEQ_BLOCK
        self.NL = SEQ // SEQ_BLOCK
        self.NS = self.NC + self.NL
        self.tm = next(t for t in (512, 256) if (B * CTX) % t == 0 and SEQ % t == 0)
        assert B + 1 <= MOD_ROWS

    def row_block(self, b, s):
        return jnp.where(s < self.NC, b * self.NC + s, self.B * self.NC + b * self.NL + (s - self.NC))

    def batch_seq(self, i):
        nctx = self.B * self.NC
        j = jnp.maximum(i - nctx, 0)
        return (jnp.where(i < nctx, i // self.NC, j // self.NL),
                jnp.where(i < nctx, i % self.NC, self.NC + j % self.NL))

    def seq_order(self, n, rev):
        if not rev:
            return n
        return jnp.where(n < self.NC, self.NC - 1 - n, self.NC + self.NL - 1 - (n - self.NC))

    def mod_row(self, i, tm):
        nctx = (self.B * self.CTX) // tm
        return jnp.where(i < nctx, self.B, (i - nctx) // (self.SEQ // tm))


def _ada_kernel(c_ref, w_ref, b_ref, o_ref):
    s = _silu(c_ref[...]).astype(BF16)
    o_ref[0] = _dot(s, w_ref[0].astype(BF16)) + b_ref[0]


def _ada(cvec, w_ada, b_ada):
    depth, D, n = w_ada.shape
    tn = 1024
    return pl.pallas_call(
        _ada_kernel,
        out_shape=jax.ShapeDtypeStruct((depth, MOD_ROWS, n), F32),
        grid=(depth, n // tn),
        in_specs=[
            pl.BlockSpec((MOD_ROWS, D), lambda l, j: (0, 0)),
            pl.BlockSpec((1, D, tn), lambda l, j: (l, 0, j)),
            pl.BlockSpec((1, 1, tn), lambda l, j: (l, 0, j)),
        ],
        out_specs=pl.BlockSpec((1, MOD_ROWS, tn), lambda l, j: (l, 0, j)),
        compiler_params=_cparams(("arbitrary", "arbitrary")),
        name="ada",
    )(cvec, w_ada, b_ada.reshape(depth, 1, n))


def _norm_mod(x_ref, g_ref, mod_ref, k, h_ref):
    D = x_ref.shape[-1]
    shift = mod_ref[:, k * D:(k + 1) * D]
    a = g_ref[...] * (1.0 + mod_ref[:, (k + 1) * D:(k + 2) * D])
    rc = 64

    def body(r, carry):
        sl = pl.ds(pl.multiple_of(r * rc, rc), rc)
        xf = x_ref[sl, :]
        ms = jnp.mean(xf * xf, axis=-1, keepdims=True)
        h_ref[sl, :] = (xf * lax.rsqrt(ms + EPS) * a + shift).astype(h_ref.dtype)
        return carry

    lax.fori_loop(0, x_ref.shape[0] // rc, body, 0)


def _resident(shape, layer=None):
    if layer is None:
        return pl.BlockSpec(shape, lambda *_: (0,) * len(shape), pipeline_mode=pl.Buffered(1))
    return pl.BlockSpec((None,) + tuple(shape), lambda *_: (layer,) + (0,) * len(shape),
                        pipeline_mode=pl.Buffered(1))


def _inproj_kernel(x_ref, mod_ref, g_ref, wa_ref, wb_ref, ws_ref, qg_ref, kg_ref, cos_ref, sa_ref, sb_ref,
                   u_ref, small_ref, qt_ref, ko_ref, vt_ref, h_ref, *, tn, att_q, att_kv):
    _norm_mod(x_ref, g_ref, mod_ref, 0, h_ref)
    h = h_ref[...]
    hd = ATT_HEAD_DIM
    cos, sa, sb = cos_ref[...], sa_ref[...], sb_ref[...]

    def prep(y, g):
        y = y * lax.rsqrt(jnp.mean(y * y, axis=-1, keepdims=True) + EPS) * g
        return y * cos + pltpu.roll(y, hd - hd // 4, 1) * sa + pltpu.roll(y, hd // 4, 1) * sb

    q_scale = (hd ** -0.5) * math.log2(math.e)
    for j in range(att_q // tn):
        q = _dot(h, wa_ref[:, j * tn:(j + 1) * tn])
        for hh in range(tn // hd):
            qh = prep(q[:, hh * hd:(hh + 1) * hd], qg_ref[...]) * q_scale
            qt_ref[j * (tn // hd) + hh] = qh.T.astype(qt_ref.dtype)
    kv = _dot(h, wa_ref[:, att_q:att_q + 2 * att_kv])
    for g in range(att_kv // hd):
        sl = slice(g * hd, (g + 1) * hd)
        ko_ref[:, sl] = prep(kv[:, sl], kg_ref[...]).astype(ko_ref.dtype)
        vt_ref[g, 0:hd, :] = kv[:, att_kv + g * hd:att_kv + (g + 1) * hd].T.astype(vt_ref.dtype)
        vt_ref[g, hd:, :] = jnp.ones((ATT_ONES_ROWS, x_ref.shape[0]), vt_ref.dtype)

    small_ref[...] = _dot(h, ws_ref[...])
    a0 = att_q + 2 * att_kv
    na = wa_ref.shape[1] - a0
    for j in range(na // tn):
        u_ref[:, j * tn:(j + 1) * tn] = _dot(h, wa_ref[:, a0 + j * tn:a0 + (j + 1) * tn]).astype(u_ref.dtype)
    for j in range(wb_ref.shape[1] // tn):
        u_ref[:, na + j * tn:na + (j + 1) * tn] = _dot(h, wb_ref[:, j * tn:(j + 1) * tn]).astype(u_ref.dtype)


def _inproj(lay, X, mod, g, w_a, n_a, w_b, w_small, q_gain, k_gain, rope, l):
    D = w_a.shape[1]
    n_b = w_b.shape[2]
    tm = RESIDENT_TM
    assert tm == SEQ_BLOCK
    hd = ATT_HEAD_DIM
    att_q = D // 2
    att_kv = ATT_KV_HEADS * hd
    heads = att_q // hd
    n_u = n_a - att_q - 2 * att_kv + n_b
    B, NS = lay.B, lay.NS
    bs = lay.batch_seq
    tab = pl.BlockSpec((tm, hd), lambda i: (bs(i)[1], 0))
    return pl.pallas_call(
        functools.partial(_inproj_kernel, tn=512, att_q=att_q, att_kv=att_kv),
        out_shape=(jax.ShapeDtypeStruct((lay.R, n_u), BF16),
                   jax.ShapeDtypeStruct((lay.R, SMALL_COLS), F32),
                   jax.ShapeDtypeStruct((lay.R // tm, heads, hd, tm), BF16),
                   jax.ShapeDtypeStruct((B, NS * tm, att_kv), BF16),
                   jax.ShapeDtypeStruct((B, ATT_KV_HEADS, NS, hd + ATT_ONES_ROWS, tm), BF16),
                   jax.ShapeDtypeStruct((lay.R, D), BF16)),
        grid=(lay.R // tm,),
        in_specs=[
            pl.BlockSpec((tm, D), lambda i: (i, 0)),
            pl.BlockSpec((None, 1, N_MOD * D), lambda i: (lay.mod_row(i, tm), 0, 0)),
            _resident((1, D)),
            _resident((D, n_a), l),
            _resident((D, n_b), l),
            _resident((D, SMALL_COLS), l),
            _resident((1, hd)), _resident((1, hd)),
            tab, tab, tab,
        ],
        out_specs=(pl.BlockSpec((tm, n_u), lambda i: (i, 0)),
                   pl.BlockSpec((tm, SMALL_COLS), lambda i: (i, 0)),
                   pl.BlockSpec((None, heads, hd, tm), lambda i: (i, 0, 0, 0)),
                   pl.BlockSpec((None, tm, att_kv), lambda i: (*bs(i), 0)),
                   pl.BlockSpec((None, ATT_KV_HEADS, None, hd + ATT_ONES_ROWS, tm),
                                lambda i: (bs(i)[0], 0, bs(i)[1], 0, 0)),
                   pl.BlockSpec((tm, D), lambda i: (i, 0))),
        compiler_params=_cparams(("arbitrary",)),
        name="in_proj",
    )(X, mod, g, w_a, w_b, w_small, q_gain, k_gain, *rope)


def _slab_count(n_steps, *row_counts):
    n = 1
    while 2 * n <= n_steps and all(r % (2 * n * 16) == 0 for r in row_counts):
        n *= 2
    return n


def _merge_kernel(h_ref, oa_ref, og_ref, os_ref, wg_ref, wb_ref, *rest, tn, cast_next):
    if cast_next:
        c1_ref, c2_ref, m_ref, n1_ref, n2_ref = rest
        n1_ref[...] = c1_ref[...].astype(n1_ref.dtype)
        n2_ref[...] = c2_ref[...].astype(n2_ref.dtype)
    else:
        (m_ref,) = rest
    h = h_ref[...]
    D = m_ref.shape[1]
    for j in range(D // tn):
        acc = None
        for b, o_ref in enumerate((oa_ref, og_ref, os_ref)):
            gate = _dot(h, wg_ref[:, b * D + j * tn:b * D + (j + 1) * tn])
            t = jax.nn.sigmoid(gate) * _dot(o_ref[...], wb_ref[b, :, j * tn:(j + 1) * tn])
            acc = t if acc is None else acc + t
        m_ref[:, j * tn:(j + 1) * tn] = acc.astype(m_ref.dtype)


def _merge(lay, H, o_att, o_gla, o_ssd, w_gate, w_branch, l, skip_rows, cast_next=None):
    D = lay.D
    W = o_att.shape[1]
    tm = RESIDENT_TM
    assert tm == SEQ_BLOCK
    row0 = skip_rows // tm
    rows = lambda i: (i + row0, 0)
    seq_rows = pl.BlockSpec((None, tm, W), lambda i: (*lay.batch_seq(i + row0), 0))
    n_i = lay.R // tm - row0
    in_specs = [
        pl.BlockSpec((tm, D), rows),
        pl.BlockSpec((tm, W), rows), seq_rows, seq_rows,
        _resident(w_gate.shape[1:], l), _resident(w_branch.shape),
    ]
    args = [H, o_att, o_gla, o_ssd, w_gate, w_branch]
    out_shape = [jax.ShapeDtypeStruct((lay.R, D), BF16)]
    out_specs = [pl.BlockSpec((tm, D), rows)]
    if cast_next is not None:
        n_slabs = _slab_count(n_i, N_BRANCH * W, D)
        r1, r2 = N_BRANCH * W // n_slabs, D // n_slabs
        slab = lambda i: jnp.minimum(i, n_slabs - 1)
        in_specs += [pl.BlockSpec((None, r1, D), lambda i: (l + 1, slab(i), 0)),
                     pl.BlockSpec((None, r2, D), lambda i: (l + 1, slab(i), 0))]
        args += list(cast_next)
        out_shape += [jax.ShapeDtypeStruct((N_BRANCH * W, D), BF16), jax.ShapeDtypeStruct((D, D), BF16)]
        out_specs += [pl.BlockSpec((r1, D), lambda i: (slab(i), 0)), pl.BlockSpec((r2, D), lambda i: (slab(i), 0))]
    res = pl.pallas_call(
        functools.partial(_merge_kernel, tn=512, cast_next=cast_next is not None),
        out_shape=tuple(out_shape),
        grid=(n_i,),
        in_specs=in_specs,
        out_specs=tuple(out_specs),
        compiler_params=_cparams(("arbitrary",)),
        name="merge",
    )(*args)
    if cast_next is None:
        return res[0]
    return res[0], res[1].reshape(N_BRANCH, W, D), res[2]


def _mlp_kernel(x_ref, m_ref, wo_ref, mod_ref, g_ref, w1_ref, w2_ref, *rest, final, cast_next):
    rest = list(rest)
    fg_ref = rest.pop(0) if final else None
    if cast_next:
        c1_ref, c2_ref = rest.pop(0), rest.pop(0)
        o_ref, n1_ref, n2_ref, x1_ref, h_ref = rest
        n1_ref[...] = c1_ref[...].astype(n1_ref.dtype)
        n2_ref[...] = c2_ref[...].astype(n2_ref.dtype)
    else:
        o_ref, x1_ref, h_ref = rest
    k = pl.program_id(1)
    D = x_ref.shape[-1]

    @pl.when(k == 0)
    def _():
        tn = 512
        for j in range(D // tn):
            sl = slice(j * tn, (j + 1) * tn)
            x1_ref[:, sl] = x_ref[:, sl] + mod_ref[:, 2 * D + j * tn:2 * D + (j + 1) * tn] * _dot(
                m_ref[...], wo_ref[:, sl])
        _norm_mod(x1_ref, g_ref, mod_ref, 3, h_ref)
        o_ref[...] = jnp.zeros_like(o_ref)

    f = jnp.square(jnp.maximum(_dot(h_ref[...], w1_ref[...]), 0.0)).astype(BF16)
    o_ref[...] += _dot(f, w2_ref[...])

    @pl.when(k == pl.num_programs(1) - 1)
    def _():
        gate = mod_ref[:, 5 * D:6 * D]
        rc = 64

        def body(r, carry):
            sl = pl.ds(pl.multiple_of(r * rc, rc), rc)
            y = x1_ref[sl, :] + gate * o_ref[sl, :]
            if final:
                ms = jnp.mean(y * y, axis=-1, keepdims=True)
                y = y * lax.rsqrt(ms + EPS) * fg_ref[...]
            o_ref[sl, :] = y
            return carry

        lax.fori_loop(0, x_ref.shape[0] // rc, body, 0)


def _mlp(lay, X, m, w_out, mod, g, w1, w2, l, skip_rows, final_gain=None, cast_next=None):
    D, FF = w1.shape
    tm, tf = lay.tm, 1024
    row0 = skip_rows // tm
    final = final_gain is not None
    out_rows = lay.R - skip_rows if final else lay.R
    out_off = 0 if final else row0
    rows = lambda i, k: (i + row0, 0)
    n_i, n_k = lay.R // tm - row0, FF // tf
    in_specs = [
        pl.BlockSpec((tm, D), rows),
        pl.BlockSpec((tm, D), rows),
        _resident((D, D)),
        pl.BlockSpec((None, 1, N_MOD * D), lambda i, k: (lay.mod_row(i + row0, tm), 0, 0)),
        _resident((1, D)),
        pl.BlockSpec((D, tf), lambda i, k: (0, k)),
        pl.BlockSpec((tf, D), lambda i, k: (k, 0)),
    ]
    args = [X, m, w_out, mod, g, w1, w2]
    out_shape = [jax.ShapeDtypeStruct((out_rows, D), F32)]
    out_specs = [pl.BlockSpec((tm, D), lambda i, k: (i + out_off, 0))]
    if final:
        in_specs.append(_resident((1, D)))
        args.append(final_gain)
    if cast_next is not None:
        n_slabs = _slab_count(n_i * n_k, D, FF)
        r1, r2 = D // n_slabs, FF // n_slabs
        slab = lambda i, k: jnp.minimum(i * n_k + k, n_slabs - 1)
        in_specs += [pl.BlockSpec((None, r1, FF), lambda i, k: (l + 1, slab(i, k), 0)),
                     pl.BlockSpec((None, r2, D), lambda i, k: (l + 1, slab(i, k), 0))]
        args += list(cast_next)
        out_shape += [jax.ShapeDtypeStruct((D, FF), BF16), jax.ShapeDtypeStruct((FF, D), BF16)]
        out_specs += [pl.BlockSpec((r1, FF), lambda i, k: (slab(i, k), 0)),
                      pl.BlockSpec((r2, D), lambda i, k: (slab(i, k), 0))]
    res = pl.pallas_call(
        functools.partial(_mlp_kernel, final=final, cast_next=cast_next is not None),
        out_shape=tuple(out_shape),
        grid=(n_i, n_k),
        in_specs=in_specs,
        out_specs=tuple(out_specs),
        scratch_shapes=[pltpu.VMEM((tm, D), F32), pltpu.VMEM((tm, D), BF16)],
        compiler_params=_cparams(("arbitrary", "arbitrary")),
        name="mlp",
    )(*args)
    return res if cast_next is not None else res[0]


def _attn_kernel(qt_ref, k_ref, vt_ref, o_ref, m_ref, acc_ref, s0_ref, s1_ref, *, n_ctx_blocks):
    n = pl.program_id(2)
    heads, hd, tq = qt_ref.shape
    nblk, vrows, tk = vt_ref.shape
    n_kv = jnp.where(n < n_ctx_blocks, n_ctx_blocks, nblk)
    m_ref[...] = jnp.full(m_ref.shape, -1e30, F32)
    acc_ref[...] = jnp.zeros(acc_ref.shape, F32)

    def scores(c, s_ref):
        c = jnp.minimum(c, n_kv - 1)
        k = k_ref[pl.ds(pl.multiple_of(c * tk, tk), tk), :]
        for h in range(heads):
            s_ref[h] = _dot(k, qt_ref[h])

    def update(c, s_ref):
        vt = vt_ref[c]
        for h in range(heads):
            st = s_ref[h]
            m_old = m_ref[h]
            m_new = jnp.maximum(m_old, jnp.max(st, axis=0, keepdims=True))
            pt = jnp.exp2((st - m_new).astype(BF16))
            acc_ref[h] = jnp.exp2(m_old - m_new) * acc_ref[h] + _dot(vt, pt)
            m_ref[h] = m_new

    scores(0, s0_ref)

    def pairs(c0, n_pairs):
        for j in range(n_pairs):
            scores(c0 + 2 * j + 1, s1_ref)
            update(c0 + 2 * j, s0_ref)
            scores(c0 + 2 * j + 2, s0_ref)
            update(c0 + 2 * j + 1, s1_ref)

    n_pairs = n_kv // 2
    unroll = ATT_PAIR_UNROLL

    def body(i, carry):
        pairs(2 * unroll * i, unroll)
        return carry

    lax.fori_loop(0, n_pairs // unroll, body, 0)
    done = (n_pairs // unroll) * unroll
    piece = unroll // 2
    while piece >= 1:
        @pl.when(((n_pairs - done) & piece) != 0)
        def _(done=done, piece=piece):
            pairs(2 * done, piece)

        done = done + ((n_pairs - done) & piece)
        piece //= 2

    @pl.when(n_kv % 2 == 1)
    def _():
        update(n_kv - 1, s0_ref)

    for h in range(heads):
        acc = acc_ref[h]
        ot = acc[0:hd] / acc[hd:hd + 1]
        o_ref[:, h * hd:(h + 1) * hd] = ot.T.astype(o_ref.dtype)


def _attention(lay, qt, kk, vt):
    B, kvh, NS, vrows, blk = vt.shape
    _, heads, hd, _ = qt.shape
    group = heads // kvh
    return pl.pallas_call(
        functools.partial(_attn_kernel, n_ctx_blocks=lay.NC),
        out_shape=jax.ShapeDtypeStruct((lay.R, heads * hd), BF16),
        grid=(B, kvh, NS),
        in_specs=[
            pl.BlockSpec((None, group, hd, blk), lambda b, g, n: (lay.row_block(b, n), g, 0, 0)),
            pl.BlockSpec((None, NS * blk, hd), lambda b, g, n: (b, 0, g)),
            pl.BlockSpec((None, None, NS, vrows, blk), lambda b, g, n: (b, g, 0, 0, 0)),
        ],
        out_specs=pl.BlockSpec((blk, group * hd), lambda b, g, n: (lay.row_block(b, n), g)),
        scratch_shapes=[pltpu.VMEM((group, 1, blk), F32), pltpu.VMEM((group, vrows, blk), F32),
                        pltpu.VMEM((group, blk, blk), F32), pltpu.VMEM((group, blk, blk), F32)],
        compiler_params=_cparams(("arbitrary", "arbitrary", "arbitrary")),
        name="attention",
    )(qt, kk, vt)


def _gla_kernel(*refs, nb, rev, final, dk, dv):
    per_b = 5 if final else 4
    seqs = [refs[i * per_b:(i + 1) * per_b] for i in range(nb)]
    rest = refs[nb * per_b:]
    if final:
        wd_ref, bd_ref, ng_ref, prev_ref, o_ref, st_ref = rest
    else:
        wd_ref, bd_ref, o_ref, st_ref = rest
    blk = seqs[0][0].shape[0]
    heads = seqs[0][0].shape[1] // dk
    C = GLA_CHUNK

    @pl.when(pl.program_id(0) == 0)
    def _():
        st_ref[...] = jnp.zeros_like(st_ref)

    tri = _block_tri(blk, C, rev)
    bcums = []
    for bi in range(nb):
        z = _dot(seqs[bi][3][...].astype(BF16), wd_ref[...]) + bd_ref[...]
        la = (jnp.minimum(z, 0.0) - _log1p_exp_neg_abs(z)) * (1.0 / GLA_TAU)
        bcums.append(_dot_sum(tri, la))

    ri = lax.broadcasted_iota(jnp.int32, (C, C), 0)
    ci = lax.broadcasted_iota(jnp.int32, (C, C), 1)
    causal = (ci >= ri) if rev else (ci <= ri)

    n_chunks = blk // C
    for step in range(n_chunks):
        c = n_chunks - 1 - step if rev else step
        rs = slice(c * C, (c + 1) * C)
        last = c * C if rev else (c + 1) * C - 1
        units = [(h, bi) for h in range(heads) for bi in range(nb)]
        scores, inter = [], []
        for h, bi in units:
            ks = slice(h * dk, (h + 1) * dk)
            q_ref, k_ref, v_ref = seqs[bi][:3]
            b = bcums[bi][rs, ks]
            b_last = bcums[bi][last:last + 1, ks]
            q = q_ref[rs, ks].astype(F32) * (dk ** -0.5)
            k = k_ref[rs, ks].astype(F32)
            q_dec = (q * jnp.exp(b)).astype(BF16)
            k_dec = (k * jnp.exp(-b)).astype(BF16)
            k_st = (k * jnp.exp(b_last - b)).astype(BF16)
            st = st_ref[bi * heads + h]
            scores.append(_dot_nt(q_dec, k_dec))
            inter.append(_dot_nt(q_dec, st.astype(BF16)))
            st_ref[bi * heads + h] = st * jnp.exp(b_last) + _dot_tn(v_ref[rs, h * dv:(h + 1) * dv], k_st)
        for (h, bi), sc, it in zip(units, scores, inter):
            vs = slice(h * dv, (h + 1) * dv)
            att = jnp.where(causal, sc, 0.0).astype(BF16)
            o = _dot(att, seqs[bi][2][rs, vs]) + it
            if final:
                tot = o + prev_ref[bi, rs, vs]
                y = tot * lax.rsqrt(jnp.mean(tot * tot, axis=-1, keepdims=True) + EPS) * ng_ref[...]
                o_ref[bi, rs, vs] = (y * _silu(seqs[bi][4][rs, vs].astype(F32))).astype(o_ref.dtype)
            else:
                o_ref[bi, rs, vs] = o


def _gla_dir(lay, U, small, cols, w_dec, b_dec, rev, norm_g=None, prev=None):
    B, NS, blk = lay.B, lay.NS, SEQ_BLOCK
    qk_w = cols["gla_q"][1]
    v_w = cols["gla_v"][1]
    dk, dv = qk_w // GLA_HEADS, v_w // GLA_HEADS
    final = prev is not None
    const = lambda shape: pl.BlockSpec(shape, lambda n: (0,) * len(shape))
    seq = lambda n: lay.seq_order(n, rev)

    def ucol(name, b):
        off, w = cols[name]
        return pl.BlockSpec((blk, w), lambda n: (lay.row_block(b, seq(n)), off // w))

    in_specs, args = [], []
    for b in range(B):
        in_specs += [ucol("gla_q", b), ucol("gla_k", b), ucol("gla_v", b),
                     pl.BlockSpec((blk, SMALL_COLS), lambda n, b=b: (lay.row_block(b, seq(n)), 0))]
        args += [U, U, U, small]
        if final:
            in_specs.append(ucol("gla_gate", b))
            args.append(U)
    in_specs += [const((SMALL_COLS, qk_w)), const((1, qk_w))]
    args += [w_dec, b_dec]
    per_batch = pl.BlockSpec((B, blk, v_w), lambda n: (0, seq(n), 0))
    if final:
        in_specs += [const((1, dv)), per_batch]
        args += [norm_g, prev]
    return pl.pallas_call(
        functools.partial(_gla_kernel, nb=B, rev=rev, final=final, dk=dk, dv=dv),
        out_shape=jax.ShapeDtypeStruct((B, NS * blk, v_w), BF16 if final else F32),
        grid=(NS,),
        in_specs=in_specs,
        out_specs=per_batch,
        scratch_shapes=[pltpu.VMEM((B * GLA_HEADS, dv, dk), F32)],
        compiler_params=_cparams(("arbitrary",)),
        name="gla_bwd" if rev else "gla_fwd",
    )(*args)


def _conv_kernel(x_ref, xp_ref, xn_ref, bc_ref, bcp_ref, bcn_ref, wx_ref, bx_ref, wbc_ref, bbc_ref,
                 xo_ref, bco_ref, *, n_ctx_total, NC, NL):
    i = pl.program_id(0)
    in_ctx = i < n_ctx_total
    p = jnp.where(in_ctx, i % NC, jnp.maximum(i - n_ctx_total, 0) % NL)
    seg = jnp.where(in_ctx, NC, NL)
    has_left = (p > 0).astype(F32)
    has_right = (p < seg - 1).astype(F32)
    blk = x_ref.shape[0]
    pad = SSD_CONV // 2
    edge = 8
    assert pad <= edge <= HALO_ROWS

    r = lax.broadcasted_iota(jnp.int32, (blk, blk), 0)
    c = lax.broadcasted_iota(jnp.int32, (blk, blk), 1)
    re = lax.broadcasted_iota(jnp.int32, (edge, HALO_ROWS), 0)
    ce = lax.broadcasted_iota(jnp.int32, (edge, HALO_ROWS), 1)
    taps = [j - pad for j in range(SSD_CONV)]
    shift = {s: jnp.where(c == r + s, 1.0, 0.0).astype(BF16) for s in taps if s != 0}
    head = {s: (jnp.where(ce == HALO_ROWS + re + s, 1.0, 0.0) * has_left).astype(BF16) for s in taps if s < 0}
    tail = {s: (jnp.where(ce == re + s - edge, 1.0, 0.0) * has_right).astype(BF16) for s in taps if s > 0}

    def run(m_ref, p_ref, n_ref, w_ref, b_ref, o_ref):
        main = m_ref[...]
        acc = b_ref[...] + w_ref[pad:pad + 1, :] * main.astype(F32)
        for j, s in enumerate(taps):
            if s != 0:
                acc = acc + w_ref[j:j + 1, :] * _dot(shift[s], main)
        first, last = acc[0:edge], acc[blk - edge:blk]
        for j, s in enumerate(taps):
            if s < 0:
                first = first + w_ref[j:j + 1, :] * _dot(head[s], p_ref[...])
            if s > 0:
                last = last + w_ref[j:j + 1, :] * _dot(tail[s], n_ref[...])
        o_ref[0:edge, :] = _silu(first).astype(o_ref.dtype)
        o_ref[edge:blk - edge, :] = _silu(acc[edge:blk - edge]).astype(o_ref.dtype)
        o_ref[blk - edge:blk, :] = _silu(last).astype(o_ref.dtype)

    run(x_ref, xp_ref, xn_ref, wx_ref, bx_ref, xo_ref)
    run(bc_ref, bcp_ref, bcn_ref, wbc_ref, bbc_ref, bco_ref)


def _conv(lay, U, cols, wx, bx, wbc, bbc):
    blk = SEQ_BLOCK
    nb = lay.R // blk
    hb = blk // HALO_ROWS
    nh = lay.R // HALO_ROWS
    xw = cols["ssd_x"][1]
    bcw = cols["ssd_bc"][1]
    xc, bcc = cols["ssd_x"][0] // xw, cols["ssd_bc"][0] // bcw
    prev = lambda i: jnp.maximum(i * hb - 1, 0)
    nxt = lambda i: jnp.minimum((i + 1) * hb, nh - 1)
    const = lambda shape: pl.BlockSpec(shape, lambda i: (0,) * len(shape))
    return pl.pallas_call(
        functools.partial(_conv_kernel, n_ctx_total=lay.B * lay.NC, NC=lay.NC, NL=lay.NL),
        out_shape=(jax.ShapeDtypeStruct((lay.R, xw), BF16), jax.ShapeDtypeStruct((lay.R, bcw), BF16)),
        grid=(nb,),
        in_specs=[
            pl.BlockSpec((blk, xw), lambda i: (i, xc)),
            pl.BlockSpec((HALO_ROWS, xw), lambda i: (prev(i), xc)),
            pl.BlockSpec((HALO_ROWS, xw), lambda i: (nxt(i), xc)),
            pl.BlockSpec((blk, bcw), lambda i: (i, bcc)),
            pl.BlockSpec((HALO_ROWS, bcw), lambda i: (prev(i), bcc)),
            pl.BlockSpec((HALO_ROWS, bcw), lambda i: (nxt(i), bcc)),
            const((SSD_CONV, xw)), const((1, xw)), const((SSD_CONV, bcw)), const((1, bcw)),
        ],
        out_specs=(pl.BlockSpec((blk, xw), lambda i: (i, 0)), pl.BlockSpec((blk, bcw), lambda i: (i, 0))),
        compiler_params=_cparams(("arbitrary",)),
        name="ssd_conv",
    )(U, U, U, U, U, U, wx, bx, wbc, bbc)


def _ssd_kernel(*refs, nb, rev, final):
    per_b = 4 if final else 3
    seqs = [refs[i * per_b:(i + 1) * per_b] for i in range(nb)]
    rest = refs[nb * per_b:]
    if final:
        dtb_ref, apad_ref, e_ref, dsk_ref, ng_ref, prev_ref, o_ref, st_ref, y_ref = rest
    else:
        dtb_ref, apad_ref, e_ref, o_ref, st_ref = rest
        y_ref = o_ref
    blk, inner = seqs[0][0].shape
    N = SSD_STATE
    P = SSD_HEAD_DIM
    C = SSD_CHUNK
    gw = inner // SSD_GROUPS
    pair = 2 * P

    @pl.when(pl.program_id(0) == 0)
    def _():
        st_ref[...] = jnp.zeros_like(st_ref)

    tri = _block_tri(blk, C, rev)
    e = e_ref[...]
    dt_fulls, cs_fulls = [], []
    for bi in range(nb):
        dt16 = _softplus(seqs[bi][2][...] + dtb_ref[...])
        cs16 = _dot_sum(tri, dt16 * apad_ref[...])
        dt_fulls.append(_dot_spread(dt16, e))
        cs_fulls.append(_dot_spread(cs16, e))

    ri = lax.broadcasted_iota(jnp.int32, (C, pair), 0)
    ci = lax.broadcasted_iota(jnp.int32, (C, pair), 1)
    cm = jnp.where(ci >= P, ci - P, ci)
    eye2 = cm == ri
    causal2 = (cm >= ri) if rev else (cm <= ri)
    left = ci < P
    zero = jnp.zeros((C, pair), BF16)

    n_chunks = blk // C
    for step in range(n_chunks):
        c = n_chunks - 1 - step if rev else step
        rs = slice(c * C, (c + 1) * C)
        last = c * C if rev else (c + 1) * C - 1
        units = [(g, bi) for g in range(SSD_GROUPS) for bi in range(nb)]
        held = []
        for g, bi in units:
            gs = slice(g * gw, (g + 1) * gw)
            x_ref, bc_ref = seqs[bi][:2]
            bm = bc_ref[rs, g * N:(g + 1) * N]
            cmat = bc_ref[rs, SSD_GROUPS * N + g * N:SSD_GROUPS * N + (g + 1) * N]
            cb2 = _dot_nt(cmat, jnp.concatenate([bm, bm], axis=0))
            cs_g = cs_fulls[bi][rs, gs]
            cs_last = cs_fulls[bi][last:last + 1, gs]
            xdt = x_ref[rs, gs].astype(F32) * dt_fulls[bi][rs, gs]
            st = st_ref[bi * SSD_GROUPS + g]
            y_inter = _dot(cmat, st.astype(BF16)) * jnp.exp(cs_g)
            w = (xdt * jnp.exp(cs_last - cs_g)).astype(BF16)
            st_ref[bi * SSD_GROUPS + g] = st * jnp.exp(cs_last) + _dot_tn(bm, w)
            held.append((cb2, cs_g, y_inter, xdt.astype(BF16)))
        for (g, bi), (cb2, cs_g, y_inter, xdt_b) in zip(units, held):
            x_ref = seqs[bi][0]
            for pr in range(gw // pair):
                ps = slice(pr * pair, (pr + 1) * pair)
                col = cs_g[:, ps]
                rowv = jnp.sum(jnp.where(eye2, col, 0.0), axis=0, keepdims=True)
                lmat = jnp.exp(jnp.where(causal2, col - rowv, -jnp.inf))
                mm = (lmat * cb2).astype(BF16)
                xp = xdt_b[:, ps]
                rhs = jnp.concatenate([jnp.where(left, xp, zero), jnp.where(left, zero, xp)], axis=0)
                y = _dot(mm, rhs) + y_inter[:, ps]
                os_ = slice(g * gw + pr * pair, g * gw + (pr + 1) * pair)
                if final:
                    y_ref[bi, rs, os_] = (y + prev_ref[bi, rs, os_]
                                          + x_ref[rs, os_].astype(F32) * dsk_ref[:, os_])
                else:
                    y_ref[bi, rs, os_] = y

    if final:
        rc = 64
        for bi in range(nb):
            z_ref = seqs[bi][3]
            for r in range(blk // rc):
                rs = slice(r * rc, (r + 1) * rc)
                for g in range(SSD_GROUPS):
                    gs = slice(g * gw, (g + 1) * gw)
                    yg = y_ref[bi, rs, gs] * _silu(z_ref[rs, gs].astype(F32))
                    ms = jnp.mean(yg * yg, axis=-1, keepdims=True)
                    o_ref[bi, rs, gs] = (yg * lax.rsqrt(ms + EPS) * ng_ref[:, gs]).astype(o_ref.dtype)


def _ssd_dir(lay, xc, bcc, small, U, cols, dtb, apad, expand, rev, dsk=None, norm_g=None, prev=None):
    B, NS, blk = lay.B, lay.NS, SEQ_BLOCK
    inner = xc.shape[1]
    bcw = bcc.shape[1]
    final = prev is not None
    seq = lambda n: lay.seq_order(n, rev)
    const = lambda shape: pl.BlockSpec(shape, lambda n: (0,) * len(shape))

    def rows(w, b, col=0):
        return pl.BlockSpec((blk, w), lambda n: (lay.row_block(b, seq(n)), col))

    in_specs, args = [], []
    for b in range(B):
        in_specs += [rows(inner, b), rows(bcw, b), rows(SMALL_COLS, b)]
        args += [xc, bcc, small]
        if final:
            zc = cols["ssd_z"]
            in_specs.append(rows(inner, b, zc[0] // zc[1]))
            args.append(U)
    in_specs += [const((1, SMALL_COLS)), const((1, SMALL_COLS)), const((SMALL_COLS, inner))]
    args += [dtb, apad, expand]
    per_batch = pl.BlockSpec((B, blk, inner), lambda n: (0, seq(n), 0))
    if final:
        in_specs += [const((1, inner)), const((1, inner)), per_batch]
        args += [dsk, norm_g, prev]
    return pl.pallas_call(
        functools.partial(_ssd_kernel, nb=B, rev=rev, final=final),
        out_shape=jax.ShapeDtypeStruct((B, NS * blk, inner), BF16 if final else F32),
        grid=(NS,),
        in_specs=in_specs,
        out_specs=per_batch,
        scratch_shapes=[pltpu.VMEM((B * SSD_GROUPS, SSD_STATE, inner // SSD_GROUPS), F32)]
        + ([pltpu.VMEM((B, blk, inner), F32)] if final else []),
        compiler_params=_cparams(("arbitrary",)),
        name="ssd_bwd" if rev else "ssd_fwd",
    )(*args)


def _rope_tables(CTX, SEQ):
    hd = ATT_HEAD_DIM
    pairs = hd // 4
    rows = SEQ // GRID_W
    row = jnp.repeat(jnp.arange(rows, dtype=F32), GRID_W)
    col = jnp.tile(jnp.arange(GRID_W, dtype=F32), rows)
    inv = jnp.exp(-math.log(ROPE_THETA) * jnp.arange(pairs, dtype=F32) / pairs)
    ar = row[:, None] * inv
    ac = col[:, None] * inv
    ang = jnp.concatenate([ar, ar, ac, ac], axis=-1)
    cos, sin = jnp.cos(ang), jnp.sin(ang)
    first = (jnp.arange(hd) % (2 * pairs)) < pairs
    sa = jnp.where(first, -sin, 0.0)
    sb = jnp.where(first, 0.0, sin)
    ones = jnp.ones((CTX, hd), F32)
    zeros = jnp.zeros((CTX, hd), F32)
    return (jnp.concatenate([ones, cos]), jnp.concatenate([zeros, sa]), jnp.concatenate([zeros, sb]))


def _column_plan(D):
    mix = D // 2
    gla_v = mix
    gla_qk = mix // 2
    ssd_inner = mix
    ssd_bc = 2 * SSD_GROUPS * SSD_STATE
    order = [("gla_q", gla_qk), ("gla_k", gla_qk), ("gla_v", gla_v), ("gla_gate", gla_v),
             ("ssd_z", ssd_inner), ("ssd_x", ssd_inner), ("ssd_bc", ssd_bc)]
    cols, off = {}, 0
    for name, w in order:
        assert off % w == 0
        cols[name] = (off, w)
        off += w
    return cols, off


def _source_columns(D):
    mix = D // 2
    att_kv = ATT_KV_HEADS * ATT_HEAD_DIM
    ssd_bc = SSD_GROUPS * SSD_STATE
    heads = mix // SSD_HEAD_DIM
    widths = [("att_q", mix), ("att_k", att_kv), ("att_v", att_kv), ("gla_q", mix // 2), ("gla_k", mix // 2),
              ("gla_v", mix), ("gla_gate", mix), ("gla_lr", 2 * GLA_LOWRANK), ("ssd_z", mix),
              ("ssd_x", mix), ("ssd_bc", 2 * ssd_bc), ("ssd_dt", 2 * heads), ("gates", N_BRANCH * D)]
    src, off = {}, 0
    for name, w in widths:
        src[name] = (off, w)
        off += w
    return src, off


def kernel(x, c, ctx, c_ctx, norm1, norm2, w_ada, b_ada, w_in, att_q_norm, att_k_norm, gla_w_decay,
           gla_b_decay, gla_norm, ssd_conv_w, ssd_conv_b, ssd_dt_bias, ssd_a_log, ssd_d, ssd_norm,
           w_branch, w_out, w_ff1, w_ff2, final_norm):
    B, SEQ, D = x.shape
    CTX = ctx.shape[1]
    depth = w_in.shape[0]
    lay = _Layout(B, CTX, SEQ, D)
    cols, n_main = _column_plan(D)
    src, n_src = _source_columns(D)
    assert n_src == w_in.shape[2]
    mix = D // 2
    ssd_heads = mix // SSD_HEAD_DIM
    lr_w = 2 * GLA_LOWRANK
    assert lr_w + 2 * ssd_heads <= SMALL_COLS

    X = jnp.concatenate([ctx.reshape(B * CTX, D), x.reshape(B * SEQ, D)], axis=0)
    cvec = jnp.concatenate([c, c_ctx[None, :], jnp.zeros((MOD_ROWS - B - 1, D), F32)], axis=0)
    mods = _ada(cvec, w_ada, b_ada).reshape(depth, MOD_ROWS, 1, N_MOD * D)
    rope = _rope_tables(CTX, SEQ)

    expand = np.zeros((2, SMALL_COLS, mix), np.float32)
    for d in range(2):
        for h in range(ssd_heads):
            expand[d, lr_w + d * ssd_heads + h, h * SSD_HEAD_DIM:(h + 1) * SSD_HEAD_DIM] = 1.0
    expand = jnp.asarray(expand, BF16)

    lead = ["att_q", "att_k", "att_v", "gla_q", "gla_k", "gla_v", "gla_gate"]
    assert [src[n][0] for n in lead] == list(np.cumsum([0] + [src[n][1] for n in lead[:-1]]))
    assert [cols[n][0] for n in lead[3:]] == [src[n][0] - src["gla_q"][0] for n in lead[3:]]
    n_a = src["gla_gate"][0] + src["gla_gate"][1]
    w_in_b = w_in.astype(BF16)
    take = lambda name: w_in_b[:, :, src[name][0]:src[name][0] + src[name][1]]
    w_b = jnp.concatenate([take("ssd_z"), take("ssd_x"), take("ssd_bc")], axis=2)
    assert cols["ssd_z"][0] == n_a - src["gla_q"][0] and cols["ssd_x"][0] == cols["ssd_z"][0] + mix
    w_small = jnp.concatenate([take("gla_lr"), take("ssd_dt"),
                               jnp.zeros((depth, D, SMALL_COLS - lr_w - 2 * ssd_heads), BF16)], axis=2)
    w_gate = take("gates")
    wbr_b, wo_b = w_branch[0].astype(BF16), w_out[0].astype(BF16)
    w1_b, w2_b = w_ff1[0].astype(BF16), w_ff2[0].astype(BF16)
    w_branch_rows = w_branch.reshape(depth, N_BRANCH * mix, D)

    out = None
    for l in range(depth):
        mod = mods[l]
        g1 = norm1[l][None, :]

        U, small, qt, kk, vt, H = _inproj(lay, X, mod, g1, w_in_b, n_a, w_b, w_small, att_q_norm[l][None, :],
                                          att_k_norm[l][None, :], rope, l)

        o_att = _attention(lay, qt, kk, vt)

        gla_out = None
        for d in range(2):
            wd = jnp.zeros((SMALL_COLS, mix // 2), F32).at[d * GLA_LOWRANK:(d + 1) * GLA_LOWRANK].set(
                gla_w_decay[l, d]).astype(BF16)
            bd = gla_b_decay[l, d][None, :]
            if d == 0:
                gla_out = _gla_dir(lay, U, small, cols, wd, bd, rev=False)
            else:
                gla_out = _gla_dir(lay, U, small, cols, wd, bd, rev=True,
                                   norm_g=gla_norm[l][None, :], prev=gla_out)
        o_gla = gla_out

        cw, cbias = ssd_conv_w[l], ssd_conv_b[l]
        xc, bcc = _conv(lay, U, cols, cw[:, :mix], cbias[None, :mix], cw[:, mix:], cbias[None, mix:])
        ssd_out = None
        for d in range(2):
            lanes = slice(lr_w + d * ssd_heads, lr_w + (d + 1) * ssd_heads)
            dtb = jnp.zeros((1, SMALL_COLS), F32).at[0, lanes].set(ssd_dt_bias[l, d])
            apad = jnp.zeros((1, SMALL_COLS), F32).at[0, lanes].set(-jnp.exp(ssd_a_log[l, d]))
            if d == 0:
                ssd_out = _ssd_dir(lay, xc, bcc, small, U, cols, dtb, apad, expand[d], rev=False)
            else:
                ssd_out = _ssd_dir(lay, xc, bcc, small, U, cols, dtb, apad, expand[d], rev=True,
                                   dsk=jnp.repeat(ssd_d[l], SSD_HEAD_DIM)[None, :],
                                   norm_g=ssd_norm[l][None, :], prev=ssd_out)
        o_ssd = ssd_out

        last = l == depth - 1
        skip = B * CTX if last else 0
        if last:
            m = _merge(lay, H, o_att, o_gla, o_ssd, w_gate, wbr_b, l, skip)
            out = _mlp(lay, X, m, wo_b, mod, norm2[l][None, :], w1_b, w2_b, l, skip,
                       final_gain=final_norm[None, :])
        else:
            m, wbr_next, wo_next = _merge(lay, H, o_att, o_gla, o_ssd, w_gate, wbr_b, l, skip,
                                          cast_next=(w_branch_rows, w_out))
            X, w1_b, w2_b = _mlp(lay, X, m, wo_b, mod, norm2[l][None, :], w1_b, w2_b, l, skip,
                                 cast_next=(w_ff1, w_ff2))
            wbr_b, wo_b = wbr_next, wo_next
    return out.reshape(B, SEQ, D)
```

```python
import functools
import math

import jax
import jax.numpy as jnp
import numpy as np
from jax import lax
from jax.experimental import pallas as pl
from jax.experimental.pallas import tpu as pltpu

F32 = jnp.float32
BF16 = jnp.bfloat16

EPS = 1e-6
GRID_W = 64
N_BRANCH = 3
N_MOD = 6
ATT_HEAD_DIM = 128
ATT_KV_HEADS = 2
ROPE_THETA = 10000.0
GLA_HEADS = 4
GLA_LOWRANK = 16
GLA_TAU = 16.0
GLA_CHUNK = 64
SSD_HEAD_DIM = 64
SSD_GROUPS = 2
SSD_STATE = 128
SSD_CONV = 5
SSD_CHUNK = 64

SEQ_BLOCK = 256
RESIDENT_TM = 256
DOT_COLS = 512
MLP_FF_TILE = 1024
ADA_COLS = 1024
ROW_CHUNK = 64
MOD_ROWS = 8
SMALL_COLS = 128
HALO_ROWS = 16
ATT_ONES_ROWS = 16
ATT_PAIR_UNROLL = 4
VMEM_LIMIT = 56 * 1024 * 1024


def _cparams(sem):
    return pltpu.CompilerParams(dimension_semantics=sem, vmem_limit_bytes=VMEM_LIMIT)


def _dot(a, b):
    return jnp.dot(a, b, preferred_element_type=F32)


def _dot_nt(a, b):
    return lax.dot_general(a, b, (((1,), (1,)), ((), ())), preferred_element_type=F32)


def _dot_tn(a, b):
    return lax.dot_general(a, b, (((0,), (0,)), ((), ())), preferred_element_type=F32)


def _split2(x):
    hi = x.astype(BF16)
    lo = (x - hi.astype(F32)).astype(BF16)
    return hi, lo


def _dot_sum(a, x):
    hi, lo = _split2(x)
    return _dot(a, lo) + _dot(a, hi)


def _dot_spread(x, e):
    hi, lo = _split2(x)
    return _dot(lo, e) + _dot(hi, e)


def _silu(x):
    return x * jax.nn.sigmoid(x)


def _log1p_exp_neg_abs(x):
    return jnp.log(1.0 + jnp.exp(-jnp.abs(x)))


def _softplus(x):
    return jnp.maximum(x, 0.0) + _log1p_exp_neg_abs(x)


def _block_tri(n, chunk, rev):
    r = lax.broadcasted_iota(jnp.int32, (n, n), 0)
    c = lax.broadcasted_iota(jnp.int32, (n, n), 1)
    shift = chunk.bit_length() - 1
    assert chunk == 1 << shift
    same = jnp.right_shift(r, shift) == jnp.right_shift(c, shift)
    tri = (c >= r) if rev else (c <= r)
    return jnp.where(same, jnp.where(tri, 1.0, 0.0), 0.0).astype(BF16)


class _Layout:
    def __init__(self, B, CTX, SEQ, D):
        assert CTX % SEQ_BLOCK == 0 and SEQ % SEQ_BLOCK == 0
        self.B, self.CTX, self.SEQ, self.D = B, CTX, SEQ, D
        self.R = B * (CTX + SEQ)
        self.NC = CTX // SEQ_BLOCK
        self.NL = SEQ // SEQ_BLOCK
        self.NS = self.NC + self.NL
        self.tm = next(t for t in (512, 256) if (B * CTX) % t == 0 and SEQ % t == 0)
        assert B + 1 <= MOD_ROWS

    def row_block(self, b, s):
        return jnp.where(s < self.NC, b * self.NC + s, self.B * self.NC + b * self.NL + (s - self.NC))

    def batch_seq(self, i):
        nctx = self.B * self.NC
        j = jnp.maximum(i - nctx, 0)
        return (jnp.where(i < nctx, i // self.NC, j // self.NL),
                jnp.where(i < nctx, i % self.NC, self.NC + j % self.NL))

    def seq_order(self, n, rev):
        if not rev:
            return n
        return jnp.where(n < self.NC, self.NC - 1 - n, self.NC + self.NL - 1 - (n - self.NC))

    def mod_row(self, i, tm):
        nctx = (self.B * self.CTX) // tm
        return jnp.where(i < nctx, self.B, (i - nctx) // (self.SEQ // tm))


def _ada_kernel(c_ref, w_ref, b_ref, o_ref):
    s = _silu(c_ref[...]).astype(BF16)
    o_ref[0] = _dot(s, w_ref[0].astype(BF16)) + b_ref[0]


def _ada(cvec, w_ada, b_ada):
    depth, D, n = w_ada.shape
    tn = ADA_COLS
    return pl.pallas_call(
        _ada_kernel,
        out_shape=jax.ShapeDtypeStruct((depth, MOD_ROWS, n), F32),
        grid=(depth, n // tn),
        in_specs=[
            pl.BlockSpec((MOD_ROWS, D), lambda l, j: (0, 0)),
            pl.BlockSpec((1, D, tn), lambda l, j: (l, 0, j)),
            pl.BlockSpec((1, 1, tn), lambda l, j: (l, 0, j)),
        ],
        out_specs=pl.BlockSpec((1, MOD_ROWS, tn), lambda l, j: (l, 0, j)),
        compiler_params=_cparams(("arbitrary", "arbitrary")),
        name="ada",
    )(cvec, w_ada, b_ada.reshape(depth, 1, n))


def _norm_mod(x_ref, g_ref, mod_ref, k, h_ref):
    D = x_ref.shape[-1]
    shift = mod_ref[:, k * D:(k + 1) * D]
    a = g_ref[...] * (1.0 + mod_ref[:, (k + 1) * D:(k + 2) * D])
    rc = ROW_CHUNK

    def body(r, carry):
        sl = pl.ds(pl.multiple_of(r * rc, rc), rc)
        xf = x_ref[sl, :]
        ms = jnp.mean(xf * xf, axis=-1, keepdims=True)
        h_ref[sl, :] = (xf * lax.rsqrt(ms + EPS) * a + shift).astype(h_ref.dtype)
        return carry

    lax.fori_loop(0, x_ref.shape[0] // rc, body, 0)


def _resident(shape, layer=None):
    if layer is None:
        return pl.BlockSpec(shape, lambda *_: (0,) * len(shape), pipeline_mode=pl.Buffered(1))
    return pl.BlockSpec((None,) + tuple(shape), lambda *_: (layer,) + (0,) * len(shape),
                        pipeline_mode=pl.Buffered(1))


def _inproj_kernel(x_ref, mod_ref, g_ref, wa_ref, wb_ref, ws_ref, qg_ref, kg_ref, cos_ref, sa_ref, sb_ref,
                   u_ref, small_ref, qt_ref, ko_ref, vt_ref, h_ref, *, tn, att_q, att_kv):
    _norm_mod(x_ref, g_ref, mod_ref, 0, h_ref)
    h = h_ref[...]
    hd = ATT_HEAD_DIM
    cos, sa, sb = cos_ref[...], sa_ref[...], sb_ref[...]

    def prep(y, g):
        y = y * lax.rsqrt(jnp.mean(y * y, axis=-1, keepdims=True) + EPS) * g
        return y * cos + pltpu.roll(y, hd - hd // 4, 1) * sa + pltpu.roll(y, hd // 4, 1) * sb

    q_scale = (hd ** -0.5) * math.log2(math.e)
    for j in range(att_q // tn):
        q = _dot(h, wa_ref[:, j * tn:(j + 1) * tn])
        for hh in range(tn // hd):
            qh = prep(q[:, hh * hd:(hh + 1) * hd], qg_ref[...]) * q_scale
            qt_ref[j * (tn // hd) + hh] = qh.T.astype(qt_ref.dtype)
    kv = _dot(h, wa_ref[:, att_q:att_q + 2 * att_kv])
    for g in range(att_kv // hd):
        sl = slice(g * hd, (g + 1) * hd)
        ko_ref[:, sl] = prep(kv[:, sl], kg_ref[...]).astype(ko_ref.dtype)
        vt_ref[g, 0:hd, :] = kv[:, att_kv + g * hd:att_kv + (g + 1) * hd].T.astype(vt_ref.dtype)
        vt_ref[g, hd:, :] = jnp.ones((ATT_ONES_ROWS, x_ref.shape[0]), vt_ref.dtype)

    small_ref[...] = _dot(h, ws_ref[...])
    a0 = att_q + 2 * att_kv
    na = wa_ref.shape[1] - a0
    for j in range(na // tn):
        u_ref[:, j * tn:(j + 1) * tn] = _dot(h, wa_ref[:, a0 + j * tn:a0 + (j + 1) * tn]).astype(u_ref.dtype)
    for j in range(wb_ref.shape[1] // tn):
        u_ref[:, na + j * tn:na + (j + 1) * tn] = _dot(h, wb_ref[:, j * tn:(j + 1) * tn]).astype(u_ref.dtype)


def _inproj(lay, X, mod, g, w_a, n_a, w_b, w_small, q_gain, k_gain, rope, l):
    D = w_a.shape[1]
    n_b = w_b.shape[2]
    tm = RESIDENT_TM
    assert tm == SEQ_BLOCK
    hd = ATT_HEAD_DIM
    att_q = D // 2
    att_kv = ATT_KV_HEADS * hd
    heads = att_q // hd
    n_u = n_a - att_q - 2 * att_kv + n_b
    B, NS = lay.B, lay.NS
    bs = lay.batch_seq
    tab = pl.BlockSpec((tm, hd), lambda i: (bs(i)[1], 0))
    return pl.pallas_call(
        functools.partial(_inproj_kernel, tn=DOT_COLS, att_q=att_q, att_kv=att_kv),
        out_shape=(jax.ShapeDtypeStruct((lay.R, n_u), BF16),
                   jax.ShapeDtypeStruct((lay.R, SMALL_COLS), F32),
                   jax.ShapeDtypeStruct((lay.R // tm, heads, hd, tm), BF16),
                   jax.ShapeDtypeStruct((B, NS * tm, att_kv), BF16),
                   jax.ShapeDtypeStruct((B, ATT_KV_HEADS, NS, hd + ATT_ONES_ROWS, tm), BF16),
                   jax.ShapeDtypeStruct((lay.R, D), BF16)),
        grid=(lay.R // tm,),
        in_specs=[
            pl.BlockSpec((tm, D), lambda i: (i, 0)),
            pl.BlockSpec((None, 1, N_MOD * D), lambda i: (lay.mod_row(i, tm), 0, 0)),
            _resident((1, D)),
            _resident((D, n_a), l),
            _resident((D, n_b), l),
            _resident((D, SMALL_COLS), l),
            _resident((1, hd)), _resident((1, hd)),
            tab, tab, tab,
        ],
        out_specs=(pl.BlockSpec((tm, n_u), lambda i: (i, 0)),
                   pl.BlockSpec((tm, SMALL_COLS), lambda i: (i, 0)),
                   pl.BlockSpec((None, heads, hd, tm), lambda i: (i, 0, 0, 0)),
                   pl.BlockSpec((None, tm, att_kv), lambda i: (*bs(i), 0)),
                   pl.BlockSpec((None, ATT_KV_HEADS, None, hd + ATT_ONES_ROWS, tm),
                                lambda i: (bs(i)[0], 0, bs(i)[1], 0, 0)),
                   pl.BlockSpec((tm, D), lambda i: (i, 0))),
        compiler_params=_cparams(("arbitrary",)),
        name="in_proj",
    )(X, mod, g, w_a, w_b, w_small, q_gain, k_gain, *rope)


def _slab_count(n_steps, *row_counts):
    n = 1
    while 2 * n <= n_steps and all(r % (2 * n * 16) == 0 for r in row_counts):
        n *= 2
    return n


def _merge_kernel(h_ref, oa_ref, og_ref, os_ref, wg_ref, wb_ref, *rest, tn, cast_next):
    if cast_next:
        c1_ref, c2_ref, m_ref, n1_ref, n2_ref = rest
        n1_ref[...] = c1_ref[...].astype(n1_ref.dtype)
        n2_ref[...] = c2_ref[...].astype(n2_ref.dtype)
    else:
        (m_ref,) = rest
    h = h_ref[...]
    D = m_ref.shape[1]
    for j in range(D // tn):
        acc = None
        for b, o_ref in enumerate((oa_ref, og_ref, os_ref)):
            gate = _dot(h, wg_ref[:, b * D + j * tn:b * D + (j + 1) * tn])
            t = jax.nn.sigmoid(gate) * _dot(o_ref[...], wb_ref[b, :, j * tn:(j + 1) * tn])
            acc = t if acc is None else acc + t
        m_ref[:, j * tn:(j + 1) * tn] = acc.astype(m_ref.dtype)


def _merge(lay, H, o_att, o_gla, o_ssd, w_gate, w_branch, l, skip_rows, cast_next=None):
    D = lay.D
    W = o_att.shape[1]
    tm = RESIDENT_TM
    assert tm == SEQ_BLOCK
    row0 = skip_rows // tm
    rows = lambda i: (i + row0, 0)
    seq_rows = pl.BlockSpec((None, tm, W), lambda i: (*lay.batch_seq(i + row0), 0))
    n_i = lay.R // tm - row0
    in_specs = [
        pl.BlockSpec((tm, D), rows),
        pl.BlockSpec((tm, W), rows), seq_rows, seq_rows,
        _resident(w_gate.shape[1:], l), _resident(w_branch.shape),
    ]
    args = [H, o_att, o_gla, o_ssd, w_gate, w_branch]
    out_shape = [jax.ShapeDtypeStruct((lay.R, D), BF16)]
    out_specs = [pl.BlockSpec((tm, D), rows)]
    if cast_next is not None:
        n_slabs = _slab_count(n_i, N_BRANCH * W, D)
        r1, r2 = N_BRANCH * W // n_slabs, D // n_slabs
        slab = lambda i: jnp.minimum(i, n_slabs - 1)
        in_specs += [pl.BlockSpec((None, r1, D), lambda i: (l + 1, slab(i), 0)),
                     pl.BlockSpec((None, r2, D), lambda i: (l + 1, slab(i), 0))]
        args += list(cast_next)
        out_shape += [jax.ShapeDtypeStruct((N_BRANCH * W, D), BF16), jax.ShapeDtypeStruct((D, D), BF16)]
        out_specs += [pl.BlockSpec((r1, D), lambda i: (slab(i), 0)), pl.BlockSpec((r2, D), lambda i: (slab(i), 0))]
    res = pl.pallas_call(
        functools.partial(_merge_kernel, tn=DOT_COLS, cast_next=cast_next is not None),
        out_shape=tuple(out_shape),
        grid=(n_i,),
        in_specs=in_specs,
        out_specs=tuple(out_specs),
        compiler_params=_cparams(("arbitrary",)),
        name="merge",
    )(*args)
    if cast_next is None:
        return res[0]
    return res[0], res[1].reshape(N_BRANCH, W, D), res[2]


def _mlp_kernel(x_ref, m_ref, wo_ref, mod_ref, g_ref, w1_ref, w2_ref, *rest, final, cast_next):
    rest = list(rest)
    fg_ref = rest.pop(0) if final else None
    if cast_next:
        c1_ref, c2_ref = rest.pop(0), rest.pop(0)
        o_ref, n1_ref, n2_ref, x1_ref, h_ref = rest
        n1_ref[...] = c1_ref[...].astype(n1_ref.dtype)
        n2_ref[...] = c2_ref[...].astype(n2_ref.dtype)
    else:
        o_ref, x1_ref, h_ref = rest
    k = pl.program_id(1)
    D = x_ref.shape[-1]

    @pl.when(k == 0)
    def _():
        tn = DOT_COLS
        for j in range(D // tn):
            sl = slice(j * tn, (j + 1) * tn)
            x1_ref[:, sl] = x_ref[:, sl] + mod_ref[:, 2 * D + j * tn:2 * D + (j + 1) * tn] * _dot(
                m_ref[...], wo_ref[:, sl])
        _norm_mod(x1_ref, g_ref, mod_ref, 3, h_ref)
        o_ref[...] = jnp.zeros_like(o_ref)

    f = jnp.square(jnp.maximum(_dot(h_ref[...], w1_ref[...]), 0.0)).astype(BF16)
    o_ref[...] += _dot(f, w2_ref[...])

    @pl.when(k == pl.num_programs(1) - 1)
    def _():
        gate = mod_ref[:, 5 * D:6 * D]
        rc = ROW_CHUNK

        def body(r, carry):
            sl = pl.ds(pl.multiple_of(r * rc, rc), rc)
            y = x1_ref[sl, :] + gate * o_ref[sl, :]
            if final:
                ms = jnp.mean(y * y, axis=-1, keepdims=True)
                y = y * lax.rsqrt(ms + EPS) * fg_ref[...]
            o_ref[sl, :] = y
            return carry

        lax.fori_loop(0, x_ref.shape[0] // rc, body, 0)


def _mlp(lay, X, m, w_out, mod, g, w1, w2, l, skip_rows, final_gain=None, cast_next=None):
    D, FF = w1.shape
    tm, tf = lay.tm, MLP_FF_TILE
    row0 = skip_rows // tm
    final = final_gain is not None
    out_rows = lay.R - skip_rows if final else lay.R
    out_off = 0 if final else row0
    rows = lambda i, k: (i + row0, 0)
    n_i, n_k = lay.R // tm - row0, FF // tf
    in_specs = [
        pl.BlockSpec((tm, D), rows),
        pl.BlockSpec((tm, D), rows),
        _resident((D, D)),
        pl.BlockSpec((None, 1, N_MOD * D), lambda i, k: (lay.mod_row(i + row0, tm), 0, 0)),
        _resident((1, D)),
        pl.BlockSpec((D, tf), lambda i, k: (0, k)),
        pl.BlockSpec((tf, D), lambda i, k: (k, 0)),
    ]
    args = [X, m, w_out, mod, g, w1, w2]
    out_shape = [jax.ShapeDtypeStruct((out_rows, D), F32)]
    out_specs = [pl.BlockSpec((tm, D), lambda i, k: (i + out_off, 0))]
    if final:
        in_specs.append(_resident((1, D)))
        args.append(final_gain)
    if cast_next is not None:
        n_slabs = _slab_count(n_i * n_k, D, FF)
        r1, r2 = D // n_slabs, FF // n_slabs
        slab = lambda i, k: jnp.minimum(i * n_k + k, n_slabs - 1)
        in_specs += [pl.BlockSpec((None, r1, FF), lambda i, k: (l + 1, slab(i, k), 0)),
                     pl.BlockSpec((None, r2, D), lambda i, k: (l + 1, slab(i, k), 0))]
        args += list(cast_next)
        out_shape += [jax.ShapeDtypeStruct((D, FF), BF16), jax.ShapeDtypeStruct((FF, D), BF16)]
        out_specs += [pl.BlockSpec((r1, FF), lambda i, k: (slab(i, k), 0)),
                      pl.BlockSpec((r2, D), lambda i, k: (slab(i, k), 0))]
    res = pl.pallas_call(
        functools.partial(_mlp_kernel, final=final, cast_next=cast_next is not None),
        out_shape=tuple(out_shape),
        grid=(n_i, n_k),
        in_specs=in_specs,
        out_specs=tuple(out_specs),
        scratch_shapes=[pltpu.VMEM((tm, D), F32), pltpu.VMEM((tm, D), BF16)],
        compiler_params=_cparams(("arbitrary", "arbitrary")),
        name="mlp",
    )(*args)
    return res if cast_next is not None else res[0]


def _attn_kernel(qt_ref, k_ref, vt_ref, o_ref, m_ref, acc_ref, s0_ref, s1_ref, *, n_ctx_blocks):
    n = pl.program_id(2)
    heads, hd, tq = qt_ref.shape
    nblk, vrows, tk = vt_ref.shape
    n_kv = jnp.where(n < n_ctx_blocks, n_ctx_blocks, nblk)
    m_ref[...] = jnp.full(m_ref.shape, -1e30, F32)
    acc_ref[...] = jnp.zeros(acc_ref.shape, F32)

    def scores(c, s_ref):
        c = jnp.minimum(c, n_kv - 1)
        k = k_ref[pl.ds(pl.multiple_of(c * tk, tk), tk), :]
        for h in range(heads):
            s_ref[h] = _dot(k, qt_ref[h])

    def update(c, s_ref):
        vt = vt_ref[c]
        for h in range(heads):
            st = s_ref[h]
            m_old = m_ref[h]
            m_new = jnp.maximum(m_old, jnp.max(st, axis=0, keepdims=True))
            pt = jnp.exp2((st - m_new).astype(BF16))
            acc_ref[h] = jnp.exp2(m_old - m_new) * acc_ref[h] + _dot(vt, pt)
            m_ref[h] = m_new

    scores(0, s0_ref)

    def pairs(c0, n_pairs):
        for j in range(n_pairs):
            scores(c0 + 2 * j + 1, s1_ref)
            update(c0 + 2 * j, s0_ref)
            scores(c0 + 2 * j + 2, s0_ref)
            update(c0 + 2 * j + 1, s1_ref)

    n_pairs = n_kv // 2
    unroll = ATT_PAIR_UNROLL

    def body(i, carry):
        pairs(2 * unroll * i, unroll)
        return carry

    lax.fori_loop(0, n_pairs // unroll, body, 0)
    done = (n_pairs // unroll) * unroll
    piece = unroll // 2
    while piece >= 1:
        @pl.when(((n_pairs - done) & piece) != 0)
        def _(done=done, piece=piece):
            pairs(2 * done, piece)

        done = done + ((n_pairs - done) & piece)
        piece //= 2

    @pl.when(n_kv % 2 == 1)
    def _():
        update(n_kv - 1, s0_ref)

    for h in range(heads):
        acc = acc_ref[h]
        ot = acc[0:hd] / acc[hd:hd + 1]
        o_ref[:, h * hd:(h + 1) * hd] = ot.T.astype(o_ref.dtype)


def _attention(lay, qt, kk, vt):
    B, kvh, NS, vrows, blk = vt.shape
    _, heads, hd, _ = qt.shape
    group = heads // kvh
    return pl.pallas_call(
        functools.partial(_attn_kernel, n_ctx_blocks=lay.NC),
        out_shape=jax.ShapeDtypeStruct((lay.R, heads * hd), BF16),
        grid=(B, kvh, NS),
        in_specs=[
            pl.BlockSpec((None, group, hd, blk), lambda b, g, n: (lay.row_block(b, n), g, 0, 0)),
            pl.BlockSpec((None, NS * blk, hd), lambda b, g, n: (b, 0, g)),
            pl.BlockSpec((None, None, NS, vrows, blk), lambda b, g, n: (b, g, 0, 0, 0)),
        ],
        out_specs=pl.BlockSpec((blk, group * hd), lambda b, g, n: (lay.row_block(b, n), g)),
        scratch_shapes=[pltpu.VMEM((group, 1, blk), F32), pltpu.VMEM((group, vrows, blk), F32),
                        pltpu.VMEM((group, blk, blk), F32), pltpu.VMEM((group, blk, blk), F32)],
        compiler_params=_cparams(("arbitrary", "arbitrary", "arbitrary")),
        name="attention",
    )(qt, kk, vt)


def _gla_kernel(*refs, nb, rev, final, dk, dv):
    per_b = 5 if final else 4
    seqs = [refs[i * per_b:(i + 1) * per_b] for i in range(nb)]
    rest = refs[nb * per_b:]
    if final:
        wd_ref, bd_ref, ng_ref, prev_ref, o_ref, st_ref = rest
    else:
        wd_ref, bd_ref, o_ref, st_ref = rest
    blk = seqs[0][0].shape[0]
    heads = seqs[0][0].shape[1] // dk
    C = GLA_CHUNK

    @pl.when(pl.program_id(0) == 0)
    def _():
        st_ref[...] = jnp.zeros_like(st_ref)

    tri = _block_tri(blk, C, rev)
    bcums = []
    for bi in range(nb):
        z = _dot(seqs[bi][3][...].astype(BF16), wd_ref[...]) + bd_ref[...]
        la = (jnp.minimum(z, 0.0) - _log1p_exp_neg_abs(z)) * (1.0 / GLA_TAU)
        bcums.append(_dot_sum(tri, la))

    ri = lax.broadcasted_iota(jnp.int32, (C, C), 0)
    ci = lax.broadcasted_iota(jnp.int32, (C, C), 1)
    causal = (ci >= ri) if rev else (ci <= ri)

    n_chunks = blk // C
    for step in range(n_chunks):
        c = n_chunks - 1 - step if rev else step
        rs = slice(c * C, (c + 1) * C)
        last = c * C if rev else (c + 1) * C - 1
        units = [(h, bi) for h in range(heads) for bi in range(nb)]
        scores, inter = [], []
        for h, bi in units:
            ks = slice(h * dk, (h + 1) * dk)
            q_ref, k_ref, v_ref = seqs[bi][:3]
            b = bcums[bi][rs, ks]
            b_last = bcums[bi][last:last + 1, ks]
            q = q_ref[rs, ks].astype(F32) * (dk ** -0.5)
            k = k_ref[rs, ks].astype(F32)
            q_dec = (q * jnp.exp(b)).astype(BF16)
            k_dec = (k * jnp.exp(-b)).astype(BF16)
            k_st = (k * jnp.exp(b_last - b)).astype(BF16)
            st = st_ref[bi * heads + h]
            scores.append(_dot_nt(q_dec, k_dec))
            inter.append(_dot_nt(q_dec, st.astype(BF16)))
            st_ref[bi * heads + h] = st * jnp.exp(b_last) + _dot_tn(v_ref[rs, h * dv:(h + 1) * dv], k_st)
        for (h, bi), sc, it in zip(units, scores, inter):
            vs = slice(h * dv, (h + 1) * dv)
            att = jnp.where(causal, sc, 0.0).astype(BF16)
            o = _dot(att, seqs[bi][2][rs, vs]) + it
            if final:
                tot = o + prev_ref[bi, rs, vs]
                y = tot * lax.rsqrt(jnp.mean(tot * tot, axis=-1, keepdims=True) + EPS) * ng_ref[...]
                o_ref[bi, rs, vs] = (y * _silu(seqs[bi][4][rs, vs].astype(F32))).astype(o_ref.dtype)
            else:
                o_ref[bi, rs, vs] = o


def _gla_dir(lay, U, small, cols, w_dec, b_dec, rev, norm_g=None, prev=None):
    B, NS, blk = lay.B, lay.NS, SEQ_BLOCK
    qk_w = cols["gla_q"][1]
    v_w = cols["gla_v"][1]
    dk, dv = qk_w // GLA_HEADS, v_w // GLA_HEADS
    final = prev is not None
    const = lambda shape: pl.BlockSpec(shape, lambda n: (0,) * len(shape))
    seq = lambda n: lay.seq_order(n, rev)

    def ucol(name, b):
        off, w = cols[name]
        return pl.BlockSpec((blk, w), lambda n: (lay.row_block(b, seq(n)), off // w))

    in_specs, args = [], []
    for b in range(B):
        in_specs += [ucol("gla_q", b), ucol("gla_k", b), ucol("gla_v", b),
                     pl.BlockSpec((blk, SMALL_COLS), lambda n, b=b: (lay.row_block(b, seq(n)), 0))]
        args += [U, U, U, small]
        if final:
            in_specs.append(ucol("gla_gate", b))
            args.append(U)
    in_specs += [const((SMALL_COLS, qk_w)), const((1, qk_w))]
    args += [w_dec, b_dec]
    per_batch = pl.BlockSpec((B, blk, v_w), lambda n: (0, seq(n), 0))
    if final:
        in_specs += [const((1, dv)), per_batch]
        args += [norm_g, prev]
    return pl.pallas_call(
        functools.partial(_gla_kernel, nb=B, rev=rev, final=final, dk=dk, dv=dv),
        out_shape=jax.ShapeDtypeStruct((B, NS * blk, v_w), BF16 if final else F32),
        grid=(NS,),
        in_specs=in_specs,
        out_specs=per_batch,
        scratch_shapes=[pltpu.VMEM((B * GLA_HEADS, dv, dk), F32)],
        compiler_params=_cparams(("arbitrary",)),
        name="gla_bwd" if rev else "gla_fwd",
    )(*args)


def _conv_kernel(x_ref, xp_ref, xn_ref, bc_ref, bcp_ref, bcn_ref, wx_ref, bx_ref, wbc_ref, bbc_ref,
                 xo_ref, bco_ref, *, n_ctx_total, NC, NL):
    i = pl.program_id(0)
    in_ctx = i < n_ctx_total
    p = jnp.where(in_ctx, i % NC, jnp.maximum(i - n_ctx_total, 0) % NL)
    seg = jnp.where(in_ctx, NC, NL)
    has_left = (p > 0).astype(F32)
    has_right = (p < seg - 1).astype(F32)
    blk = x_ref.shape[0]
    pad = SSD_CONV // 2
    edge = 8
    assert pad <= edge <= HALO_ROWS

    r = lax.broadcasted_iota(jnp.int32, (blk, blk), 0)
    c = lax.broadcasted_iota(jnp.int32, (blk, blk), 1)
    re = lax.broadcasted_iota(jnp.int32, (edge, HALO_ROWS), 0)
    ce = lax.broadcasted_iota(jnp.int32, (edge, HALO_ROWS), 1)
    taps = [j - pad for j in range(SSD_CONV)]
    shift = {s: jnp.where(c == r + s, 1.0, 0.0).astype(BF16) for s in taps if s != 0}
    head = {s: (jnp.where(ce == HALO_ROWS + re + s, 1.0, 0.0) * has_left).astype(BF16) for s in taps if s < 0}
    tail = {s: (jnp.where(ce == re + s - edge, 1.0, 0.0) * has_right).astype(BF16) for s in taps if s > 0}

    def run(m_ref, p_ref, n_ref, w_ref, b_ref, o_ref):
        main = m_ref[...]
        acc = b_ref[...] + w_ref[pad:pad + 1, :] * main.astype(F32)
        for j, s in enumerate(taps):
            if s != 0:
                acc = acc + w_ref[j:j + 1, :] * _dot(shift[s], main)
        first, last = acc[0:edge], acc[blk - edge:blk]
        for j, s in enumerate(taps):
            if s < 0:
                first = first + w_ref[j:j + 1, :] * _dot(head[s], p_ref[...])
            if s > 0:
                last = last + w_ref[j:j + 1, :] * _dot(tail[s], n_ref[...])
        o_ref[0:edge, :] = _silu(first).astype(o_ref.dtype)
        o_ref[edge:blk - edge, :] = _silu(acc[edge:blk - edge]).astype(o_ref.dtype)
        o_ref[blk - edge:blk, :] = _silu(last).astype(o_ref.dtype)

    run(x_ref, xp_ref, xn_ref, wx_ref, bx_ref, xo_ref)
    run(bc_ref, bcp_ref, bcn_ref, wbc_ref, bbc_ref, bco_ref)


def _conv(lay, U, cols, wx, bx, wbc, bbc):
    blk = SEQ_BLOCK
    nb = lay.R // blk
    hb = blk // HALO_ROWS
    nh = lay.R // HALO_ROWS
    xw = cols["ssd_x"][1]
    bcw = cols["ssd_bc"][1]
    xc, bcc = cols["ssd_x"][0] // xw, cols["ssd_bc"][0] // bcw
    prev = lambda i: jnp.maximum(i * hb - 1, 0)
    nxt = lambda i: jnp.minimum((i + 1) * hb, nh - 1)
    const = lambda shape: pl.BlockSpec(shape, lambda i: (0,) * len(shape))
    return pl.pallas_call(
        functools.partial(_conv_kernel, n_ctx_total=lay.B * lay.NC, NC=lay.NC, NL=lay.NL),
        out_shape=(jax.ShapeDtypeStruct((lay.R, xw), BF16), jax.ShapeDtypeStruct((lay.R, bcw), BF16)),
        grid=(nb,),
        in_specs=[
            pl.BlockSpec((blk, xw), lambda i: (i, xc)),
            pl.BlockSpec((HALO_ROWS, xw), lambda i: (prev(i), xc)),
            pl.BlockSpec((HALO_ROWS, xw), lambda i: (nxt(i), xc)),
            pl.BlockSpec((blk, bcw), lambda i: (i, bcc)),
            pl.BlockSpec((HALO_ROWS, bcw), lambda i: (prev(i), bcc)),
            pl.BlockSpec((HALO_ROWS, bcw), lambda i: (nxt(i), bcc)),
            const((SSD_CONV, xw)), const((1, xw)), const((SSD_CONV, bcw)), const((1, bcw)),
        ],
        out_specs=(pl.BlockSpec((blk, xw), lambda i: (i, 0)), pl.BlockSpec((blk, bcw), lambda i: (i, 0))),
        compiler_params=_cparams(("arbitrary",)),
        name="ssd_conv",
    )(U, U, U, U, U, U, wx, bx, wbc, bbc)


def _ssd_kernel(*refs, nb, rev, final):
    per_b = 4 if final else 3
    seqs = [refs[i * per_b:(i + 1) * per_b] for i in range(nb)]
    rest = refs[nb * per_b:]
    if final:
        dtb_ref, apad_ref, e_ref, dsk_ref, ng_ref, prev_ref, o_ref, st_ref, y_ref = rest
    else:
        dtb_ref, apad_ref, e_ref, o_ref, st_ref = rest
        y_ref = o_ref
    blk, inner = seqs[0][0].shape
    N = SSD_STATE
    P = SSD_HEAD_DIM
    C = SSD_CHUNK
    gw = inner // SSD_GROUPS
    pair = 2 * P

    @pl.when(pl.program_id(0) == 0)
    def _():
        st_ref[...] = jnp.zeros_like(st_ref)

    tri = _block_tri(blk, C, rev)
    e = e_ref[...]
    dt_fulls, cs_fulls = [], []
    for bi in range(nb):
        dt16 = _softplus(seqs[bi][2][...] + dtb_ref[...])
        cs16 = _dot_sum(tri, dt16 * apad_ref[...])
        dt_fulls.append(_dot_spread(dt16, e))
        cs_fulls.append(_dot_spread(cs16, e))

    ri = lax.broadcasted_iota(jnp.int32, (C, pair), 0)
    ci = lax.broadcasted_iota(jnp.int32, (C, pair), 1)
    cm = jnp.where(ci >= P, ci - P, ci)
    eye2 = cm == ri
    causal2 = (cm >= ri) if rev else (cm <= ri)
    left = ci < P
    zero = jnp.zeros((C, pair), BF16)

    n_chunks = blk // C
    for step in range(n_chunks):
        c = n_chunks - 1 - step if rev else step
        rs = slice(c * C, (c + 1) * C)
        last = c * C if rev else (c + 1) * C - 1
        units = [(g, bi) for g in range(SSD_GROUPS) for bi in range(nb)]
        held = []
        for g, bi in units:
            gs = slice(g * gw, (g + 1) * gw)
            x_ref, bc_ref = seqs[bi][:2]
            bm = bc_ref[rs, g * N:(g + 1) * N]
            cmat = bc_ref[rs, SSD_GROUPS * N + g * N:SSD_GROUPS * N + (g + 1) * N]
            cb2 = _dot_nt(cmat, jnp.concatenate([bm, bm], axis=0))
            cs_g = cs_fulls[bi][rs, gs]
            cs_last = cs_fulls[bi][last:last + 1, gs]
            xdt = x_ref[rs, gs].astype(F32) * dt_fulls[bi][rs, gs]
            st = st_ref[bi * SSD_GROUPS + g]
            y_inter = _dot(cmat, st.astype(BF16)) * jnp.exp(cs_g)
            w = (xdt * jnp.exp(cs_last - cs_g)).astype(BF16)
            st_ref[bi * SSD_GROUPS + g] = st * jnp.exp(cs_last) + _dot_tn(bm, w)
            held.append((cb2, cs_g, y_inter, xdt.astype(BF16)))
        for (g, bi), (cb2, cs_g, y_inter, xdt_b) in zip(units, held):
            x_ref = seqs[bi][0]
            for pr in range(gw // pair):
                ps = slice(pr * pair, (pr + 1) * pair)
                col = cs_g[:, ps]
                rowv = jnp.sum(jnp.where(eye2, col, 0.0), axis=0, keepdims=True)
                lmat = jnp.exp(jnp.where(causal2, col - rowv, -jnp.inf))
                mm = (lmat * cb2).astype(BF16)
                xp = xdt_b[:, ps]
                rhs = jnp.concatenate([jnp.where(left, xp, zero), jnp.where(left, zero, xp)], axis=0)
                y = _dot(mm, rhs) + y_inter[:, ps]
                os_ = slice(g * gw + pr * pair, g * gw + (pr + 1) * pair)
                if final:
                    y_ref[bi, rs, os_] = (y + prev_ref[bi, rs, os_]
                                          + x_ref[rs, os_].astype(F32) * dsk_ref[:, os_])
                else:
                    y_ref[bi, rs, os_] = y

    if final:
        rc = ROW_CHUNK
        for bi in range(nb):
            z_ref = seqs[bi][3]
            for r in range(blk // rc):
                rs = slice(r * rc, (r + 1) * rc)
                for g in range(SSD_GROUPS):
                    gs = slice(g * gw, (g + 1) * gw)
                    yg = y_ref[bi, rs, gs] * _silu(z_ref[rs, gs].astype(F32))
                    ms = jnp.mean(yg * yg, axis=-1, keepdims=True)
                    o_ref[bi, rs, gs] = (yg * lax.rsqrt(ms + EPS) * ng_ref[:, gs]).astype(o_ref.dtype)


def _ssd_dir(lay, xc, bcc, small, U, cols, dtb, apad, expand, rev, dsk=None, norm_g=None, prev=None):
    B, NS, blk = lay.B, lay.NS, SEQ_BLOCK
    inner = xc.shape[1]
    bcw = bcc.shape[1]
    final = prev is not None
    seq = lambda n: lay.seq_order(n, rev)
    const = lambda shape: pl.BlockSpec(shape, lambda n: (0,) * len(shape))

    def rows(w, b, col=0):
        return pl.BlockSpec((blk, w), lambda n: (lay.row_block(b, seq(n)), col))

    in_specs, args = [], []
    for b in range(B):
        in_specs += [rows(inner, b), rows(bcw, b), rows(SMALL_COLS, b)]
        args += [xc, bcc, small]
        if final:
            zc = cols["ssd_z"]
            in_specs.append(rows(inner, b, zc[0] // zc[1]))
            args.append(U)
    in_specs += [const((1, SMALL_COLS)), const((1, SMALL_COLS)), const((SMALL_COLS, inner))]
    args += [dtb, apad, expand]
    per_batch = pl.BlockSpec((B, blk, inner), lambda n: (0, seq(n), 0))
    if final:
        in_specs += [const((1, inner)), const((1, inner)), per_batch]
        args += [dsk, norm_g, prev]
    return pl.pallas_call(
        functools.partial(_ssd_kernel, nb=B, rev=rev, final=final),
        out_shape=jax.ShapeDtypeStruct((B, NS * blk, inner), BF16 if final else F32),
        grid=(NS,),
        in_specs=in_specs,
        out_specs=per_batch,
        scratch_shapes=[pltpu.VMEM((B * SSD_GROUPS, SSD_STATE, inner // SSD_GROUPS), F32)]
        + ([pltpu.VMEM((B, blk, inner), F32)] if final else []),
        compiler_params=_cparams(("arbitrary",)),
        name="ssd_bwd" if rev else "ssd_fwd",
    )(*args)


def _rope_tables(CTX, SEQ):
    hd = ATT_HEAD_DIM
    pairs = hd // 4
    rows = SEQ // GRID_W
    row = jnp.repeat(jnp.arange(rows, dtype=F32), GRID_W)
    col = jnp.tile(jnp.arange(GRID_W, dtype=F32), rows)
    inv = jnp.exp(-math.log(ROPE_THETA) * jnp.arange(pairs, dtype=F32) / pairs)
    ar = row[:, None] * inv
    ac = col[:, None] * inv
    ang = jnp.concatenate([ar, ar, ac, ac], axis=-1)
    cos, sin = jnp.cos(ang), jnp.sin(ang)
    first = (jnp.arange(hd) % (2 * pairs)) < pairs
    sa = jnp.where(first, -sin, 0.0)
    sb = jnp.where(first, 0.0, sin)
    ones = jnp.ones((CTX, hd), F32)
    zeros = jnp.zeros((CTX, hd), F32)
    return (jnp.concatenate([ones, cos]), jnp.concatenate([zeros, sa]), jnp.concatenate([zeros, sb]))


def _column_plan(D):
    mix = D // 2
    gla_v = mix
    gla_qk = mix // 2
    ssd_inner = mix
    ssd_bc = 2 * SSD_GROUPS * SSD_STATE
    order = [("gla_q", gla_qk), ("gla_k", gla_qk), ("gla_v", gla_v), ("gla_gate", gla_v),
             ("ssd_z", ssd_inner), ("ssd_x", ssd_inner), ("ssd_bc", ssd_bc)]
    cols, off = {}, 0
    for name, w in order:
        assert off % w == 0
        cols[name] = (off, w)
        off += w
    return cols, off


def _source_columns(D):
    mix = D // 2
    att_kv = ATT_KV_HEADS * ATT_HEAD_DIM
    ssd_bc = SSD_GROUPS * SSD_STATE
    heads = mix // SSD_HEAD_DIM
    widths = [("att_q", mix), ("att_k", att_kv), ("att_v", att_kv), ("gla_q", mix // 2), ("gla_k", mix // 2),
              ("gla_v", mix), ("gla_gate", mix), ("gla_lr", 2 * GLA_LOWRANK), ("ssd_z", mix),
              ("ssd_x", mix), ("ssd_bc", 2 * ssd_bc), ("ssd_dt", 2 * heads), ("gates", N_BRANCH * D)]
    src, off = {}, 0
    for name, w in widths:
        src[name] = (off, w)
        off += w
    return src, off


def kernel(x, c, ctx, c_ctx, norm1, norm2, w_ada, b_ada, w_in, att_q_norm, att_k_norm, gla_w_decay,
           gla_b_decay, gla_norm, ssd_conv_w, ssd_conv_b, ssd_dt_bias, ssd_a_log, ssd_d, ssd_norm,
           w_branch, w_out, w_ff1, w_ff2, final_norm):
    B, SEQ, D = x.shape
    CTX = ctx.shape[1]
    depth = w_in.shape[0]
    lay = _Layout(B, CTX, SEQ, D)
    cols, n_main = _column_plan(D)
    src, n_src = _source_columns(D)
    assert n_src == w_in.shape[2]
    mix = D // 2
    ssd_heads = mix // SSD_HEAD_DIM
    lr_w = 2 * GLA_LOWRANK
    assert lr_w + 2 * ssd_heads <= SMALL_COLS

    X = jnp.concatenate([ctx.reshape(B * CTX, D), x.reshape(B * SEQ, D)], axis=0)
    cvec = jnp.concatenate([c, c_ctx[None, :], jnp.zeros((MOD_ROWS - B - 1, D), F32)], axis=0)
    mods = _ada(cvec, w_ada, b_ada).reshape(depth, MOD_ROWS, 1, N_MOD * D)
    rope = _rope_tables(CTX, SEQ)

    expand = np.zeros((2, SMALL_COLS, mix), np.float32)
    for d in range(2):
        for h in range(ssd_heads):
            expand[d, lr_w + d * ssd_heads + h, h * SSD_HEAD_DIM:(h + 1) * SSD_HEAD_DIM] = 1.0
    expand = jnp.asarray(expand, BF16)

    lead = ["att_q", "att_k", "att_v", "gla_q", "gla_k", "gla_v", "gla_gate"]
    assert [src[n][0] for n in lead] == list(np.cumsum([0] + [src[n][1] for n in lead[:-1]]))
    assert [cols[n][0] for n in lead[3:]] == [src[n][0] - src["gla_q"][0] for n in lead[3:]]
    n_a = src["gla_gate"][0] + src["gla_gate"][1]
    w_in_b = w_in.astype(BF16)
    take = lambda name: w_in_b[:, :, src[name][0]:src[name][0] + src[name][1]]
    w_b = jnp.concatenate([take("ssd_z"), take("ssd_x"), take("ssd_bc")], axis=2)
    assert cols["ssd_z"][0] == n_a - src["gla_q"][0] and cols["ssd_x"][0] == cols["ssd_z"][0] + mix
    w_small = jnp.concatenate([take("gla_lr"), take("ssd_dt"),
                               jnp.zeros((depth, D, SMALL_COLS - lr_w - 2 * ssd_heads), BF16)], axis=2)
    w_gate = take("gates")
    wbr_b, wo_b = w_branch[0].astype(BF16), w_out[0].astype(BF16)
    w1_b, w2_b = w_ff1[0].astype(BF16), w_ff2[0].astype(BF16)
    w_branch_rows = w_branch.reshape(depth, N_BRANCH * mix, D)

    out = None
    for l in range(depth):
        mod = mods[l]
        g1 = norm1[l][None, :]

        U, small, qt, kk, vt, H = _inproj(lay, X, mod, g1, w_in_b, n_a, w_b, w_small, att_q_norm[l][None, :],
                                          att_k_norm[l][None, :], rope, l)

        o_att = _attention(lay, qt, kk, vt)

        gla_out = None
        for d in range(2):
            wd = jnp.zeros((SMALL_COLS, mix // 2), F32).at[d * GLA_LOWRANK:(d + 1) * GLA_LOWRANK].set(
                gla_w_decay[l, d]).astype(BF16)
            bd = gla_b_decay[l, d][None, :]
            if d == 0:
                gla_out = _gla_dir(lay, U, small, cols, wd, bd, rev=False)
            else:
                gla_out = _gla_dir(lay, U, small, cols, wd, bd, rev=True,
                                   norm_g=gla_norm[l][None, :], prev=gla_out)
        o_gla = gla_out

        cw, cbias = ssd_conv_w[l], ssd_conv_b[l]
        xc, bcc = _conv(lay, U, cols, cw[:, :mix], cbias[None, :mix], cw[:, mix:], cbias[None, mix:])
        ssd_out = None
        for d in range(2):
            lanes = slice(lr_w + d * ssd_heads, lr_w + (d + 1) * ssd_heads)
            dtb = jnp.zeros((1, SMALL_COLS), F32).at[0, lanes].set(ssd_dt_bias[l, d])
            apad = jnp.zeros((1, SMALL_COLS), F32).at[0, lanes].set(-jnp.exp(ssd_a_log[l, d]))
            if d == 0:
                ssd_out = _ssd_dir(lay, xc, bcc, small, U, cols, dtb, apad, expand[d], rev=False)
            else:
                ssd_out = _ssd_dir(lay, xc, bcc, small, U, cols, dtb, apad, expand[d], rev=True,
                                   dsk=jnp.repeat(ssd_d[l], SSD_HEAD_DIM)[None, :],
                                   norm_g=ssd_norm[l][None, :], prev=ssd_out)
        o_ssd = ssd_out

        last = l == depth - 1
        skip = B * CTX if last else 0
        if last:
            m = _merge(lay, H, o_att, o_gla, o_ssd, w_gate, wbr_b, l, skip)
            out = _mlp(lay, X, m, wo_b, mod, norm2[l][None, :], w1_b, w2_b, l, skip,
                       final_gain=final_norm[None, :])
        else:
            m, wbr_next, wo_next = _merge(lay, H, o_att, o_gla, o_ssd, w_gate, wbr_b, l, skip,
                                          cast_next=(w_branch_rows, w_out))
            X, w1_b, w2_b = _mlp(lay, X, m, wo_b, mod, norm2[l][None, :], w1_b, w2_b, l, skip,
                                 cast_next=(w_ff1, w_ff2))
            wbr_b, wo_b = wbr_next, wo_next
    return out.reshape(B, SEQ, D)
```

```python
import functools
import math

import jax
import jax.numpy as jnp
import numpy as np
from jax import lax
from jax.experimental import pallas as pl
from jax.experimental.pallas import tpu as pltpu

F32 = jnp.float32
BF16 = jnp.bfloat16

EPS = 1e-6
GRID_W = 64
N_BRANCH = 3
N_MOD = 6
ATT_HEAD_DIM = 128
ATT_KV_HEADS = 2
ROPE_THETA = 10000.0
GLA_HEADS = 4
GLA_LOWRANK = 16
GLA_TAU = 16.0
GLA_CHUNK = 64
SSD_HEAD_DIM = 64
SSD_GROUPS = 2
SSD_STATE = 128
SSD_CONV = 5
SSD_CHUNK = 64

SEQ_BLOCK = 256
RESIDENT_TM = 256
DOT_COLS = 512
MLP_FF_TILE = 1024
ADA_COLS = 1024
ROW_CHUNK = 64
MOD_ROWS = 8
SMALL_COLS = 128
HALO_ROWS = 16
ATT_ONES_ROWS = 16
ATT_PAIR_UNROLL = 4
VMEM_LIMIT = 56 * 1024 * 1024


def _cparams(sem):
    return pltpu.CompilerParams(dimension_semantics=sem, vmem_limit_bytes=VMEM_LIMIT)


def _dot(a, b):
    return jnp.dot(a, b, preferred_element_type=F32)


def _dot_nt(a, b):
    return lax.dot_general(a, b, (((1,), (1,)), ((), ())), preferred_element_type=F32)


def _dot_tn(a, b):
    return lax.dot_general(a, b, (((0,), (0,)), ((), ())), preferred_element_type=F32)


def _split2(x):
    hi = x.astype(BF16)
    lo = (x - hi.astype(F32)).astype(BF16)
    return hi, lo


def _dot_sum(a, x):
    hi, lo = _split2(x)
    return _dot(a, lo) + _dot(a, hi)


def _dot_spread(x, e):
    hi, lo = _split2(x)
    return _dot(lo, e) + _dot(hi, e)


def _silu(x):
    return x * jax.nn.sigmoid(x)


def _log1p_exp_neg_abs(x):
    return jnp.log(1.0 + jnp.exp(-jnp.abs(x)))


def _softplus(x):
    return jnp.maximum(x, 0.0) + _log1p_exp_neg_abs(x)


def _block_tri(n, chunk, rev):
    r = lax.broadcasted_iota(jnp.int32, (n, n), 0)
    c = lax.broadcasted_iota(jnp.int32, (n, n), 1)
    shift = chunk.bit_length() - 1
    assert chunk == 1 << shift
    same = jnp.right_shift(r, shift) == jnp.right_shift(c, shift)
    tri = (c >= r) if rev else (c <= r)
    return jnp.where(same, jnp.where(tri, 1.0, 0.0), 0.0).astype(BF16)


class _Layout:
    def __init__(self, B, CTX, SEQ, D):
        assert CTX % SEQ_BLOCK == 0 and SEQ % SEQ_BLOCK == 0
        self.B, self.CTX, self.SEQ, self.D = B, CTX, SEQ, D
        self.R = B * (CTX + SEQ)
        self.NC = CTX // SEQ_BLOCK
        self.NL = SEQ // SEQ_BLOCK
        self.NS = self.NC + self.NL
        self.tm = next(t for t in (512, 256) if (B * CTX) % t == 0 and SEQ % t == 0)
        assert B + 1 <= MOD_ROWS

    def row_block(self, b, s):
        return jnp.where(s < self.NC, b * self.NC + s, self.B * self.NC + b * self.NL + (s - self.NC))

    def batch_seq(self, i):
        nctx = self.B * self.NC
        j = jnp.maximum(i - nctx, 0)
        return (jnp.where(i < nctx, i // self.NC, j // self.NL),
                jnp.where(i < nctx, i % self.NC, self.NC + j % self.NL))

    def seq_order(self, n, rev):
        if not rev:
            return n
        return jnp.where(n < self.NC, self.NC - 1 - n, self.NC + self.NL - 1 - (n - self.NC))

    def mod_row(self, i, tm):
        nctx = (self.B * self.CTX) // tm
        return jnp.where(i < nctx, self.B, (i - nctx) // (self.SEQ // tm))


def _ada_kernel(c_ref, w_ref, b_ref, o_ref):
    s = _silu(c_ref[...]).astype(BF16)
    o_ref[0] = _dot(s, w_ref[0].astype(BF16)) + b_ref[0]


def _ada(cvec, w_ada, b_ada):
    depth, D, n = w_ada.shape
    tn = ADA_COLS
    return pl.pallas_call(
        _ada_kernel,
        out_shape=jax.ShapeDtypeStruct((depth, MOD_ROWS, n), F32),
        grid=(depth, n // tn),
        in_specs=[
            pl.BlockSpec((MOD_ROWS, D), lambda l, j: (0, 0)),
            pl.BlockSpec((1, D, tn), lambda l, j: (l, 0, j)),
            pl.BlockSpec((1, 1, tn), lambda l, j: (l, 0, j)),
        ],
        out_specs=pl.BlockSpec((1, MOD_ROWS, tn), lambda l, j: (l, 0, j)),
        compiler_params=_cparams(("arbitrary", "arbitrary")),
        name="ada",
    )(cvec, w_ada, b_ada.reshape(depth, 1, n))


def _norm_mod(x_ref, g_ref, mod_ref, k, h_ref):
    D = x_ref.shape[-1]
    shift = mod_ref[:, k * D:(k + 1) * D]
    a = g_ref[...] * (1.0 + mod_ref[:, (k + 1) * D:(k + 2) * D])
    rc = ROW_CHUNK

    def body(r, carry):
        sl = pl.ds(pl.multiple_of(r * rc, rc), rc)
        xf = x_ref[sl, :]
        ms = jnp.mean(xf * xf, axis=-1, keepdims=True)
        h_ref[sl, :] = (xf * lax.rsqrt(ms + EPS) * a + shift).astype(h_ref.dtype)
        return carry

    lax.fori_loop(0, x_ref.shape[0] // rc, body, 0)


def _resident(shape, layer=None):
    if layer is None:
        return pl.BlockSpec(shape, lambda *_: (0,) * len(shape), pipeline_mode=pl.Buffered(1))
    return pl.BlockSpec((None,) + tuple(shape), lambda *_: (layer,) + (0,) * len(shape),
                        pipeline_mode=pl.Buffered(1))


def _inproj_kernel(x_ref, mod_ref, g_ref, wa_ref, wb_ref, ws_ref, qg_ref, kg_ref, cos_ref, sa_ref, sb_ref,
                   u_ref, small_ref, qt_ref, ko_ref, vt_ref, h_ref, *, tn, att_q, att_kv):
    _norm_mod(x_ref, g_ref, mod_ref, 0, h_ref)
    h = h_ref[...]
    hd = ATT_HEAD_DIM
    cos, sa, sb = cos_ref[...], sa_ref[...], sb_ref[...]

    def prep(y, g):
        y = y * lax.rsqrt(jnp.mean(y * y, axis=-1, keepdims=True) + EPS) * g
        return y * cos + pltpu.roll(y, hd - hd // 4, 1) * sa + pltpu.roll(y, hd // 4, 1) * sb

    q_scale = (hd ** -0.5) * math.log2(math.e)
    for j in range(att_q // tn):
        q = _dot(h, wa_ref[:, j * tn:(j + 1) * tn])
        for hh in range(tn // hd):
            qh = prep(q[:, hh * hd:(hh + 1) * hd], qg_ref[...]) * q_scale
            qt_ref[j * (tn // hd) + hh] = qh.T.astype(qt_ref.dtype)
    kv = _dot(h, wa_ref[:, att_q:att_q + 2 * att_kv])
    for g in range(att_kv // hd):
        sl = slice(g * hd, (g + 1) * hd)
        ko_ref[:, sl] = prep(kv[:, sl], kg_ref[...]).astype(ko_ref.dtype)
        vt_ref[g, 0:hd, :] = kv[:, att_kv + g * hd:att_kv + (g + 1) * hd].T.astype(vt_ref.dtype)
        vt_ref[g, hd:, :] = jnp.ones((ATT_ONES_ROWS, x_ref.shape[0]), vt_ref.dtype)

    small_ref[...] = _dot(h, ws_ref[...])
    a0 = att_q + 2 * att_kv
    na = wa_ref.shape[1] - a0
    for j in range(na // tn):
        u_ref[:, j * tn:(j + 1) * tn] = _dot(h, wa_ref[:, a0 + j * tn:a0 + (j + 1) * tn]).astype(u_ref.dtype)
    for j in range(wb_ref.shape[1] // tn):
        u_ref[:, na + j * tn:na + (j + 1) * tn] = _dot(h, wb_ref[:, j * tn:(j + 1) * tn]).astype(u_ref.dtype)


def _inproj(lay, X, mod, g, w_a, n_a, w_b, w_small, q_gain, k_gain, rope, l):
    D = w_a.shape[1]
    n_b = w_b.shape[2]
    tm = RESIDENT_TM
    assert tm == SEQ_BLOCK
    hd = ATT_HEAD_DIM
    att_q = D // 2
    att_kv = ATT_KV_HEADS * hd
    heads = att_q // hd
    n_u = n_a - att_q - 2 * att_kv + n_b
    B, NS = lay.B, lay.NS
    bs = lay.batch_seq
    tab = pl.BlockSpec((tm, hd), lambda i: (bs(i)[1], 0))
    return pl.pallas_call(
        functools.partial(_inproj_kernel, tn=DOT_COLS, att_q=att_q, att_kv=att_kv),
        out_shape=(jax.ShapeDtypeStruct((lay.R, n_u), BF16),
                   jax.ShapeDtypeStruct((lay.R, SMALL_COLS), F32),
                   jax.ShapeDtypeStruct((lay.R // tm, heads, hd, tm), BF16),
                   jax.ShapeDtypeStruct((B, NS * tm, att_kv), BF16),
                   jax.ShapeDtypeStruct((B, ATT_KV_HEADS, NS, hd + ATT_ONES_ROWS, tm), BF16),
                   jax.ShapeDtypeStruct((lay.R, D), BF16)),
        grid=(lay.R // tm,),
        in_specs=[
            pl.BlockSpec((tm, D), lambda i: (i, 0)),
            pl.BlockSpec((None, 1, N_MOD * D), lambda i: (lay.mod_row(i, tm), 0, 0)),
            _resident((1, D)),
            _resident((D, n_a), l),
            _resident((D, n_b), l),
            _resident((D, SMALL_COLS), l),
            _resident((1, hd)), _resident((1, hd)),
            tab, tab, tab,
        ],
        out_specs=(pl.BlockSpec((tm, n_u), lambda i: (i, 0)),
                   pl.BlockSpec((tm, SMALL_COLS), lambda i: (i, 0)),
                   pl.BlockSpec((None, heads, hd, tm), lambda i: (i, 0, 0, 0)),
                   pl.BlockSpec((None, tm, att_kv), lambda i: (*bs(i), 0)),
                   pl.BlockSpec((None, ATT_KV_HEADS, None, hd + ATT_ONES_ROWS, tm),
                                lambda i: (bs(i)[0], 0, bs(i)[1], 0, 0)),
                   pl.BlockSpec((tm, D), lambda i: (i, 0))),
        compiler_params=_cparams(("arbitrary",)),
        name="in_proj",
    )(X, mod, g, w_a, w_b, w_small, q_gain, k_gain, *rope)


def _slab_count(n_steps, *row_counts):
    n = 1
    while 2 * n <= n_steps and all(r % (2 * n * 16) == 0 for r in row_counts):
        n *= 2
    return n


def _merge_kernel(h_ref, oa_ref, og_ref, os_ref, wg_ref, wb_ref, *rest, tn, cast_next):
    if cast_next:
        c1_ref, c2_ref, m_ref, n1_ref, n2_ref = rest
        n1_ref[...] = c1_ref[...].astype(n1_ref.dtype)
        n2_ref[...] = c2_ref[...].astype(n2_ref.dtype)
    else:
        (m_ref,) = rest
    h = h_ref[...]
    D = m_ref.shape[1]
    for j in range(D // tn):
        acc = None
        for b, o_ref in enumerate((oa_ref, og_ref, os_ref)):
            gate = _dot(h, wg_ref[:, b * D + j * tn:b * D + (j + 1) * tn])
            t = jax.nn.sigmoid(gate) * _dot(o_ref[...], wb_ref[b, :, j * tn:(j + 1) * tn])
            acc = t if acc is None else acc + t
        m_ref[:, j * tn:(j + 1) * tn] = acc.astype(m_ref.dtype)


def _merge(lay, H, o_att, o_gla, o_ssd, w_gate, w_branch, l, skip_rows, cast_next=None):
    D = lay.D
    W = o_att.shape[1]
    tm = RESIDENT_TM
    assert tm == SEQ_BLOCK
    row0 = skip_rows // tm
    rows = lambda i: (i + row0, 0)
    seq_rows = pl.BlockSpec((None, tm, W), lambda i: (*lay.batch_seq(i + row0), 0))
    n_i = lay.R // tm - row0
    in_specs = [
        pl.BlockSpec((tm, D), rows),
        pl.BlockSpec((tm, W), rows), seq_rows, seq_rows,
        _resident(w_gate.shape[1:], l), _resident(w_branch.shape),
    ]
    args = [H, o_att, o_gla, o_ssd, w_gate, w_branch]
    out_shape = [jax.ShapeDtypeStruct((lay.R - skip_rows, D), BF16)]
    out_specs = [pl.BlockSpec((tm, D), lambda i: (i, 0))]
    if cast_next is not None:
        n_slabs = _slab_count(n_i, N_BRANCH * W, D)
        r1, r2 = N_BRANCH * W // n_slabs, D // n_slabs
        slab = lambda i: jnp.minimum(i, n_slabs - 1)
        in_specs += [pl.BlockSpec((None, r1, D), lambda i: (l + 1, slab(i), 0)),
                     pl.BlockSpec((None, r2, D), lambda i: (l + 1, slab(i), 0))]
        args += list(cast_next)
        out_shape += [jax.ShapeDtypeStruct((N_BRANCH * W, D), BF16), jax.ShapeDtypeStruct((D, D), BF16)]
        out_specs += [pl.BlockSpec((r1, D), lambda i: (slab(i), 0)), pl.BlockSpec((r2, D), lambda i: (slab(i), 0))]
    res = pl.pallas_call(
        functools.partial(_merge_kernel, tn=DOT_COLS, cast_next=cast_next is not None),
        out_shape=tuple(out_shape),
        grid=(n_i,),
        in_specs=in_specs,
        out_specs=tuple(out_specs),
        compiler_params=_cparams(("arbitrary",)),
        name="merge",
    )(*args)
    if cast_next is None:
        return res[0]
    return res[0], res[1].reshape(N_BRANCH, W, D), res[2]


def _mlp_kernel(x_ref, m_ref, wo_ref, mod_ref, g_ref, w1_ref, w2_ref, *rest, final, cast_next):
    rest = list(rest)
    fg_ref = rest.pop(0) if final else None
    if cast_next:
        c1_ref, c2_ref = rest.pop(0), rest.pop(0)
        o_ref, n1_ref, n2_ref, x1_ref, h_ref = rest
        n1_ref[...] = c1_ref[...].astype(n1_ref.dtype)
        n2_ref[...] = c2_ref[...].astype(n2_ref.dtype)
    else:
        o_ref, x1_ref, h_ref = rest
    k = pl.program_id(1)
    D = x_ref.shape[-1]

    @pl.when(k == 0)
    def _():
        tn = DOT_COLS
        for j in range(D // tn):
            sl = slice(j * tn, (j + 1) * tn)
            x1_ref[:, sl] = x_ref[:, sl] + mod_ref[:, 2 * D + j * tn:2 * D + (j + 1) * tn] * _dot(
                m_ref[...], wo_ref[:, sl])
        _norm_mod(x1_ref, g_ref, mod_ref, 3, h_ref)
        o_ref[...] = jnp.zeros_like(o_ref)

    f = jnp.square(jnp.maximum(_dot(h_ref[...], w1_ref[...]), 0.0)).astype(BF16)
    o_ref[...] += _dot(f, w2_ref[...])

    @pl.when(k == pl.num_programs(1) - 1)
    def _():
        gate = mod_ref[:, 5 * D:6 * D]
        rc = ROW_CHUNK

        def body(r, carry):
            sl = pl.ds(pl.multiple_of(r * rc, rc), rc)
            y = x1_ref[sl, :] + gate * o_ref[sl, :]
            if final:
                ms = jnp.mean(y * y, axis=-1, keepdims=True)
                y = y * lax.rsqrt(ms + EPS) * fg_ref[...]
            o_ref[sl, :] = y
            return carry

        lax.fori_loop(0, x_ref.shape[0] // rc, body, 0)


def _mlp(lay, X, m, w_out, mod, g, w1, w2, l, skip_rows, final_gain=None, cast_next=None):
    D, FF = w1.shape
    tm, tf = lay.tm, MLP_FF_TILE
    row0 = skip_rows // tm
    final = final_gain is not None
    out_rows = lay.R - skip_rows if final else lay.R
    out_off = 0 if final else row0
    rows = lambda i, k: (i + row0, 0)
    n_i, n_k = lay.R // tm - row0, FF // tf
    assert m.shape[0] == lay.R - skip_rows
    in_specs = [
        pl.BlockSpec((tm, D), rows),
        pl.BlockSpec((tm, D), lambda i, k: (i, 0)),
        _resident((D, D)),
        pl.BlockSpec((None, 1, N_MOD * D), lambda i, k: (lay.mod_row(i + row0, tm), 0, 0)),
        _resident((1, D)),
        pl.BlockSpec((D, tf), lambda i, k: (0, k)),
        pl.BlockSpec((tf, D), lambda i, k: (k, 0)),
    ]
    args = [X, m, w_out, mod, g, w1, w2]
    out_shape = [jax.ShapeDtypeStruct((out_rows, D), F32)]
    out_specs = [pl.BlockSpec((tm, D), lambda i, k: (i + out_off, 0))]
    if final:
        in_specs.append(_resident((1, D)))
        args.append(final_gain)
    if cast_next is not None:
        n_slabs = _slab_count(n_i * n_k, D, FF)
        r1, r2 = D // n_slabs, FF // n_slabs
        slab = lambda i, k: jnp.minimum(i * n_k + k, n_slabs - 1)
        in_specs += [pl.BlockSpec((None, r1, FF), lambda i, k: (l + 1, slab(i, k), 0)),
                     pl.BlockSpec((None, r2, D), lambda i, k: (l + 1, slab(i, k), 0))]
        args += list(cast_next)
        out_shape += [jax.ShapeDtypeStruct((D, FF), BF16), jax.ShapeDtypeStruct((FF, D), BF16)]
        out_specs += [pl.BlockSpec((r1, FF), lambda i, k: (slab(i, k), 0)),
                      pl.BlockSpec((r2, D), lambda i, k: (slab(i, k), 0))]
    res = pl.pallas_call(
        functools.partial(_mlp_kernel, final=final, cast_next=cast_next is not None),
        out_shape=tuple(out_shape),
        grid=(n_i, n_k),
        in_specs=in_specs,
        out_specs=tuple(out_specs),
        scratch_shapes=[pltpu.VMEM((tm, D), F32), pltpu.VMEM((tm, D), BF16)],
        compiler_params=_cparams(("arbitrary", "arbitrary")),
        name="mlp",
    )(*args)
    return res if cast_next is not None else res[0]


def _attn_kernel(qt_ref, k_ref, vt_ref, o_ref, m_ref, acc_ref, s0_ref, s1_ref, *, n_ctx_blocks):
    n = pl.program_id(2)
    heads, hd, tq = qt_ref.shape
    nblk, vrows, tk = vt_ref.shape
    n_kv = jnp.where(n < n_ctx_blocks, n_ctx_blocks, nblk)
    m_ref[...] = jnp.full(m_ref.shape, -1e30, F32)
    acc_ref[...] = jnp.zeros(acc_ref.shape, F32)

    def scores(c, s_ref):
        c = jnp.minimum(c, n_kv - 1)
        k = k_ref[pl.ds(pl.multiple_of(c * tk, tk), tk), :]
        for h in range(heads):
            s_ref[h] = _dot(k, qt_ref[h])

    def update(c, s_ref):
        vt = vt_ref[c]
        for h in range(heads):
            st = s_ref[h]
            m_old = m_ref[h]
            m_new = jnp.maximum(m_old, jnp.max(st, axis=0, keepdims=True))
            pt = jnp.exp2((st - m_new).astype(BF16))
            acc_ref[h] = jnp.exp2(m_old - m_new) * acc_ref[h] + _dot(vt, pt)
            m_ref[h] = m_new

    scores(0, s0_ref)

    def pairs(c0, n_pairs):
        for j in range(n_pairs):
            scores(c0 + 2 * j + 1, s1_ref)
            update(c0 + 2 * j, s0_ref)
            scores(c0 + 2 * j + 2, s0_ref)
            update(c0 + 2 * j + 1, s1_ref)

    n_pairs = n_kv // 2
    unroll = ATT_PAIR_UNROLL

    def body(i, carry):
        pairs(2 * unroll * i, unroll)
        return carry

    lax.fori_loop(0, n_pairs // unroll, body, 0)
    done = (n_pairs // unroll) * unroll
    piece = unroll // 2
    while piece >= 1:
        @pl.when(((n_pairs - done) & piece) != 0)
        def _(done=done, piece=piece):
            pairs(2 * done, piece)

        done = done + ((n_pairs - done) & piece)
        piece //= 2

    @pl.when(n_kv % 2 == 1)
    def _():
        update(n_kv - 1, s0_ref)

    for h in range(heads):
        acc = acc_ref[h]
        ot = acc[0:hd] / acc[hd:hd + 1]
        o_ref[:, h * hd:(h + 1) * hd] = ot.T.astype(o_ref.dtype)


def _attention(lay, qt, kk, vt):
    B, kvh, NS, vrows, blk = vt.shape
    _, heads, hd, _ = qt.shape
    group = heads // kvh
    return pl.pallas_call(
        functools.partial(_attn_kernel, n_ctx_blocks=lay.NC),
        out_shape=jax.ShapeDtypeStruct((lay.R, heads * hd), BF16),
        grid=(B, kvh, NS),
        in_specs=[
            pl.BlockSpec((None, group, hd, blk), lambda b, g, n: (lay.row_block(b, n), g, 0, 0)),
            pl.BlockSpec((None, NS * blk, hd), lambda b, g, n: (b, 0, g)),
            pl.BlockSpec((None, None, NS, vrows, blk), lambda b, g, n: (b, g, 0, 0, 0)),
        ],
        out_specs=pl.BlockSpec((blk, group * hd), lambda b, g, n: (lay.row_block(b, n), g)),
        scratch_shapes=[pltpu.VMEM((group, 1, blk), F32), pltpu.VMEM((group, vrows, blk), F32),
                        pltpu.VMEM((group, blk, blk), F32), pltpu.VMEM((group, blk, blk), F32)],
        compiler_params=_cparams(("arbitrary", "arbitrary", "arbitrary")),
        name="attention",
    )(qt, kk, vt)


def _gla_kernel(*refs, nb, rev, final, dk, dv):
    per_b = 5 if final else 4
    seqs = [refs[i * per_b:(i + 1) * per_b] for i in range(nb)]
    rest = refs[nb * per_b:]
    if final:
        wd_ref, bd_ref, ng_ref, prev_ref, o_ref, st_ref = rest
    else:
        wd_ref, bd_ref, o_ref, st_ref = rest
    blk = seqs[0][0].shape[0]
    heads = seqs[0][0].shape[1] // dk
    C = GLA_CHUNK

    @pl.when(pl.program_id(0) == 0)
    def _():
        st_ref[...] = jnp.zeros_like(st_ref)

    tri = _block_tri(blk, C, rev)
    bcums = []
    for bi in range(nb):
        z = _dot(seqs[bi][3][...].astype(BF16), wd_ref[...]) + bd_ref[...]
        la = (jnp.minimum(z, 0.0) - _log1p_exp_neg_abs(z)) * (1.0 / GLA_TAU)
        bcums.append(_dot_sum(tri, la))

    ri = lax.broadcasted_iota(jnp.int32, (C, C), 0)
    ci = lax.broadcasted_iota(jnp.int32, (C, C), 1)
    causal = (ci >= ri) if rev else (ci <= ri)

    n_chunks = blk // C
    for step in range(n_chunks):
        c = n_chunks - 1 - step if rev else step
        rs = slice(c * C, (c + 1) * C)
        last = c * C if rev else (c + 1) * C - 1
        units = [(h, bi) for h in range(heads) for bi in range(nb)]
        scores, inter = [], []
        for h, bi in units:
            ks = slice(h * dk, (h + 1) * dk)
            q_ref, k_ref, v_ref = seqs[bi][:3]
            b = bcums[bi][rs, ks]
            b_last = bcums[bi][last:last + 1, ks]
            q = q_ref[rs, ks].astype(F32) * (dk ** -0.5)
            k = k_ref[rs, ks].astype(F32)
            q_dec = (q * jnp.exp(b)).astype(BF16)
            k_dec = (k * jnp.exp(-b)).astype(BF16)
            k_st = (k * jnp.exp(b_last - b)).astype(BF16)
            st = st_ref[bi * heads + h]
            scores.append(_dot_nt(q_dec, k_dec))
            inter.append(_dot_nt(q_dec, st.astype(BF16)))
            st_ref[bi * heads + h] = st * jnp.exp(b_last) + _dot_tn(v_ref[rs, h * dv:(h + 1) * dv], k_st)
        for (h, bi), sc, it in zip(units, scores, inter):
            vs = slice(h * dv, (h + 1) * dv)
            att = jnp.where(causal, sc, 0.0).astype(BF16)
            o = _dot(att, seqs[bi][2][rs, vs]) + it
            if final:
                tot = o + prev_ref[bi, rs, vs]
                y = tot * lax.rsqrt(jnp.mean(tot * tot, axis=-1, keepdims=True) + EPS) * ng_ref[...]
                o_ref[bi, rs, vs] = (y * _silu(seqs[bi][4][rs, vs].astype(F32))).astype(o_ref.dtype)
            else:
                o_ref[bi, rs, vs] = o


def _gla_dir(lay, U, small, cols, w_dec, b_dec, rev, norm_g=None, prev=None):
    B, NS, blk = lay.B, lay.NS, SEQ_BLOCK
    qk_w = cols["gla_q"][1]
    v_w = cols["gla_v"][1]
    dk, dv = qk_w // GLA_HEADS, v_w // GLA_HEADS
    final = prev is not None
    const = lambda shape: pl.BlockSpec(shape, lambda n: (0,) * len(shape))
    seq = lambda n: lay.seq_order(n, rev)

    def ucol(name, b):
        off, w = cols[name]
        return pl.BlockSpec((blk, w), lambda n: (lay.row_block(b, seq(n)), off // w))

    in_specs, args = [], []
    for b in range(B):
        in_specs += [ucol("gla_q", b), ucol("gla_k", b), ucol("gla_v", b),
                     pl.BlockSpec((blk, SMALL_COLS), lambda n, b=b: (lay.row_block(b, seq(n)), 0))]
        args += [U, U, U, small]
        if final:
            in_specs.append(ucol("gla_gate", b))
            args.append(U)
    in_specs += [const((SMALL_COLS, qk_w)), const((1, qk_w))]
    args += [w_dec, b_dec]
    per_batch = pl.BlockSpec((B, blk, v_w), lambda n: (0, seq(n), 0))
    if final:
        in_specs += [const((1, dv)), per_batch]
        args += [norm_g, prev]
    return pl.pallas_call(
        functools.partial(_gla_kernel, nb=B, rev=rev, final=final, dk=dk, dv=dv),
        out_shape=jax.ShapeDtypeStruct((B, NS * blk, v_w), BF16 if final else F32),
        grid=(NS,),
        in_specs=in_specs,
        out_specs=per_batch,
        scratch_shapes=[pltpu.VMEM((B * GLA_HEADS, dv, dk), F32)],
        compiler_params=_cparams(("arbitrary",)),
        name="gla_bwd" if rev else "gla_fwd",
    )(*args)


def _conv_kernel(x_ref, xp_ref, xn_ref, bc_ref, bcp_ref, bcn_ref, wx_ref, bx_ref, wbc_ref, bbc_ref,
                 xo_ref, bco_ref, *, n_ctx_total, NC, NL):
    i = pl.program_id(0)
    in_ctx = i < n_ctx_total
    p = jnp.where(in_ctx, i % NC, jnp.maximum(i - n_ctx_total, 0) % NL)
    seg = jnp.where(in_ctx, NC, NL)
    has_left = (p > 0).astype(F32)
    has_right = (p < seg - 1).astype(F32)
    blk = x_ref.shape[0]
    pad = SSD_CONV // 2
    edge = 8
    assert pad <= edge <= HALO_ROWS

    r = lax.broadcasted_iota(jnp.int32, (blk, blk), 0)
    c = lax.broadcasted_iota(jnp.int32, (blk, blk), 1)
    re = lax.broadcasted_iota(jnp.int32, (edge, HALO_ROWS), 0)
    ce = lax.broadcasted_iota(jnp.int32, (edge, HALO_ROWS), 1)
    taps = [j - pad for j in range(SSD_CONV)]
    shift = {s: jnp.where(c == r + s, 1.0, 0.0).astype(BF16) for s in taps if s != 0}
    head = {s: (jnp.where(ce == HALO_ROWS + re + s, 1.0, 0.0) * has_left).astype(BF16) for s in taps if s < 0}
    tail = {s: (jnp.where(ce == re + s - edge, 1.0, 0.0) * has_right).astype(BF16) for s in taps if s > 0}

    def run(m_ref, p_ref, n_ref, w_ref, b_ref, o_ref):
        main = m_ref[...]
        acc = b_ref[...] + w_ref[pad:pad + 1, :] * main.astype(F32)
        for j, s in enumerate(taps):
            if s != 0:
                acc = acc + w_ref[j:j + 1, :] * _dot(shift[s], main)
        first, last = acc[0:edge], acc[blk - edge:blk]
        for j, s in enumerate(taps):
            if s < 0:
                first = first + w_ref[j:j + 1, :] * _dot(head[s], p_ref[...])
            if s > 0:
                last = last + w_ref[j:j + 1, :] * _dot(tail[s], n_ref[...])
        o_ref[0:edge, :] = _silu(first).astype(o_ref.dtype)
        o_ref[edge:blk - edge, :] = _silu(acc[edge:blk - edge]).astype(o_ref.dtype)
        o_ref[blk - edge:blk, :] = _silu(last).astype(o_ref.dtype)

    run(x_ref, xp_ref, xn_ref, wx_ref, bx_ref, xo_ref)
    run(bc_ref, bcp_ref, bcn_ref, wbc_ref, bbc_ref, bco_ref)


def _conv(lay, U, cols, wx, bx, wbc, bbc):
    blk = SEQ_BLOCK
    nb = lay.R // blk
    hb = blk // HALO_ROWS
    nh = lay.R // HALO_ROWS
    xw = cols["ssd_x"][1]
    bcw = cols["ssd_bc"][1]
    xc, bcc = cols["ssd_x"][0] // xw, cols["ssd_bc"][0] // bcw
    prev = lambda i: jnp.maximum(i * hb - 1, 0)
    nxt = lambda i: jnp.minimum((i + 1) * hb, nh - 1)
    const = lambda shape: pl.BlockSpec(shape, lambda i: (0,) * len(shape))
    return pl.pallas_call(
        functools.partial(_conv_kernel, n_ctx_total=lay.B * lay.NC, NC=lay.NC, NL=lay.NL),
        out_shape=(jax.ShapeDtypeStruct((lay.R, xw), BF16), jax.ShapeDtypeStruct((lay.R, bcw), BF16)),
        grid=(nb,),
        in_specs=[
            pl.BlockSpec((blk, xw), lambda i: (i, xc)),
            pl.BlockSpec((HALO_ROWS, xw), lambda i: (prev(i), xc)),
            pl.BlockSpec((HALO_ROWS, xw), lambda i: (nxt(i), xc)),
            pl.BlockSpec((blk, bcw), lambda i: (i, bcc)),
            pl.BlockSpec((HALO_ROWS, bcw), lambda i: (prev(i), bcc)),
            pl.BlockSpec((HALO_ROWS, bcw), lambda i: (nxt(i), bcc)),
            const((SSD_CONV, xw)), const((1, xw)), const((SSD_CONV, bcw)), const((1, bcw)),
        ],
        out_specs=(pl.BlockSpec((blk, xw), lambda i: (i, 0)), pl.BlockSpec((blk, bcw), lambda i: (i, 0))),
        compiler_params=_cparams(("arbitrary",)),
        name="ssd_conv",
    )(U, U, U, U, U, U, wx, bx, wbc, bbc)


def _ssd_kernel(*refs, nb, rev, final):
    per_b = 4 if final else 3
    seqs = [refs[i * per_b:(i + 1) * per_b] for i in range(nb)]
    rest = refs[nb * per_b:]
    if final:
        dtb_ref, apad_ref, e_ref, dsk_ref, ng_ref, prev_ref, o_ref, st_ref, y_ref = rest
    else:
        dtb_ref, apad_ref, e_ref, o_ref, st_ref = rest
        y_ref = o_ref
    blk, inner = seqs[0][0].shape
    N = SSD_STATE
    P = SSD_HEAD_DIM
    C = SSD_CHUNK
    gw = inner // SSD_GROUPS
    pair = 2 * P

    @pl.when(pl.program_id(0) == 0)
    def _():
        st_ref[...] = jnp.zeros_like(st_ref)

    tri = _block_tri(blk, C, rev)
    e = e_ref[...]
    dt_fulls, cs_fulls = [], []
    for bi in range(nb):
        dt16 = _softplus(seqs[bi][2][...] + dtb_ref[...])
        cs16 = _dot_sum(tri, dt16 * apad_ref[...])
        dt_fulls.append(_dot_spread(dt16, e))
        cs_fulls.append(_dot_spread(cs16, e))

    ri = lax.broadcasted_iota(jnp.int32, (C, pair), 0)
    ci = lax.broadcasted_iota(jnp.int32, (C, pair), 1)
    cm = jnp.where(ci >= P, ci - P, ci)
    eye2 = cm == ri
    causal2 = (cm >= ri) if rev else (cm <= ri)
    left = ci < P
    zero = jnp.zeros((C, pair), BF16)

    n_chunks = blk // C
    for step in range(n_chunks):
        c = n_chunks - 1 - step if rev else step
        rs = slice(c * C, (c + 1) * C)
        last = c * C if rev else (c + 1) * C - 1
        units = [(g, bi) for g in range(SSD_GROUPS) for bi in range(nb)]
        held = []
        for g, bi in units:
            gs = slice(g * gw, (g + 1) * gw)
            x_ref, bc_ref = seqs[bi][:2]
            bm = bc_ref[rs, g * N:(g + 1) * N]
            cmat = bc_ref[rs, SSD_GROUPS * N + g * N:SSD_GROUPS * N + (g + 1) * N]
            cb2 = _dot_nt(cmat, jnp.concatenate([bm, bm], axis=0))
            cs_g = cs_fulls[bi][rs, gs]
            cs_last = cs_fulls[bi][last:last + 1, gs]
            xdt = x_ref[rs, gs].astype(F32) * dt_fulls[bi][rs, gs]
            st = st_ref[bi * SSD_GROUPS + g]
            y_inter = _dot(cmat, st.astype(BF16)) * jnp.exp(cs_g)
            w = (xdt * jnp.exp(cs_last - cs_g)).astype(BF16)
            st_ref[bi * SSD_GROUPS + g] = st * jnp.exp(cs_last) + _dot_tn(bm, w)
            held.append((cb2, cs_g, y_inter, xdt.astype(BF16)))
        for (g, bi), (cb2, cs_g, y_inter, xdt_b) in zip(units, held):
            x_ref = seqs[bi][0]
            for pr in range(gw // pair):
                ps = slice(pr * pair, (pr + 1) * pair)
                col = cs_g[:, ps]
                rowv = jnp.sum(jnp.where(eye2, col, 0.0), axis=0, keepdims=True)
                lmat = jnp.exp(jnp.where(causal2, col - rowv, -jnp.inf))
                mm = (lmat * cb2).astype(BF16)
                xp = xdt_b[:, ps]
                rhs = jnp.concatenate([jnp.where(left, xp, zero), jnp.where(left, zero, xp)], axis=0)
                y = _dot(mm, rhs) + y_inter[:, ps]
                os_ = slice(g * gw + pr * pair, g * gw + (pr + 1) * pair)
                if final:
                    y_ref[bi, rs, os_] = (y + prev_ref[bi, rs, os_]
                                          + x_ref[rs, os_].astype(F32) * dsk_ref[:, os_])
                else:
                    y_ref[bi, rs, os_] = y

    if final:
        rc = ROW_CHUNK
        for bi in range(nb):
            z_ref = seqs[bi][3]
            for r in range(blk // rc):
                rs = slice(r * rc, (r + 1) * rc)
                for g in range(SSD_GROUPS):
                    gs = slice(g * gw, (g + 1) * gw)
                    yg = y_ref[bi, rs, gs] * _silu(z_ref[rs, gs].astype(F32))
                    ms = jnp.mean(yg * yg, axis=-1, keepdims=True)
                    o_ref[bi, rs, gs] = (yg * lax.rsqrt(ms + EPS) * ng_ref[:, gs]).astype(o_ref.dtype)


def _ssd_dir(lay, xc, bcc, small, U, cols, dtb, apad, expand, rev, dsk=None, norm_g=None, prev=None):
    B, NS, blk = lay.B, lay.NS, SEQ_BLOCK
    inner = xc.shape[1]
    bcw = bcc.shape[1]
    final = prev is not None
    seq = lambda n: lay.seq_order(n, rev)
    const = lambda shape: pl.BlockSpec(shape, lambda n: (0,) * len(shape))

    def rows(w, b, col=0):
        return pl.BlockSpec((blk, w), lambda n: (lay.row_block(b, seq(n)), col))

    in_specs, args = [], []
    for b in range(B):
        in_specs += [rows(inner, b), rows(bcw, b), rows(SMALL_COLS, b)]
        args += [xc, bcc, small]
        if final:
            zc = cols["ssd_z"]
            in_specs.append(rows(inner, b, zc[0] // zc[1]))
            args.append(U)
    in_specs += [const((1, SMALL_COLS)), const((1, SMALL_COLS)), const((SMALL_COLS, inner))]
    args += [dtb, apad, expand]
    per_batch = pl.BlockSpec((B, blk, inner), lambda n: (0, seq(n), 0))
    if final:
        in_specs += [const((1, inner)), const((1, inner)), per_batch]
        args += [dsk, norm_g, prev]
    return pl.pallas_call(
        functools.partial(_ssd_kernel, nb=B, rev=rev, final=final),
        out_shape=jax.ShapeDtypeStruct((B, NS * blk, inner), BF16 if final else F32),
        grid=(NS,),
        in_specs=in_specs,
        out_specs=per_batch,
        scratch_shapes=[pltpu.VMEM((B * SSD_GROUPS, SSD_STATE, inner // SSD_GROUPS), F32)]
        + ([pltpu.VMEM((B, blk, inner), F32)] if final else []),
        compiler_params=_cparams(("arbitrary",)),
        name="ssd_bwd" if rev else "ssd_fwd",
    )(*args)


def _rope_tables(CTX, SEQ):
    hd = ATT_HEAD_DIM
    pairs = hd // 4
    rows = SEQ // GRID_W
    row = jnp.repeat(jnp.arange(rows, dtype=F32), GRID_W)
    col = jnp.tile(jnp.arange(GRID_W, dtype=F32), rows)
    inv = jnp.exp(-math.log(ROPE_THETA) * jnp.arange(pairs, dtype=F32) / pairs)
    ar = row[:, None] * inv
    ac = col[:, None] * inv
    ang = jnp.concatenate([ar, ar, ac, ac], axis=-1)
    cos, sin = jnp.cos(ang), jnp.sin(ang)
    first = (jnp.arange(hd) % (2 * pairs)) < pairs
    sa = jnp.where(first, -sin, 0.0)
    sb = jnp.where(first, 0.0, sin)
    ones = jnp.ones((CTX, hd), F32)
    zeros = jnp.zeros((CTX, hd), F32)
    return (jnp.concatenate([ones, cos]), jnp.concatenate([zeros, sa]), jnp.concatenate([zeros, sb]))


def _column_plan(D):
    mix = D // 2
    gla_v = mix
    gla_qk = mix // 2
    ssd_inner = mix
    ssd_bc = 2 * SSD_GROUPS * SSD_STATE
    order = [("gla_q", gla_qk), ("gla_k", gla_qk), ("gla_v", gla_v), ("gla_gate", gla_v),
             ("ssd_z", ssd_inner), ("ssd_x", ssd_inner), ("ssd_bc", ssd_bc)]
    cols, off = {}, 0
    for name, w in order:
        assert off % w == 0
        cols[name] = (off, w)
        off += w
    return cols, off


def _source_columns(D):
    mix = D // 2
    att_kv = ATT_KV_HEADS * ATT_HEAD_DIM
    ssd_bc = SSD_GROUPS * SSD_STATE
    heads = mix // SSD_HEAD_DIM
    widths = [("att_q", mix), ("att_k", att_kv), ("att_v", att_kv), ("gla_q", mix // 2), ("gla_k", mix // 2),
              ("gla_v", mix), ("gla_gate", mix), ("gla_lr", 2 * GLA_LOWRANK), ("ssd_z", mix),
              ("ssd_x", mix), ("ssd_bc", 2 * ssd_bc), ("ssd_dt", 2 * heads), ("gates", N_BRANCH * D)]
    src, off = {}, 0
    for name, w in widths:
        src[name] = (off, w)
        off += w
    return src, off


def kernel(x, c, ctx, c_ctx, norm1, norm2, w_ada, b_ada, w_in, att_q_norm, att_k_norm, gla_w_decay,
           gla_b_decay, gla_norm, ssd_conv_w, ssd_conv_b, ssd_dt_bias, ssd_a_log, ssd_d, ssd_norm,
           w_branch, w_out, w_ff1, w_ff2, final_norm):
    B, SEQ, D = x.shape
    CTX = ctx.shape[1]
    depth = w_in.shape[0]
    lay = _Layout(B, CTX, SEQ, D)
    cols, n_main = _column_plan(D)
    src, n_src = _source_columns(D)
    assert n_src == w_in.shape[2]
    mix = D // 2
    ssd_heads = mix // SSD_HEAD_DIM
    lr_w = 2 * GLA_LOWRANK
    assert lr_w + 2 * ssd_heads <= SMALL_COLS

    X = jnp.concatenate([ctx.reshape(B * CTX, D), x.reshape(B * SEQ, D)], axis=0)
    cvec = jnp.concatenate([c, c_ctx[None, :], jnp.zeros((MOD_ROWS - B - 1, D), F32)], axis=0)
    mods = _ada(cvec, w_ada, b_ada).reshape(depth, MOD_ROWS, 1, N_MOD * D)
    rope = _rope_tables(CTX, SEQ)

    expand = np.zeros((2, SMALL_COLS, mix), np.float32)
    for d in range(2):
        for h in range(ssd_heads):
            expand[d, lr_w + d * ssd_heads + h, h * SSD_HEAD_DIM:(h + 1) * SSD_HEAD_DIM] = 1.0
    expand = jnp.asarray(expand, BF16)

    lead = ["att_q", "att_k", "att_v", "gla_q", "gla_k", "gla_v", "gla_gate"]
    assert [src[n][0] for n in lead] == list(np.cumsum([0] + [src[n][1] for n in lead[:-1]]))
    assert [cols[n][0] for n in lead[3:]] == [src[n][0] - src["gla_q"][0] for n in lead[3:]]
    n_a = src["gla_gate"][0] + src["gla_gate"][1]
    w_in_b = w_in.astype(BF16)
    take = lambda name: w_in_b[:, :, src[name][0]:src[name][0] + src[name][1]]
    w_b = jnp.concatenate([take("ssd_z"), take("ssd_x"), take("ssd_bc")], axis=2)
    assert cols["ssd_z"][0] == n_a - src["gla_q"][0] and cols["ssd_x"][0] == cols["ssd_z"][0] + mix
    w_small = jnp.concatenate([take("gla_lr"), take("ssd_dt"),
                               jnp.zeros((depth, D, SMALL_COLS - lr_w - 2 * ssd_heads), BF16)], axis=2)
    w_gate = take("gates")
    wbr_b, wo_b = w_branch[0].astype(BF16), w_out[0].astype(BF16)
    w1_b, w2_b = w_ff1[0].astype(BF16), w_ff2[0].astype(BF16)
    w_branch_rows = w_branch.reshape(depth, N_BRANCH * mix, D)

    out = None
    for l in range(depth):
        mod = mods[l]
        g1 = norm1[l][None, :]

        U, small, qt, kk, vt, H = _inproj(lay, X, mod, g1, w_in_b, n_a, w_b, w_small, att_q_norm[l][None, :],
                                          att_k_norm[l][None, :], rope, l)

        o_att = _attention(lay, qt, kk, vt)

        gla_out = None
        for d in range(2):
            wd = jnp.zeros((SMALL_COLS, mix // 2), F32).at[d * GLA_LOWRANK:(d + 1) * GLA_LOWRANK].set(
                gla_w_decay[l, d]).astype(BF16)
            bd = gla_b_decay[l, d][None, :]
            if d == 0:
                gla_out = _gla_dir(lay, U, small, cols, wd, bd, rev=False)
            else:
                gla_out = _gla_dir(lay, U, small, cols, wd, bd, rev=True,
                                   norm_g=gla_norm[l][None, :], prev=gla_out)
        o_gla = gla_out

        cw, cbias = ssd_conv_w[l], ssd_conv_b[l]
        xc, bcc = _conv(lay, U, cols, cw[:, :mix], cbias[None, :mix], cw[:, mix:], cbias[None, mix:])
        ssd_out = None
        for d in range(2):
            lanes = slice(lr_w + d * ssd_heads, lr_w + (d + 1) * ssd_heads)
            dtb = jnp.zeros((1, SMALL_COLS), F32).at[0, lanes].set(ssd_dt_bias[l, d])
            apad = jnp.zeros((1, SMALL_COLS), F32).at[0, lanes].set(-jnp.exp(ssd_a_log[l, d]))
            if d == 0:
                ssd_out = _ssd_dir(lay, xc, bcc, small, U, cols, dtb, apad, expand[d], rev=False)
            else:
                ssd_out = _ssd_dir(lay, xc, bcc, small, U, cols, dtb, apad, expand[d], rev=True,
                                   dsk=jnp.repeat(ssd_d[l], SSD_HEAD_DIM)[None, :],
                                   norm_g=ssd_norm[l][None, :], prev=ssd_out)
        o_ssd = ssd_out

        last = l == depth - 1
        skip = B * CTX if last else 0
        if last:
            m = _merge(lay, H, o_att, o_gla, o_ssd, w_gate, wbr_b, l, skip)
            out = _mlp(lay, X, m, wo_b, mod, norm2[l][None, :], w1_b, w2_b, l, skip,
                       final_gain=final_norm[None, :])
        else:
            m, wbr_next, wo_next = _merge(lay, H, o_att, o_gla, o_ssd, w_gate, wbr_b, l, skip,
                                          cast_next=(w_branch_rows, w_out))
            X, w1_b, w2_b = _mlp(lay, X, m, wo_b, mod, norm2[l][None, :], w1_b, w2_b, l, skip,
                                 cast_next=(w_ff1, w_ff2))
            wbr_b, wo_b = wbr_next, wo_next
    return out.reshape(B, SEQ, D)
```

```python
import functools
import math

import jax
import jax.numpy as jnp
import numpy as np
from jax import lax
from jax.experimental import pallas as pl
from jax.experimental.pallas import tpu as pltpu

F32 = jnp.float32
BF16 = jnp.bfloat16

EPS = 1e-6
GRID_W = 64
N_BRANCH = 3
N_MOD = 6
ATT_HEAD_DIM = 128
ATT_KV_HEADS = 2
ROPE_THETA = 10000.0
GLA_HEADS = 4
GLA_LOWRANK = 16
GLA_TAU = 16.0
GLA_CHUNK = 64
SSD_HEAD_DIM = 64
SSD_GROUPS = 2
SSD_STATE = 128
SSD_CONV = 5
SSD_CHUNK = 64

SEQ_BLOCK = 256
RESIDENT_TM = 256
DOT_COLS = 512
MLP_FF_TILE = 1024
ADA_COLS = 1024
ROW_CHUNK = 64
MOD_ROWS = 8
SMALL_COLS = 128
HALO_ROWS = 16
ATT_ONES_ROWS = 16
ATT_PAIR_UNROLL = 4
VMEM_LIMIT = 56 * 1024 * 1024


def _cparams(sem):
    return pltpu.CompilerParams(dimension_semantics=sem, vmem_limit_bytes=VMEM_LIMIT)


def _dot(a, b):
    return jnp.dot(a, b, preferred_element_type=F32)


def _dot_nt(a, b):
    return lax.dot_general(a, b, (((1,), (1,)), ((), ())), preferred_element_type=F32)


def _dot_tn(a, b):
    return lax.dot_general(a, b, (((0,), (0,)), ((), ())), preferred_element_type=F32)


def _split2(x):
    hi = x.astype(BF16)
    lo = (x - hi.astype(F32)).astype(BF16)
    return hi, lo


def _dot_sum(a, x):
    hi, lo = _split2(x)
    return _dot(a, lo) + _dot(a, hi)


def _dot_spread(x, e):
    hi, lo = _split2(x)
    return _dot(lo, e) + _dot(hi, e)


def _silu(x):
    return x * jax.nn.sigmoid(x)


def _log1p_exp_neg_abs(x):
    return jnp.log(1.0 + jnp.exp(-jnp.abs(x)))


def _softplus(x):
    return jnp.maximum(x, 0.0) + _log1p_exp_neg_abs(x)


def _block_tri(n, chunk, rev):
    r = lax.broadcasted_iota(jnp.int32, (n, n), 0)
    c = lax.broadcasted_iota(jnp.int32, (n, n), 1)
    shift = chunk.bit_length() - 1
    assert chunk == 1 << shift
    same = jnp.right_shift(r, shift) == jnp.right_shift(c, shift)
    tri = (c >= r) if rev else (c <= r)
    return jnp.where(same, jnp.where(tri, 1.0, 0.0), 0.0).astype(BF16)


class _Layout:
    def __init__(self, B, CTX, SEQ, D):
        assert CTX % SEQ_BLOCK == 0 and SEQ % SEQ_BLOCK == 0
        self.B, self.CTX, self.SEQ, self.D = B, CTX, SEQ, D
        self.R = B * (CTX + SEQ)
        self.NC = CTX // SEQ_BLOCK
        self.NL = SEQ // SEQ_BLOCK
        self.NS = self.NC + self.NL
        self.tm = next(t for t in (512, 256) if (B * CTX) % t == 0 and SEQ % t == 0)
        assert B + 1 <= MOD_ROWS

    def row_block(self, b, s):
        return jnp.where(s < self.NC, b * self.NC + s, self.B * self.NC + b * self.NL + (s - self.NC))

    def batch_seq(self, i):
        nctx = self.B * self.NC
        j = jnp.maximum(i - nctx, 0)
        return (jnp.where(i < nctx, i // self.NC, j // self.NL),
                jnp.where(i < nctx, i % self.NC, self.NC + j % self.NL))

    def seq_order(self, n, rev):
        if not rev:
            return n
        return jnp.where(n < self.NC, self.NC - 1 - n, self.NC + self.NL - 1 - (n - self.NC))

    def mod_row(self, i, tm):
        nctx = (self.B * self.CTX) // tm
        return jnp.where(i < nctx, self.B, (i - nctx) // (self.SEQ // tm))


def _ada_kernel(c_ref, w_ref, b_ref, o_ref):
    s = _silu(c_ref[...]).astype(BF16)
    o_ref[0] = _dot(s, w_ref[0].astype(BF16)) + b_ref[0]


def _ada(cvec, w_ada, b_ada):
    depth, D, n = w_ada.shape
    tn = ADA_COLS
    return pl.pallas_call(
        _ada_kernel,
        out_shape=jax.ShapeDtypeStruct((depth, MOD_ROWS, n), F32),
        grid=(depth, n // tn),
        in_specs=[
            pl.BlockSpec((MOD_ROWS, D), lambda l, j: (0, 0)),
            pl.BlockSpec((1, D, tn), lambda l, j: (l, 0, j)),
            pl.BlockSpec((1, 1, tn), lambda l, j: (l, 0, j)),
        ],
        out_specs=pl.BlockSpec((1, MOD_ROWS, tn), lambda l, j: (l, 0, j)),
        compiler_params=_cparams(("arbitrary", "arbitrary")),
        name="ada",
    )(cvec, w_ada, b_ada.reshape(depth, 1, n))


def _norm_mod(x_ref, g_ref, mod_ref, k, h_ref):
    D = x_ref.shape[-1]
    shift = mod_ref[:, k * D:(k + 1) * D]
    a = g_ref[...] * (1.0 + mod_ref[:, (k + 1) * D:(k + 2) * D])
    rc = ROW_CHUNK

    def body(r, carry):
        sl = pl.ds(pl.multiple_of(r * rc, rc), rc)
        xf = x_ref[sl, :]
        ms = jnp.mean(xf * xf, axis=-1, keepdims=True)
        h_ref[sl, :] = (xf * lax.rsqrt(ms + EPS) * a + shift).astype(h_ref.dtype)
        return carry

    lax.fori_loop(0, x_ref.shape[0] // rc, body, 0)


def _resident(shape, layer=None):
    if layer is None:
        return pl.BlockSpec(shape, lambda *_: (0,) * len(shape), pipeline_mode=pl.Buffered(1))
    return pl.BlockSpec((None,) + tuple(shape), lambda *_: (layer,) + (0,) * len(shape),
                        pipeline_mode=pl.Buffered(1))


def _inproj_kernel(x_ref, mod_ref, g_ref, wa_ref, wb_ref, ws_ref, qg_ref, kg_ref, cos_ref, sa_ref, sb_ref,
                   u_ref, small_ref, qt_ref, ko_ref, vt_ref, h_ref, *, tn, att_q, att_kv):
    _norm_mod(x_ref, g_ref, mod_ref, 0, h_ref)
    h = h_ref[...]
    hd = ATT_HEAD_DIM
    cos, sa, sb = cos_ref[...], sa_ref[...], sb_ref[...]

    def prep(y, g):
        y = y * lax.rsqrt(jnp.mean(y * y, axis=-1, keepdims=True) + EPS) * g
        return y * cos + pltpu.roll(y, hd - hd // 4, 1) * sa + pltpu.roll(y, hd // 4, 1) * sb

    q_scale = (hd ** -0.5) * math.log2(math.e)
    for j in range(att_q // tn):
        q = _dot(h, wa_ref[:, j * tn:(j + 1) * tn])
        for hh in range(tn // hd):
            qh = prep(q[:, hh * hd:(hh + 1) * hd], qg_ref[...]) * q_scale
            qt_ref[j * (tn // hd) + hh] = qh.T.astype(qt_ref.dtype)
    kv = _dot(h, wa_ref[:, att_q:att_q + 2 * att_kv])
    for g in range(att_kv // hd):
        sl = slice(g * hd, (g + 1) * hd)
        ko_ref[:, sl] = prep(kv[:, sl], kg_ref[...]).astype(ko_ref.dtype)
        vt_ref[g, 0:hd, :] = kv[:, att_kv + g * hd:att_kv + (g + 1) * hd].T.astype(vt_ref.dtype)
        vt_ref[g, hd:, :] = jnp.ones((ATT_ONES_ROWS, x_ref.shape[0]), vt_ref.dtype)

    small_ref[...] = _dot(h, ws_ref[...])
    a0 = att_q + 2 * att_kv
    na = wa_ref.shape[1] - a0
    for j in range(na // tn):
        u_ref[:, j * tn:(j + 1) * tn] = _dot(h, wa_ref[:, a0 + j * tn:a0 + (j + 1) * tn]).astype(u_ref.dtype)
    for j in range(wb_ref.shape[1] // tn):
        u_ref[:, na + j * tn:na + (j + 1) * tn] = _dot(h, wb_ref[:, j * tn:(j + 1) * tn]).astype(u_ref.dtype)


def _inproj(lay, X, mod, g, w_a, n_a, w_b, w_small, q_gain, k_gain, rope, l):
    D = w_a.shape[1]
    n_b = w_b.shape[2]
    tm = RESIDENT_TM
    assert tm == SEQ_BLOCK
    hd = ATT_HEAD_DIM
    att_q = D // 2
    att_kv = ATT_KV_HEADS * hd
    heads = att_q // hd
    n_u = n_a - att_q - 2 * att_kv + n_b
    B, NS = lay.B, lay.NS
    bs = lay.batch_seq
    tab = pl.BlockSpec((tm, hd), lambda i: (bs(i)[1], 0))
    return pl.pallas_call(
        functools.partial(_inproj_kernel, tn=DOT_COLS, att_q=att_q, att_kv=att_kv),
        out_shape=(jax.ShapeDtypeStruct((lay.R, n_u), BF16),
                   jax.ShapeDtypeStruct((lay.R, SMALL_COLS), F32),
                   jax.ShapeDtypeStruct((lay.R // tm, heads, hd, tm), BF16),
                   jax.ShapeDtypeStruct((B, NS * tm, att_kv), BF16),
                   jax.ShapeDtypeStruct((B, ATT_KV_HEADS, NS, hd + ATT_ONES_ROWS, tm), BF16),
                   jax.ShapeDtypeStruct((lay.R, D), BF16)),
        grid=(lay.R // tm,),
        in_specs=[
            pl.BlockSpec((tm, D), lambda i: (i, 0)),
            pl.BlockSpec((None, 1, N_MOD * D), lambda i: (lay.mod_row(i, tm), 0, 0)),
            _resident((1, D)),
            _resident((D, n_a), l),
            _resident((D, n_b), l),
            _resident((D, SMALL_COLS), l),
            _resident((1, hd)), _resident((1, hd)),
            tab, tab, tab,
        ],
        out_specs=(pl.BlockSpec((tm, n_u), lambda i: (i, 0)),
                   pl.BlockSpec((tm, SMALL_COLS), lambda i: (i, 0)),
                   pl.BlockSpec((None, heads, hd, tm), lambda i: (i, 0, 0, 0)),
                   pl.BlockSpec((None, tm, att_kv), lambda i: (*bs(i), 0)),
                   pl.BlockSpec((None, ATT_KV_HEADS, None, hd + ATT_ONES_ROWS, tm),
                                lambda i: (bs(i)[0], 0, bs(i)[1], 0, 0)),
                   pl.BlockSpec((tm, D), lambda i: (i, 0))),
        compiler_params=_cparams(("arbitrary",)),
        name="in_proj",
    )(X, mod, g, w_a, w_b, w_small, q_gain, k_gain, *rope)


def _slab_count(n_steps, *row_counts):
    n = 1
    while 2 * n <= n_steps and all(r % (2 * n * 16) == 0 for r in row_counts):
        n *= 2
    return n


def _merge_kernel(h_ref, oa_ref, og_ref, os_ref, wg_ref, wb_ref, *rest, tn, cast_next):
    if cast_next:
        c1_ref, c2_ref, m_ref, n1_ref, n2_ref = rest
        n1_ref[...] = c1_ref[...].astype(n1_ref.dtype)
        n2_ref[...] = c2_ref[...].astype(n2_ref.dtype)
    else:
        (m_ref,) = rest
    h = h_ref[...]
    D = m_ref.shape[1]
    for j in range(D // tn):
        acc = None
        for b, o_ref in enumerate((oa_ref, og_ref, os_ref)):
            gate = _dot(h, wg_ref[:, b * D + j * tn:b * D + (j + 1) * tn])
            t = jax.nn.sigmoid(gate) * _dot(o_ref[...], wb_ref[b, :, j * tn:(j + 1) * tn])
            acc = t if acc is None else acc + t
        m_ref[:, j * tn:(j + 1) * tn] = acc.astype(m_ref.dtype)


def _merge(lay, H, o_att, o_gla, o_ssd, w_gate, w_branch, l, skip_rows, cast_next=None):
    D = lay.D
    W = o_att.shape[1]
    tm = RESIDENT_TM
    assert tm == SEQ_BLOCK
    row0 = skip_rows // tm
    rows = lambda i: (i + row0, 0)
    seq_rows = pl.BlockSpec((None, tm, W), lambda i: (*lay.batch_seq(i + row0), 0))
    n_i = lay.R // tm - row0
    in_specs = [
        pl.BlockSpec((tm, D), rows),
        pl.BlockSpec((tm, W), rows), seq_rows, seq_rows,
        _resident(w_gate.shape[1:], l), _resident(w_branch.shape),
    ]
    args = [H, o_att, o_gla, o_ssd, w_gate, w_branch]
    out_shape = [jax.ShapeDtypeStruct((lay.R - skip_rows, D), BF16)]
    out_specs = [pl.BlockSpec((tm, D), lambda i: (i, 0))]
    if cast_next is not None:
        n_slabs = _slab_count(n_i, N_BRANCH * W, D)
        r1, r2 = N_BRANCH * W // n_slabs, D // n_slabs
        slab = lambda i: jnp.minimum(i, n_slabs - 1)
        in_specs += [pl.BlockSpec((None, r1, D), lambda i: (l + 1, slab(i), 0)),
                     pl.BlockSpec((None, r2, D), lambda i: (l + 1, slab(i), 0))]
        args += list(cast_next)
        out_shape += [jax.ShapeDtypeStruct((N_BRANCH * W, D), BF16), jax.ShapeDtypeStruct((D, D), BF16)]
        out_specs += [pl.BlockSpec((r1, D), lambda i: (slab(i), 0)), pl.BlockSpec((r2, D), lambda i: (slab(i), 0))]
    res = pl.pallas_call(
        functools.partial(_merge_kernel, tn=DOT_COLS, cast_next=cast_next is not None),
        out_shape=tuple(out_shape),
        grid=(n_i,),
        in_specs=in_specs,
        out_specs=tuple(out_specs),
        compiler_params=_cparams(("arbitrary",)),
        name="merge",
    )(*args)
    if cast_next is None:
        return res[0]
    return res[0], res[1].reshape(N_BRANCH, W, D), res[2]


def _mlp_kernel(x_ref, m_ref, wo_ref, mod_ref, g_ref, w1_ref, w2_ref, *rest, final, cast_next):
    rest = list(rest)
    fg_ref = rest.pop(0) if final else None
    if cast_next:
        c1_ref, c2_ref = rest.pop(0), rest.pop(0)
        o_ref, n1_ref, n2_ref, x1_ref, h_ref = rest
        n1_ref[...] = c1_ref[...].astype(n1_ref.dtype)
        n2_ref[...] = c2_ref[...].astype(n2_ref.dtype)
    else:
        o_ref, x1_ref, h_ref = rest
    k = pl.program_id(1)
    D = x_ref.shape[-1]

    @pl.when(k == 0)
    def _():
        tn = DOT_COLS
        for j in range(D // tn):
            sl = slice(j * tn, (j + 1) * tn)
            x1_ref[:, sl] = x_ref[:, sl] + mod_ref[:, 2 * D + j * tn:2 * D + (j + 1) * tn] * _dot(
                m_ref[...], wo_ref[:, sl])
        _norm_mod(x1_ref, g_ref, mod_ref, 3, h_ref)
        o_ref[...] = jnp.zeros_like(o_ref)

    f = jnp.square(jnp.maximum(_dot(h_ref[...], w1_ref[...]), 0.0)).astype(BF16)
    o_ref[...] += _dot(f, w2_ref[...])

    @pl.when(k == pl.num_programs(1) - 1)
    def _():
        gate = mod_ref[:, 5 * D:6 * D]
        rc = ROW_CHUNK

        def body(r, carry):
            sl = pl.ds(pl.multiple_of(r * rc, rc), rc)
            y = x1_ref[sl, :] + gate * o_ref[sl, :]
            if final:
                ms = jnp.mean(y * y, axis=-1, keepdims=True)
                y = y * lax.rsqrt(ms + EPS) * fg_ref[...]
            o_ref[sl, :] = y
            return carry

        lax.fori_loop(0, x_ref.shape[0] // rc, body, 0)


def _mlp(lay, X, m, w_out, mod, g, w1, w2, l, skip_rows, final_gain=None, cast_next=None):
    D, FF = w1.shape
    tm, tf = lay.tm, MLP_FF_TILE
    row0 = skip_rows // tm
    final = final_gain is not None
    out_rows = lay.R - skip_rows if final else lay.R
    out_off = 0 if final else row0
    rows = lambda i, k: (i + row0, 0)
    n_i, n_k = lay.R // tm - row0, FF // tf
    assert m.shape[0] == lay.R - skip_rows
    in_specs = [
        pl.BlockSpec((tm, D), rows),
        pl.BlockSpec((tm, D), lambda i, k: (i, 0)),
        _resident((D, D)),
        pl.BlockSpec((None, 1, N_MOD * D), lambda i, k: (lay.mod_row(i + row0, tm), 0, 0)),
        _resident((1, D)),
        pl.BlockSpec((D, tf), lambda i, k: (0, k)),
        pl.BlockSpec((tf, D), lambda i, k: (k, 0)),
    ]
    args = [X, m, w_out, mod, g, w1, w2]
    out_shape = [jax.ShapeDtypeStruct((out_rows, D), F32)]
    out_specs = [pl.BlockSpec((tm, D), lambda i, k: (i + out_off, 0))]
    if final:
        in_specs.append(_resident((1, D)))
        args.append(final_gain)
    if cast_next is not None:
        n_slabs = _slab_count(n_i * n_k, D, FF)
        r1, r2 = D // n_slabs, FF // n_slabs
        slab = lambda i, k: jnp.minimum(i * n_k + k, n_slabs - 1)
        in_specs += [pl.BlockSpec((None, r1, FF), lambda i, k: (l + 1, slab(i, k), 0)),
                     pl.BlockSpec((None, r2, D), lambda i, k: (l + 1, slab(i, k), 0))]
        args += list(cast_next)
        out_shape += [jax.ShapeDtypeStruct((D, FF), BF16), jax.ShapeDtypeStruct((FF, D), BF16)]
        out_specs += [pl.BlockSpec((r1, FF), lambda i, k: (slab(i, k), 0)),
                      pl.BlockSpec((r2, D), lambda i, k: (slab(i, k), 0))]
    res = pl.pallas_call(
        functools.partial(_mlp_kernel, final=final, cast_next=cast_next is not None),
        out_shape=tuple(out_shape),
        grid=(n_i, n_k),
        in_specs=in_specs,
        out_specs=tuple(out_specs),
        scratch_shapes=[pltpu.VMEM((tm, D), F32), pltpu.VMEM((tm, D), BF16)],
        compiler_params=_cparams(("arbitrary", "arbitrary")),
        name="mlp",
    )(*args)
    return res if cast_next is not None else res[0]


def _attn_kernel(qt_ref, k_ref, vt_ref, o_ref, m_ref, acc_ref, s0_ref, s1_ref, *, n_ctx_blocks):
    n = pl.program_id(2)
    heads, hd, tq = qt_ref.shape
    nblk, vrows, tk = vt_ref.shape
    n_kv = jnp.where(n < n_ctx_blocks, n_ctx_blocks, nblk)
    m_ref[...] = jnp.full(m_ref.shape, -1e30, F32)
    acc_ref[...] = jnp.zeros(acc_ref.shape, F32)

    def scores(c, s_ref):
        c = jnp.minimum(c, n_kv - 1)
        k = k_ref[pl.ds(pl.multiple_of(c * tk, tk), tk), :]
        for h in range(heads):
            s_ref[h] = _dot(k, qt_ref[h])

    def update(c, s_ref):
        vt = vt_ref[c]
        for h in range(heads):
            st = s_ref[h]
            m_old = m_ref[h]
            m_new = jnp.maximum(m_old, jnp.max(st, axis=0, keepdims=True))
            pt = jnp.exp2((st - m_new).astype(BF16))
            acc_ref[h] = jnp.exp2(m_old - m_new) * acc_ref[h] + _dot(vt, pt)
            m_ref[h] = m_new

    scores(0, s0_ref)

    def pairs(c0, n_pairs):
        for j in range(n_pairs):
            scores(c0 + 2 * j + 1, s1_ref)
            update(c0 + 2 * j, s0_ref)
            scores(c0 + 2 * j + 2, s0_ref)
            update(c0 + 2 * j + 1, s1_ref)

    n_pairs = n_kv // 2
    unroll = ATT_PAIR_UNROLL

    def body(i, carry):
        pairs(2 * unroll * i, unroll)
        return carry

    lax.fori_loop(0, n_pairs // unroll, body, 0)
    done = (n_pairs // unroll) * unroll
    piece = unroll // 2
    while piece >= 1:
        @pl.when(((n_pairs - done) & piece) != 0)
        def _(done=done, piece=piece):
            pairs(2 * done, piece)

        done = done + ((n_pairs - done) & piece)
        piece //= 2

    @pl.when(n_kv % 2 == 1)
    def _():
        update(n_kv - 1, s0_ref)

    for h in range(heads):
        acc = acc_ref[h]
        ot = acc[0:hd] / acc[hd:hd + 1]
        o_ref[:, h * hd:(h + 1) * hd] = ot.T.astype(o_ref.dtype)


def _attention(lay, qt, kk, vt):
    B, kvh, NS, vrows, blk = vt.shape
    _, heads, hd, _ = qt.shape
    group = heads // kvh
    return pl.pallas_call(
        functools.partial(_attn_kernel, n_ctx_blocks=lay.NC),
        out_shape=jax.ShapeDtypeStruct((lay.R, heads * hd), BF16),
        grid=(B, kvh, NS),
        in_specs=[
            pl.BlockSpec((None, group, hd, blk), lambda b, g, n: (lay.row_block(b, n), g, 0, 0)),
            pl.BlockSpec((None, NS * blk, hd), lambda b, g, n: (b, 0, g)),
            pl.BlockSpec((None, None, NS, vrows, blk), lambda b, g, n: (b, g, 0, 0, 0)),
        ],
        out_specs=pl.BlockSpec((blk, group * hd), lambda b, g, n: (lay.row_block(b, n), g)),
        scratch_shapes=[pltpu.VMEM((group, 1, blk), F32), pltpu.VMEM((group, vrows, blk), F32),
                        pltpu.VMEM((group, blk, blk), F32), pltpu.VMEM((group, blk, blk), F32)],
        compiler_params=_cparams(("arbitrary", "arbitrary", "arbitrary")),
        name="attention",
    )(qt, kk, vt)


def _gla_kernel(*refs, nb, rev, final, dk, dv):
    per_b = 5 if final else 4
    seqs = [refs[i * per_b:(i + 1) * per_b] for i in range(nb)]
    rest = refs[nb * per_b:]
    if final:
        wd_ref, bd_ref, ng_ref, prev_ref, o_ref, st_ref = rest
    else:
        wd_ref, bd_ref, o_ref, st_ref = rest
    blk = seqs[0][0].shape[0]
    heads = seqs[0][0].shape[1] // dk
    C = GLA_CHUNK

    @pl.when(pl.program_id(0) == 0)
    def _():
        st_ref[...] = jnp.zeros_like(st_ref)

    tri = _block_tri(blk, C, rev)
    bcums = []
    for bi in range(nb):
        z = _dot(seqs[bi][3][...].astype(BF16), wd_ref[...]) + bd_ref[...]
        la = (jnp.minimum(z, 0.0) - _log1p_exp_neg_abs(z)) * (1.0 / GLA_TAU)
        bcums.append(_dot_sum(tri, la))

    ri = lax.broadcasted_iota(jnp.int32, (C, C), 0)
    ci = lax.broadcasted_iota(jnp.int32, (C, C), 1)
    causal = (ci >= ri) if rev else (ci <= ri)

    n_chunks = blk // C
    units = [(h, bi) for h in range(heads) for bi in range(nb)]

    def independent(step):
        c = n_chunks - 1 - step if rev else step
        rs = slice(c * C, (c + 1) * C)
        last = c * C if rev else (c + 1) * C - 1
        scores, inter = [], []
        for h, bi in units:
            ks = slice(h * dk, (h + 1) * dk)
            q_ref, k_ref, v_ref = seqs[bi][:3]
            b = bcums[bi][rs, ks]
            b_last = bcums[bi][last:last + 1, ks]
            q = q_ref[rs, ks].astype(F32) * (dk ** -0.5)
            k = k_ref[rs, ks].astype(F32)
            q_dec = (q * jnp.exp(b)).astype(BF16)
            k_dec = (k * jnp.exp(-b)).astype(BF16)
            k_st = (k * jnp.exp(b_last - b)).astype(BF16)
            st = st_ref[bi * heads + h]
            scores.append(_dot_nt(q_dec, k_dec))
            inter.append(_dot_nt(q_dec, st.astype(BF16)))
            st_ref[bi * heads + h] = st * jnp.exp(b_last) + _dot_tn(v_ref[rs, h * dv:(h + 1) * dv], k_st)
        return rs, scores, inter

    def dependent(rs, scores, inter):
        for (h, bi), sc, it in zip(units, scores, inter):
            vs = slice(h * dv, (h + 1) * dv)
            att = jnp.where(causal, sc, 0.0).astype(BF16)
            o = _dot(att, seqs[bi][2][rs, vs]) + it
            if final:
                tot = o + prev_ref[bi, rs, vs]
                y = tot * lax.rsqrt(jnp.mean(tot * tot, axis=-1, keepdims=True) + EPS) * ng_ref[...]
                o_ref[bi, rs, vs] = (y * _silu(seqs[bi][4][rs, vs].astype(F32))).astype(o_ref.dtype)
            else:
                o_ref[bi, rs, vs] = o

    held = independent(0)
    for step in range(1, n_chunks):
        nxt = independent(step)
        dependent(*held)
        held = nxt
    dependent(*held)


def _gla_dir(lay, U, small, cols, w_dec, b_dec, rev, norm_g=None, prev=None):
    B, NS, blk = lay.B, lay.NS, SEQ_BLOCK
    qk_w = cols["gla_q"][1]
    v_w = cols["gla_v"][1]
    dk, dv = qk_w // GLA_HEADS, v_w // GLA_HEADS
    final = prev is not None
    const = lambda shape: pl.BlockSpec(shape, lambda n: (0,) * len(shape))
    seq = lambda n: lay.seq_order(n, rev)

    def ucol(name, b):
        off, w = cols[name]
        return pl.BlockSpec((blk, w), lambda n: (lay.row_block(b, seq(n)), off // w))

    in_specs, args = [], []
    for b in range(B):
        in_specs += [ucol("gla_q", b), ucol("gla_k", b), ucol("gla_v", b),
                     pl.BlockSpec((blk, SMALL_COLS), lambda n, b=b: (lay.row_block(b, seq(n)), 0))]
        args += [U, U, U, small]
        if final:
            in_specs.append(ucol("gla_gate", b))
            args.append(U)
    in_specs += [const((SMALL_COLS, qk_w)), const((1, qk_w))]
    args += [w_dec, b_dec]
    per_batch = pl.BlockSpec((B, blk, v_w), lambda n: (0, seq(n), 0))
    if final:
        in_specs += [const((1, dv)), per_batch]
        args += [norm_g, prev]
    return pl.pallas_call(
        functools.partial(_gla_kernel, nb=B, rev=rev, final=final, dk=dk, dv=dv),
        out_shape=jax.ShapeDtypeStruct((B, NS * blk, v_w), BF16 if final else F32),
        grid=(NS,),
        in_specs=in_specs,
        out_specs=per_batch,
        scratch_shapes=[pltpu.VMEM((B * GLA_HEADS, dv, dk), F32)],
        compiler_params=_cparams(("arbitrary",)),
        name="gla_bwd" if rev else "gla_fwd",
    )(*args)


def _conv_kernel(x_ref, xp_ref, xn_ref, bc_ref, bcp_ref, bcn_ref, wx_ref, bx_ref, wbc_ref, bbc_ref,
                 xo_ref, bco_ref, *, n_ctx_total, NC, NL):
    i = pl.program_id(0)
    in_ctx = i < n_ctx_total
    p = jnp.where(in_ctx, i % NC, jnp.maximum(i - n_ctx_total, 0) % NL)
    seg = jnp.where(in_ctx, NC, NL)
    has_left = (p > 0).astype(F32)
    has_right = (p < seg - 1).astype(F32)
    blk = x_ref.shape[0]
    pad = SSD_CONV // 2
    edge = 8
    assert pad <= edge <= HALO_ROWS

    r = lax.broadcasted_iota(jnp.int32, (blk, blk), 0)
    c = lax.broadcasted_iota(jnp.int32, (blk, blk), 1)
    re = lax.broadcasted_iota(jnp.int32, (edge, HALO_ROWS), 0)
    ce = lax.broadcasted_iota(jnp.int32, (edge, HALO_ROWS), 1)
    taps = [j - pad for j in range(SSD_CONV)]
    shift = {s: jnp.where(c == r + s, 1.0, 0.0).astype(BF16) for s in taps if s != 0}
    head = {s: (jnp.where(ce == HALO_ROWS + re + s, 1.0, 0.0) * has_left).astype(BF16) for s in taps if s < 0}
    tail = {s: (jnp.where(ce == re + s - edge, 1.0, 0.0) * has_right).astype(BF16) for s in taps if s > 0}

    def run(m_ref, p_ref, n_ref, w_ref, b_ref, o_ref):
        main = m_ref[...]
        acc = b_ref[...] + w_ref[pad:pad + 1, :] * main.astype(F32)
        for j, s in enumerate(taps):
            if s != 0:
                acc = acc + w_ref[j:j + 1, :] * _dot(shift[s], main)
        first, last = acc[0:edge], acc[blk - edge:blk]
        for j, s in enumerate(taps):
            if s < 0:
                first = first + w_ref[j:j + 1, :] * _dot(head[s], p_ref[...])
            if s > 0:
                last = last + w_ref[j:j + 1, :] * _dot(tail[s], n_ref[...])
        o_ref[0:edge, :] = _silu(first).astype(o_ref.dtype)
        o_ref[edge:blk - edge, :] = _silu(acc[edge:blk - edge]).astype(o_ref.dtype)
        o_ref[blk - edge:blk, :] = _silu(last).astype(o_ref.dtype)

    run(x_ref, xp_ref, xn_ref, wx_ref, bx_ref, xo_ref)
    run(bc_ref, bcp_ref, bcn_ref, wbc_ref, bbc_ref, bco_ref)


def _conv(lay, U, cols, wx, bx, wbc, bbc):
    blk = SEQ_BLOCK
    nb = lay.R // blk
    hb = blk // HALO_ROWS
    nh = lay.R // HALO_ROWS
    xw = cols["ssd_x"][1]
    bcw = cols["ssd_bc"][1]
    xc, bcc = cols["ssd_x"][0] // xw, cols["ssd_bc"][0] // bcw
    prev = lambda i: jnp.maximum(i * hb - 1, 0)
    nxt = lambda i: jnp.minimum((i + 1) * hb, nh - 1)
    const = lambda shape: pl.BlockSpec(shape, lambda i: (0,) * len(shape))
    return pl.pallas_call(
        functools.partial(_conv_kernel, n_ctx_total=lay.B * lay.NC, NC=lay.NC, NL=lay.NL),
        out_shape=(jax.ShapeDtypeStruct((lay.R, xw), BF16), jax.ShapeDtypeStruct((lay.R, bcw), BF16)),
        grid=(nb,),
        in_specs=[
            pl.BlockSpec((blk, xw), lambda i: (i, xc)),
            pl.BlockSpec((HALO_ROWS, xw), lambda i: (prev(i), xc)),
            pl.BlockSpec((HALO_ROWS, xw), lambda i: (nxt(i), xc)),
            pl.BlockSpec((blk, bcw), lambda i: (i, bcc)),
            pl.BlockSpec((HALO_ROWS, bcw), lambda i: (prev(i), bcc)),
            pl.BlockSpec((HALO_ROWS, bcw), lambda i: (nxt(i), bcc)),
            const((SSD_CONV, xw)), const((1, xw)), const((SSD_CONV, bcw)), const((1, bcw)),
        ],
        out_specs=(pl.BlockSpec((blk, xw), lambda i: (i, 0)), pl.BlockSpec((blk, bcw), lambda i: (i, 0))),
        compiler_params=_cparams(("arbitrary",)),
        name="ssd_conv",
    )(U, U, U, U, U, U, wx, bx, wbc, bbc)


def _ssd_kernel(*refs, nb, rev, final):
    per_b = 4 if final else 3
    seqs = [refs[i * per_b:(i + 1) * per_b] for i in range(nb)]
    rest = refs[nb * per_b:]
    if final:
        dtb_ref, apad_ref, e_ref, dsk_ref, ng_ref, prev_ref, o_ref, st_ref, y_ref = rest
    else:
        dtb_ref, apad_ref, e_ref, o_ref, st_ref = rest
        y_ref = o_ref
    blk, inner = seqs[0][0].shape
    N = SSD_STATE
    P = SSD_HEAD_DIM
    C = SSD_CHUNK
    gw = inner // SSD_GROUPS
    pair = 2 * P

    @pl.when(pl.program_id(0) == 0)
    def _():
        st_ref[...] = jnp.zeros_like(st_ref)

    tri = _block_tri(blk, C, rev)
    e = e_ref[...]
    dt_fulls, cs_fulls = [], []
    for bi in range(nb):
        dt16 = _softplus(seqs[bi][2][...] + dtb_ref[...])
        cs16 = _dot_sum(tri, dt16 * apad_ref[...])
        dt_fulls.append(_dot_spread(dt16, e))
        cs_fulls.append(_dot_spread(cs16, e))

    ri = lax.broadcasted_iota(jnp.int32, (C, pair), 0)
    ci = lax.broadcasted_iota(jnp.int32, (C, pair), 1)
    cm = jnp.where(ci >= P, ci - P, ci)
    eye2 = cm == ri
    causal2 = (cm >= ri) if rev else (cm <= ri)
    left = ci < P
    zero = jnp.zeros((C, pair), BF16)

    n_chunks = blk // C
    for step in range(n_chunks):
        c = n_chunks - 1 - step if rev else step
        rs = slice(c * C, (c + 1) * C)
        last = c * C if rev else (c + 1) * C - 1
        units = [(g, bi) for g in range(SSD_GROUPS) for bi in range(nb)]
        held = []
        for g, bi in units:
            gs = slice(g * gw, (g + 1) * gw)
            x_ref, bc_ref = seqs[bi][:2]
            bm = bc_ref[rs, g * N:(g + 1) * N]
            cmat = bc_ref[rs, SSD_GROUPS * N + g * N:SSD_GROUPS * N + (g + 1) * N]
            cb2 = _dot_nt(cmat, jnp.concatenate([bm, bm], axis=0))
            cs_g = cs_fulls[bi][rs, gs]
            cs_last = cs_fulls[bi][last:last + 1, gs]
            xdt = x_ref[rs, gs].astype(F32) * dt_fulls[bi][rs, gs]
            st = st_ref[bi * SSD_GROUPS + g]
            y_inter = _dot(cmat, st.astype(BF16)) * jnp.exp(cs_g)
            w = (xdt * jnp.exp(cs_last - cs_g)).astype(BF16)
            st_ref[bi * SSD_GROUPS + g] = st * jnp.exp(cs_last) + _dot_tn(bm, w)
            held.append((cb2, cs_g, y_inter, xdt.astype(BF16)))
        for (g, bi), (cb2, cs_g, y_inter, xdt_b) in zip(units, held):
            x_ref = seqs[bi][0]
            for pr in range(gw // pair):
                ps = slice(pr * pair, (pr + 1) * pair)
                col = cs_g[:, ps]
                rowv = jnp.sum(jnp.where(eye2, col, 0.0), axis=0, keepdims=True)
                lmat = jnp.exp(jnp.where(causal2, col - rowv, -jnp.inf))
                mm = (lmat * cb2).astype(BF16)
                xp = xdt_b[:, ps]
                rhs = jnp.concatenate([jnp.where(left, xp, zero), jnp.where(left, zero, xp)], axis=0)
                y = _dot(mm, rhs) + y_inter[:, ps]
                os_ = slice(g * gw + pr * pair, g * gw + (pr + 1) * pair)
                if final:
                    y_ref[bi, rs, os_] = (y + prev_ref[bi, rs, os_]
                                          + x_ref[rs, os_].astype(F32) * dsk_ref[:, os_])
                else:
                    y_ref[bi, rs, os_] = y

    if final:
        rc = ROW_CHUNK
        for bi in range(nb):
            z_ref = seqs[bi][3]
            for r in range(blk // rc):
                rs = slice(r * rc, (r + 1) * rc)
                for g in range(SSD_GROUPS):
                    gs = slice(g * gw, (g + 1) * gw)
                    yg = y_ref[bi, rs, gs] * _silu(z_ref[rs, gs].astype(F32))
                    ms = jnp.mean(yg * yg, axis=-1, keepdims=True)
                    o_ref[bi, rs, gs] = (yg * lax.rsqrt(ms + EPS) * ng_ref[:, gs]).astype(o_ref.dtype)


def _ssd_dir(lay, xc, bcc, small, U, cols, dtb, apad, expand, rev, dsk=None, norm_g=None, prev=None):
    B, NS, blk = lay.B, lay.NS, SEQ_BLOCK
    inner = xc.shape[1]
    bcw = bcc.shape[1]
    final = prev is not None
    seq = lambda n: lay.seq_order(n, rev)
    const = lambda shape: pl.BlockSpec(shape, lambda n: (0,) * len(shape))

    def rows(w, b, col=0):
        return pl.BlockSpec((blk, w), lambda n: (lay.row_block(b, seq(n)), col))

    in_specs, args = [], []
    for b in range(B):
        in_specs += [rows(inner, b), rows(bcw, b), rows(SMALL_COLS, b)]
        args += [xc, bcc, small]
        if final:
            zc = cols["ssd_z"]
            in_specs.append(rows(inner, b, zc[0] // zc[1]))
            args.append(U)
    in_specs += [const((1, SMALL_COLS)), const((1, SMALL_COLS)), const((SMALL_COLS, inner))]
    args += [dtb, apad, expand]
    per_batch = pl.BlockSpec((B, blk, inner), lambda n: (0, seq(n), 0))
    if final:
        in_specs += [const((1, inner)), const((1, inner)), per_batch]
        args += [dsk, norm_g, prev]
    return pl.pallas_call(
        functools.partial(_ssd_kernel, nb=B, rev=rev, final=final),
        out_shape=jax.ShapeDtypeStruct((B, NS * blk, inner), BF16 if final else F32),
        grid=(NS,),
        in_specs=in_specs,
        out_specs=per_batch,
        scratch_shapes=[pltpu.VMEM((B * SSD_GROUPS, SSD_STATE, inner // SSD_GROUPS), F32)]
        + ([pltpu.VMEM((B, blk, inner), F32)] if final else []),
        compiler_params=_cparams(("arbitrary",)),
        name="ssd_bwd" if rev else "ssd_fwd",
    )(*args)


def _rope_tables(CTX, SEQ):
    hd = ATT_HEAD_DIM
    pairs = hd // 4
    rows = SEQ // GRID_W
    row = jnp.repeat(jnp.arange(rows, dtype=F32), GRID_W)
    col = jnp.tile(jnp.arange(GRID_W, dtype=F32), rows)
    inv = jnp.exp(-math.log(ROPE_THETA) * jnp.arange(pairs, dtype=F32) / pairs)
    ar = row[:, None] * inv
    ac = col[:, None] * inv
    ang = jnp.concatenate([ar, ar, ac, ac], axis=-1)
    cos, sin = jnp.cos(ang), jnp.sin(ang)
    first = (jnp.arange(hd) % (2 * pairs)) < pairs
    sa = jnp.where(first, -sin, 0.0)
    sb = jnp.where(first, 0.0, sin)
    ones = jnp.ones((CTX, hd), F32)
    zeros = jnp.zeros((CTX, hd), F32)
    return (jnp.concatenate([ones, cos]), jnp.concatenate([zeros, sa]), jnp.concatenate([zeros, sb]))


def _column_plan(D):
    mix = D // 2
    gla_v = mix
    gla_qk = mix // 2
    ssd_inner = mix
    ssd_bc = 2 * SSD_GROUPS * SSD_STATE
    order = [("gla_q", gla_qk), ("gla_k", gla_qk), ("gla_v", gla_v), ("gla_gate", gla_v),
             ("ssd_z", ssd_inner), ("ssd_x", ssd_inner), ("ssd_bc", ssd_bc)]
    cols, off = {}, 0
    for name, w in order:
        assert off % w == 0
        cols[name] = (off, w)
        off += w
    return cols, off


def _source_columns(D):
    mix = D // 2
    att_kv = ATT_KV_HEADS * ATT_HEAD_DIM
    ssd_bc = SSD_GROUPS * SSD_STATE
    heads = mix // SSD_HEAD_DIM
    widths = [("att_q", mix), ("att_k", att_kv), ("att_v", att_kv), ("gla_q", mix // 2), ("gla_k", mix // 2),
              ("gla_v", mix), ("gla_gate", mix), ("gla_lr", 2 * GLA_LOWRANK), ("ssd_z", mix),
              ("ssd_x", mix), ("ssd_bc", 2 * ssd_bc), ("ssd_dt", 2 * heads), ("gates", N_BRANCH * D)]
    src, off = {}, 0
    for name, w in widths:
        src[name] = (off, w)
        off += w
    return src, off


def kernel(x, c, ctx, c_ctx, norm1, norm2, w_ada, b_ada, w_in, att_q_norm, att_k_norm, gla_w_decay,
           gla_b_decay, gla_norm, ssd_conv_w, ssd_conv_b, ssd_dt_bias, ssd_a_log, ssd_d, ssd_norm,
           w_branch, w_out, w_ff1, w_ff2, final_norm):
    B, SEQ, D = x.shape
    CTX = ctx.shape[1]
    depth = w_in.shape[0]
    lay = _Layout(B, CTX, SEQ, D)
    cols, n_main = _column_plan(D)
    src, n_src = _source_columns(D)
    assert n_src == w_in.shape[2]
    mix = D // 2
    ssd_heads = mix // SSD_HEAD_DIM
    lr_w = 2 * GLA_LOWRANK
    assert lr_w + 2 * ssd_heads <= SMALL_COLS

    X = jnp.concatenate([ctx.reshape(B * CTX, D), x.reshape(B * SEQ, D)], axis=0)
    cvec = jnp.concatenate([c, c_ctx[None, :], jnp.zeros((MOD_ROWS - B - 1, D), F32)], axis=0)
    mods = _ada(cvec, w_ada, b_ada).reshape(depth, MOD_ROWS, 1, N_MOD * D)
    rope = _rope_tables(CTX, SEQ)

    expand = np.zeros((2, SMALL_COLS, mix), np.float32)
    for d in range(2):
        for h in range(ssd_heads):
            expand[d, lr_w + d * ssd_heads + h, h * SSD_HEAD_DIM:(h + 1) * SSD_HEAD_DIM] = 1.0
    expand = jnp.asarray(expand, BF16)

    lead = ["att_q", "att_k", "att_v", "gla_q", "gla_k", "gla_v", "gla_gate"]
    assert [src[n][0] for n in lead] == list(np.cumsum([0] + [src[n][1] for n in lead[:-1]]))
    assert [cols[n][0] for n in lead[3:]] == [src[n][0] - src["gla_q"][0] for n in lead[3:]]
    n_a = src["gla_gate"][0] + src["gla_gate"][1]
    w_in_b = w_in.astype(BF16)
    take = lambda name: w_in_b[:, :, src[name][0]:src[name][0] + src[name][1]]
    w_b = jnp.concatenate([take("ssd_z"), take("ssd_x"), take("ssd_bc")], axis=2)
    assert cols["ssd_z"][0] == n_a - src["gla_q"][0] and cols["ssd_x"][0] == cols["ssd_z"][0] + mix
    w_small = jnp.concatenate([take("gla_lr"), take("ssd_dt"),
                               jnp.zeros((depth, D, SMALL_COLS - lr_w - 2 * ssd_heads), BF16)], axis=2)
    w_gate = take("gates")
    wbr_b, wo_b = w_branch[0].astype(BF16), w_out[0].astype(BF16)
    w1_b, w2_b = w_ff1[0].astype(BF16), w_ff2[0].astype(BF16)
    w_branch_rows = w_branch.reshape(depth, N_BRANCH * mix, D)

    out = None
    for l in range(depth):
        mod = mods[l]
        g1 = norm1[l][None, :]

        U, small, qt, kk, vt, H = _inproj(lay, X, mod, g1, w_in_b, n_a, w_b, w_small, att_q_norm[l][None, :],
                                          att_k_norm[l][None, :], rope, l)

        o_att = _attention(lay, qt, kk, vt)

        gla_out = None
        for d in range(2):
            wd = jnp.zeros((SMALL_COLS, mix // 2), F32).at[d * GLA_LOWRANK:(d + 1) * GLA_LOWRANK].set(
                gla_w_decay[l, d]).astype(BF16)
            bd = gla_b_decay[l, d][None, :]
            if d == 0:
                gla_out = _gla_dir(lay, U, small, cols, wd, bd, rev=False)
            else:
                gla_out = _gla_dir(lay, U, small, cols, wd, bd, rev=True,
                                   norm_g=gla_norm[l][None, :], prev=gla_out)
        o_gla = gla_out

        cw, cbias = ssd_conv_w[l], ssd_conv_b[l]
        xc, bcc = _conv(lay, U, cols, cw[:, :mix], cbias[None, :mix], cw[:, mix:], cbias[None, mix:])
        ssd_out = None
        for d in range(2):
            lanes = slice(lr_w + d * ssd_heads, lr_w + (d + 1) * ssd_heads)
            dtb = jnp.zeros((1, SMALL_COLS), F32).at[0, lanes].set(ssd_dt_bias[l, d])
            apad = jnp.zeros((1, SMALL_COLS), F32).at[0, lanes].set(-jnp.exp(ssd_a_log[l, d]))
            if d == 0:
                ssd_out = _ssd_dir(lay, xc, bcc, small, U, cols, dtb, apad, expand[d], rev=False)
            else:
                ssd_out = _ssd_dir(lay, xc, bcc, small, U, cols, dtb, apad, expand[d], rev=True,
                                   dsk=jnp.repeat(ssd_d[l], SSD_HEAD_DIM)[None, :],
                                   norm_g=ssd_norm[l][None, :], prev=ssd_out)
        o_ssd = ssd_out

        last = l == depth - 1
        skip = B * CTX if last else 0
        if last:
            m = _merge(lay, H, o_att, o_gla, o_ssd, w_gate, wbr_b, l, skip)
            out = _mlp(lay, X, m, wo_b, mod, norm2[l][None, :], w1_b, w2_b, l, skip,
                       final_gain=final_norm[None, :])
        else:
            m, wbr_next, wo_next = _merge(lay, H, o_att, o_gla, o_ssd, w_gate, wbr_b, l, skip,
                                          cast_next=(w_branch_rows, w_out))
            X, w1_b, w2_b = _mlp(lay, X, m, wo_b, mod, norm2[l][None, :], w1_b, w2_b, l, skip,
                                 cast_next=(w_ff1, w_ff2))
            wbr_b, wo_b = wbr_next, wo_next
    return out.reshape(B, SEQ, D)
```
